```python
import jax
import jax.numpy as jnp
from jax import lax
import numpy as np

D_MODEL = 1024
BATCH = 8
SEQ = 2048
DEPTH = 4
DEC_BATCH = 128
DEC_SEQ = 8
PAST_LEN = 16384
PAGE_SIZE = 128

F32 = jnp.float32
N_MIXERS = 3
N_A = (DEPTH + 2) // N_MIXERS
N_B = (DEPTH + 1) // N_MIXERS
N_C = DEPTH // N_MIXERS
EXPAND = 2
BRANCH = EXPAND * D_MODEL
CONV_W = 4
EPS = 1e-6

LRU_WIDTH = BRANCH
LRU_BLOCKS = 8
LRU_BLOCK = LRU_WIDTH // LRU_BLOCKS
LRU_C = 8.0

SSD_HEAD_DIM = 64
SSD_HEADS = BRANCH // SSD_HEAD_DIM
SSD_STATE = 128
SSD_GROUPS = 8
SSD_CONV_DIM = BRANCH + 2 * SSD_GROUPS * SSD_STATE
SSD_CHUNK = 128

HGRN_KEY_DIM = 128
HGRN_HEADS = BRANCH // HGRN_KEY_DIM
HGRN_VAL_DIM = BRANCH // HGRN_HEADS
HGRN_CHUNK = 16

N_MEM = 256
X_HEADS = 4
X_HEAD_DIM = D_MODEL // X_HEADS

kernel_name = 'hybrid_rglru_ssd_hgrn2_memxattn_step'


def rms_norm(x, g):
    xf = x.astype(F32)
    y = xf * lax.rsqrt(jnp.mean(xf * xf, axis=-1, keepdims=True) + EPS)
    return (y * g.astype(F32)).astype(x.dtype)


def causal_conv(u, state, w, b):
    L = u.shape[1]
    uu = jnp.concatenate([state.astype(u.dtype), u], axis=1)
    y = b + w[0] * uu[:, 0:L]
    for k in range(1, CONV_W):
        y = y + w[k] * uu[:, k:k + L]
    return y, uu[:, L:]


def linear_scan(a, b):
    def combine(left, right):
        a_l, b_l = left
        a_r, b_r = right
        return a_l * a_r, a_r * b_l + b_r
    _, h = lax.associative_scan(combine, (a, b), axis=1)
    return h


def pad_seq(t, pad):
    if pad == 0:
        return t
    widths = [(0, 0)] * t.ndim
    widths[1] = (0, pad)
    return jnp.pad(t, widths)


def rglru_mixer(x, conv_state, h0, w_in, conv_w, conv_b, w_a, b_a, w_x, b_x, lam, w_out):
    bsz, L, _ = x.shape
    u, gate = jnp.split(x @ w_in, 2, axis=-1)
    u, new_conv = causal_conv(u, conv_state, conv_w, conv_b)
    ub = u.reshape(bsz, L, LRU_BLOCKS, LRU_BLOCK)
    r = jax.nn.sigmoid((jnp.einsum('blhi,hij->blhj', ub, w_a).reshape(bsz, L, LRU_WIDTH) + b_a).astype(F32))
    i = jax.nn.sigmoid((jnp.einsum('blhi,hij->blhj', ub, w_x).reshape(bsz, L, LRU_WIDTH) + b_x).astype(F32))
    log_a = -LRU_C * r * jax.nn.softplus(-lam.astype(F32))
    a = jnp.exp(log_a)
    b = jnp.sqrt(-jnp.expm1(2.0 * log_a)) * (i * u.astype(F32))
    b = b.at[:, 0].add(a[:, 0] * h0.astype(F32))
    h = linear_scan(a, b)
    y = (h.astype(x.dtype) * jax.nn.silu(gate)) @ w_out
    return y, new_conv, h[:, -1].astype(x.dtype)


def ssd_chunked(xs, dt, a, bm, cm, s0):
    bsz, L, H, P = xs.shape
    G, N = SSD_GROUPS, SSD_STATE
    J = H // G
    Q = min(SSD_CHUNK, L)
    pad = (-L) % Q
    xs, dt, bm, cm = (pad_seq(t, pad) for t in (xs, dt, bm, cm))
    nc = (L + pad) // Q
    x = xs.reshape(bsz, nc, Q, G, J, P)
    dt = dt.reshape(bsz, nc, Q, G, J)
    bc = bm.reshape(bsz, nc, Q, G, N)
    cc = cm.reshape(bsz, nc, Q, G, N)
    acs = jnp.cumsum(dt * a.reshape(G, J), axis=2)
    xdt = x * dt[..., None]
    causal = jnp.tril(jnp.ones((Q, Q), dtype=bool))[:, :, None, None]
    seg = acs[:, :, :, None] - acs[:, :, None, :]
    decay = jnp.exp(jnp.where(causal, seg, -jnp.inf))
    cb = jnp.einsum('bclgn,bcsgn->bclsg', cc, bc)
    y_diag = jnp.einsum('bclsg,bclsgj,bcsgjp->bclgjp', cb, decay, xdt)
    decay_states = jnp.exp(acs[:, :, -1:] - acs)
    states = jnp.einsum('bclgn,bclgj,bclgjp->bcgjpn', bc, decay_states, xdt)
    chunk_decay = jnp.exp(acs[:, :, -1])

    def step(s, inp):
        st, dec = inp
        return s * dec[..., None, None] + st, s

    s_final, starts = lax.scan(step, s0.reshape(bsz, G, J, P, N),
                               (jnp.moveaxis(states, 1, 0), jnp.moveaxis(chunk_decay, 1, 0)))
    starts = jnp.moveaxis(starts, 0, 1)
    y_off = jnp.einsum('bclgn,bcgjpn,bclgj->bclgjp', cc, starts, jnp.exp(acs))
    y = (y_diag + y_off).reshape(bsz, nc * Q, H, P)[:, :L]
    return y, s_final.reshape(bsz, H, P, N)


def ssd_mixer(x, conv_state, s0, w_in, conv_w, conv_b, dt_bias, a_log, d_skip, norm_g, w_out):
    bsz, L, _ = x.shape
    proj = x @ w_in
    z = proj[..., :BRANCH]
    xbc = proj[..., BRANCH:BRANCH + SSD_CONV_DIM]
    dt_raw = proj[..., BRANCH + SSD_CONV_DIM:]
    xbc, new_conv = causal_conv(xbc, conv_state, conv_w, conv_b)
    xbc = jax.nn.silu(xbc).astype(F32)
    gn = SSD_GROUPS * SSD_STATE
    xs = xbc[..., :BRANCH].reshape(bsz, L, SSD_HEADS, SSD_HEAD_DIM)
    bm = xbc[..., BRANCH:BRANCH + gn].reshape(bsz, L, SSD_GROUPS, SSD_STATE)
    cm = xbc[..., BRANCH + gn:].reshape(bsz, L, SSD_GROUPS, SSD_STATE)
    dt = jax.nn.softplus(dt_raw.astype(F32) + dt_bias.astype(F32))
    a = -jnp.exp(a_log.astype(F32))
    y, s_new = ssd_chunked(xs, dt, a, bm, cm, s0.astype(F32))
    y = y + d_skip.astype(F32)[:, None] * xs
    y = y.reshape(bsz, L, BRANCH) * jax.nn.silu(z.astype(F32))
    yg = y.reshape(bsz, L, SSD_GROUPS, BRANCH // SSD_GROUPS)
    yg = yg * lax.rsqrt(jnp.mean(yg * yg, axis=-1, keepdims=True) + EPS)
    y = yg.reshape(bsz, L, BRANCH) * norm_g.astype(F32)
    return y.astype(x.dtype) @ w_out, new_conv, s_new.astype(x.dtype)


def gla_chunked(q, k, v, g, s0):
    bsz, L, H, _ = q.shape
    DV = v.shape[-1]
    Q = min(HGRN_CHUNK, L)
    pad = (-L) % Q
    nc = (L + pad) // Q

    def chunks(t):
        t = pad_seq(t, pad)
        return jnp.moveaxis(t.reshape(bsz, nc, Q, H, t.shape[-1]), 1, 0)

    causal = jnp.tril(jnp.ones((Q, Q), dtype=bool))

    def step(S, inp):
        qc, kc, vc, gc = inp
        G = jnp.cumsum(gc, axis=1)
        qg = qc * jnp.exp(G)
        att = jnp.einsum('blhd,bshd->bhls', qg, kc * jnp.exp(-G))
        att = jnp.where(causal, att, 0.0)
        o = jnp.einsum('bhls,bshv->blhv', att, vc) + jnp.einsum('blhd,bhdv->blhv', qg, S)
        g_last = G[:, -1]
        S = S * jnp.exp(g_last)[..., None] + jnp.einsum('bshd,bshv->bhdv', kc * jnp.exp(g_last[:, None] - G), vc)
        return S, o

    s_final, o = lax.scan(step, s0, (chunks(q), chunks(k), chunks(v), chunks(g)))
    o = jnp.moveaxis(o, 0, 1).reshape(bsz, nc * Q, H, DV)[:, :L]
    return o, s_final


def hgrn2_mixer(x, s0, lb, w_in, norm_g, w_out):
    bsz, L, _ = x.shape
    q, f, v, gate = jnp.split(x @ w_in, 4, axis=-1)
    shp = (bsz, L, HGRN_HEADS, HGRN_KEY_DIM)
    lb = lb.astype(F32)
    f = f.astype(F32)
    forget = lb + (1.0 - lb) * jax.nn.sigmoid(f)
    log_f = jnp.log(forget).reshape(shp)
    k = ((1.0 - lb) * jax.nn.sigmoid(-f)).reshape(shp)
    q = jax.nn.silu(q.astype(F32)).reshape(shp)
    v = v.astype(F32).reshape(bsz, L, HGRN_HEADS, HGRN_VAL_DIM)
    o, s_new = gla_chunked(q, k, v, log_f, s0.astype(F32))
    o = o * lax.rsqrt(jnp.mean(o * o, axis=-1, keepdims=True) + EPS)
    o = o.reshape(bsz, L, BRANCH) * norm_g.astype(F32) * jax.nn.silu(gate.astype(F32))
    return o.astype(x.dtype) @ w_out, s_new.astype(x.dtype)


def memory_kv(mem, g, w_k, w_v):
    m = rms_norm(mem, g)
    shp = mem.shape[:2] + (X_HEADS, X_HEAD_DIM)
    return (m @ w_k).reshape(shp), (m @ w_v).reshape(shp)


def memory_cross_attn(x, mem_k, mem_v, w_q, w_o):
    bsz, L, _ = x.shape
    q = (x @ w_q).reshape(bsz, L, X_HEADS, X_HEAD_DIM)
    s = jnp.einsum('blhd,bmhd->bhlm', q, mem_k.astype(q.dtype)).astype(F32) * (X_HEAD_DIM ** -0.5)
    p = jax.nn.softmax(s, axis=-1).astype(x.dtype)
    o = jnp.einsum('bhlm,bmhd->blhd', p, mem_v.astype(x.dtype)).reshape(bsz, L, D_MODEL)
    return o @ w_o


def run_trunk(x, mem_k, mem_v, rg_conv0, rg_h0, ssd_conv0, ssd_s0, hg_s0, P):
    cum = jnp.cumsum(jax.nn.softmax(P['hg_lower_bounds'].astype(F32), axis=0), axis=0)
    lower_bounds = cum - cum[0]
    rg_conv, rg_h, ssd_conv, ssd_s, hg_s = [], [], [], [], []
    for layer in range(DEPTH):
        kind, idx = layer % N_MIXERS, layer // N_MIXERS
        g = P['norm_g'][layer]
        h = rms_norm(x, g[0])
        if kind == 0:
            y, c_new, h_new = rglru_mixer(h, rg_conv0[idx], rg_h0[idx], P['rg_w_in'][idx], P['rg_conv_w'][idx],
                                          P['rg_conv_b'][idx], P['rg_w_a'][idx], P['rg_b_a'][idx], P['rg_w_x'][idx],
                                          P['rg_b_x'][idx], P['rg_lambda'][idx], P['rg_w_out'][idx])
            rg_conv.append(c_new)
            rg_h.append(h_new)
        elif kind == 1:
            y, c_new, s_new = ssd_mixer(h, ssd_conv0[idx], ssd_s0[idx], P['ssd_w_in'][idx], P['ssd_conv_w'][idx],
                                        P['ssd_conv_b'][idx], P['ssd_dt_bias'][idx], P['ssd_a_log'][idx],
                                        P['ssd_d'][idx], P['ssd_norm_g'][idx], P['ssd_w_out'][idx])
            ssd_conv.append(c_new)
            ssd_s.append(s_new)
        else:
            y, s_new = hgrn2_mixer(h, hg_s0[idx], lower_bounds[layer], P['hg_w_in'][idx], P['hg_norm_g'][idx],
                                   P['hg_w_out'][idx])
            hg_s.append(s_new)
        x = x + rms_norm(y, g[1])
        h = rms_norm(x, g[2])
        y = memory_cross_attn(h, mem_k[layer], mem_v[layer], P['x_w_q'][layer], P['x_w_o'][layer])
        x = x + rms_norm(y, g[3])
    return x, jnp.stack(rg_conv), jnp.stack(rg_h), jnp.stack(ssd_conv), jnp.stack(ssd_s), jnp.stack(hg_s)


def setup_inputs(seed: int = 0) -> dict:
    key = jax.random.key(seed)
    keys = iter(jax.random.split(key, 48))

    def nrm(shape, scale):
        return scale * jax.random.normal(next(keys), shape, F32)

    def unif(shape, lo, hi):
        return jax.random.uniform(next(keys), shape, F32, lo, hi)

    d_in = D_MODEL ** -0.5
    lam_a = unif((N_A, LRU_WIDTH), 0.9, 0.999) ** (1.0 / LRU_C)
    dt0 = jnp.exp(unif((N_B, SSD_HEADS), float(np.log(1e-3)), float(np.log(1e-1))))
    return {
        'x_prompt': nrm((BATCH, SEQ, D_MODEL), 1.0),
        'x_sample': nrm((DEC_BATCH, DEC_SEQ, D_MODEL), 1.0),
        'mem_prompt': nrm((BATCH, N_MEM, D_MODEL), 1.0),
        'state_rglru_conv': nrm((N_A, DEC_BATCH, CONV_W - 1, LRU_WIDTH), 1.0),
        'state_rglru_h': nrm((N_A, DEC_BATCH, LRU_WIDTH), 0.5),
        'state_ssd_conv': nrm((N_B, DEC_BATCH, CONV_W - 1, SSD_CONV_DIM), 1.0),
        'state_ssd': nrm((N_B, DEC_BATCH, SSD_HEADS, SSD_HEAD_DIM, SSD_STATE), 0.1),
        'state_hgrn': nrm((N_C, DEC_BATCH, HGRN_HEADS, HGRN_KEY_DIM, HGRN_VAL_DIM), 0.3),
        'cache_mem_k': nrm((DEPTH, DEC_BATCH, N_MEM, X_HEADS, X_HEAD_DIM), 1.0),
        'cache_mem_v': nrm((DEPTH, DEC_BATCH, N_MEM, X_HEADS, X_HEAD_DIM), 1.0),
        'norm_g': 1.0 + nrm((DEPTH, 4, D_MODEL), 0.02),
        'mem_norm_g': 1.0 + nrm((DEPTH, D_MODEL), 0.02),
        'rg_w_in': nrm((N_A, D_MODEL, 2 * LRU_WIDTH), d_in),
        'rg_conv_w': nrm((N_A, CONV_W, LRU_WIDTH), CONV_W ** -0.5),
        'rg_conv_b': nrm((N_A, LRU_WIDTH), 0.01),
        'rg_w_a': nrm((N_A, LRU_BLOCKS, LRU_BLOCK, LRU_BLOCK), LRU_BLOCK ** -0.5),
        'rg_b_a': nrm((N_A, LRU_WIDTH), 0.01),
        'rg_w_x': nrm((N_A, LRU_BLOCKS, LRU_BLOCK, LRU_BLOCK), LRU_BLOCK ** -0.5),
        'rg_b_x': nrm((N_A, LRU_WIDTH), 0.01),
        'rg_lambda': jnp.log(lam_a) - jnp.log1p(-lam_a),
        'rg_w_out': nrm((N_A, LRU_WIDTH, D_MODEL), LRU_WIDTH ** -0.5),
        'ssd_w_in': nrm((N_B, D_MODEL, BRANCH + SSD_CONV_DIM + SSD_HEADS), d_in),
        'ssd_conv_w': nrm((N_B, CONV_W, SSD_CONV_DIM), CONV_W ** -0.5),
        'ssd_conv_b': nrm((N_B, SSD_CONV_DIM), 0.01),
        'ssd_dt_bias': dt0 + jnp.log(-jnp.expm1(-dt0)),
        'ssd_a_log': jnp.log(unif((N_B, SSD_HEADS), 1.0, 16.0)),
        'ssd_d': 1.0 + nrm((N_B, SSD_HEADS), 0.02),
        'ssd_norm_g': 1.0 + nrm((N_B, BRANCH), 0.02),
        'ssd_w_out': nrm((N_B, BRANCH, D_MODEL), BRANCH ** -0.5),
        'hg_w_in': nrm((N_C, D_MODEL, 4 * BRANCH), d_in),
        'hg_lower_bounds': nrm((DEPTH, BRANCH), 0.1),
        'hg_norm_g': 1.0 + nrm((N_C, BRANCH), 0.02),
        'hg_w_out': nrm((N_C, BRANCH, D_MODEL), BRANCH ** -0.5),
        'x_w_q': nrm((DEPTH, D_MODEL, D_MODEL), d_in),
        'x_w_k': nrm((DEPTH, D_MODEL, D_MODEL), d_in),
        'x_w_v': nrm((DEPTH, D_MODEL, D_MODEL), d_in),
        'x_w_o': nrm((DEPTH, D_MODEL, D_MODEL), d_in),
    }


def reference(x_prompt, x_sample, mem_prompt, state_rglru_conv, state_rglru_h, state_ssd_conv, state_ssd,
              state_hgrn, cache_mem_k, cache_mem_v, norm_g, mem_norm_g, rg_w_in, rg_conv_w, rg_conv_b, rg_w_a,
              rg_b_a, rg_w_x, rg_b_x, rg_lambda, rg_w_out, ssd_w_in, ssd_conv_w, ssd_conv_b, ssd_dt_bias,
              ssd_a_log, ssd_d, ssd_norm_g, ssd_w_out, hg_w_in, hg_lower_bounds, hg_norm_g, hg_w_out,
              x_w_q, x_w_k, x_w_v, x_w_o):
    P = {
        'norm_g': norm_g, 'rg_w_in': rg_w_in, 'rg_conv_w': rg_conv_w, 'rg_conv_b': rg_conv_b,
        'rg_w_a': rg_w_a, 'rg_b_a': rg_b_a, 'rg_w_x': rg_w_x, 'rg_b_x': rg_b_x, 'rg_lambda': rg_lambda,
        'rg_w_out': rg_w_out, 'ssd_w_in': ssd_w_in, 'ssd_conv_w': ssd_conv_w, 'ssd_conv_b': ssd_conv_b,
        'ssd_dt_bias': ssd_dt_bias, 'ssd_a_log': ssd_a_log, 'ssd_d': ssd_d, 'ssd_norm_g': ssd_norm_g,
        'ssd_w_out': ssd_w_out, 'hg_w_in': hg_w_in, 'hg_lower_bounds': hg_lower_bounds,
        'hg_norm_g': hg_norm_g, 'hg_w_out': hg_w_out, 'x_w_q': x_w_q, 'x_w_o': x_w_o,
    }
    dtype = x_prompt.dtype
    bp = x_prompt.shape[0]
    mem_kv = [memory_kv(mem_prompt, mem_norm_g[l], x_w_k[l], x_w_v[l]) for l in range(DEPTH)]
    new_mem_k_p = jnp.stack([kv[0] for kv in mem_kv])
    new_mem_v_p = jnp.stack([kv[1] for kv in mem_kv])
    y_prompt, rgc_p, rgh_p, sc_p, ss_p, hs_p = run_trunk(
        x_prompt, new_mem_k_p, new_mem_v_p,
        jnp.zeros((N_A, bp, CONV_W - 1, LRU_WIDTH), dtype),
        jnp.zeros((N_A, bp, LRU_WIDTH), dtype),
        jnp.zeros((N_B, bp, CONV_W - 1, SSD_CONV_DIM), dtype),
        jnp.zeros((N_B, bp, SSD_HEADS, SSD_HEAD_DIM, SSD_STATE), dtype),
        jnp.zeros((N_C, bp, HGRN_HEADS, HGRN_KEY_DIM, HGRN_VAL_DIM), dtype),
        P)
    y_sample, rgc_s, rgh_s, sc_s, ss_s, hs_s = run_trunk(
        x_sample, cache_mem_k, cache_mem_v, state_rglru_conv, state_rglru_h, state_ssd_conv, state_ssd,
        state_hgrn, P)
    return (y_prompt, y_sample, rgc_p, rgh_p, sc_p, ss_p, hs_p, new_mem_k_p, new_mem_v_p,
            rgc_s, rgh_s, sc_s, ss_s, hs_s)
```

```python
import functools

import jax
import jax.numpy as jnp
from jax import lax
from jax.experimental import pallas as pl
from jax.experimental.pallas import tpu as pltpu

F32 = jnp.float32
BF16 = jnp.bfloat16

D_MODEL = 1024
DEPTH = 4
N_MIXERS = 3
BRANCH = 2 * D_MODEL
CONV_W = 4
EPS = 1e-6
LRU_BLOCKS = 8
LRU_BLOCK = BRANCH // LRU_BLOCKS
LRU_C = 8.0
SSD_HEAD_DIM = 64
SSD_HEADS = BRANCH // SSD_HEAD_DIM
SSD_STATE = 128
SSD_GROUPS = 8
SSD_CONV_DIM = BRANCH + 2 * SSD_GROUPS * SSD_STATE
SSD_CHUNK = 128
HGRN_KEY_DIM = 128
HGRN_HEADS = BRANCH // HGRN_KEY_DIM
HGRN_VAL_DIM = BRANCH // HGRN_HEADS
HGRN_CHUNK = 16
N_MEM = 256
X_HEADS = 4
X_HEAD_DIM = D_MODEL // X_HEADS

SUBLANES = 8
LANES = 128
VMEM_BYTES_V7X = 64 * 1024 * 1024
VMEM_LIMIT = VMEM_BYTES_V7X * 7 // 8

NT_DIMS = (((1,), (1,)), ((), ()))
TN_DIMS = (((0,), (0,)), ((), ()))


def _params(n_grid_dims):
    return pltpu.CompilerParams(
        dimension_semantics=("arbitrary",) * n_grid_dims, vmem_limit_bytes=VMEM_LIMIT)


def _const_spec(shape):
    nd = len(shape)
    return pl.BlockSpec(shape, lambda *_: (0,) * nd, pipeline_mode=pl.Buffered(1))


def _rms(x, g):
    return x * lax.rsqrt(jnp.mean(x * x, axis=-1, keepdims=True) + EPS) * g


def _dot(a, b):
    return jnp.dot(a, b, preferred_element_type=F32)


def _dot_nt(a, b):
    return lax.dot_general(a, b, NT_DIMS, preferred_element_type=F32)


def _dot_tn(a, b):
    return lax.dot_general(a, b, TN_DIMS, preferred_element_type=F32)


def _group_iota(width):
    return lax.broadcasted_iota(jnp.int32, (1, SUBLANES, width), 1)


def _conv8(u3, prev3, cw, cb):
    t = _group_iota(u3.shape[-1])
    acc = cb + cw[CONV_W - 1:CONV_W, :] * u3
    for k in range(1, CONV_W):
        shifted = pltpu.roll(jnp.where(t >= SUBLANES - k, prev3, u3), k, 1)
        acc = acc + cw[CONV_W - 1 - k:CONV_W - k, :] * shifted
    return acc


def _scan8(a3, b3):
    t = _group_iota(a3.shape[-1])
    for s in (1, 2, 4):
        m = t >= s
        a_sh = pltpu.roll(a3, s, 1)
        b_sh = pltpu.roll(b3, s, 1)
        b3 = jnp.where(m, a3 * b_sh + b3, b3)
        a3 = jnp.where(m, a3 * a_sh, a3)
    return a3, b3


def _cumsum8(x3):
    t = _group_iota(x3.shape[-1])
    for s in (1, 2, 4):
        x3 = x3 + jnp.where(t >= s, pltpu.roll(x3, s, 1), 0.0)
    return x3


def _norm_matmul_kernel(x_ref, g_ref, *refs, n_chunk):
    n_w = len(refs) // 2
    h = _rms(x_ref[...], g_ref[...]).astype(BF16)
    for w_ref, o_ref in zip(refs[:n_w], refs[n_w:]):
        n = o_ref.shape[-1]
        step = min(n_chunk, n)
        for c in range(0, n, step):
            o_ref[:, c:c + step] = _dot(h, w_ref[:, c:c + step])


def _norm_matmul(x, g, ws, tm):
    t, d = x.shape
    grid = (t // tm,)
    in_specs = [pl.BlockSpec((tm, d), lambda i: (i, 0)), _const_spec((1, d))]
    in_specs += [_const_spec(w.shape) for w in ws]
    out_specs = [pl.BlockSpec((tm, w.shape[1]), lambda i: (i, 0)) for w in ws]
    out_shape = [jax.ShapeDtypeStruct((t, w.shape[1]), F32) for w in ws]
    return pl.pallas_call(
        functools.partial(_norm_matmul_kernel, n_chunk=512),
        grid=grid, in_specs=in_specs, out_specs=out_specs, out_shape=out_shape,
        compiler_params=_params(1), name="norm_matmul",
    )(x, g, *ws)


def _proj_norm_res_kernel(a_ref, w_ref, g_ref, x_ref, o_ref):
    y = _dot(a_ref[...], w_ref[...])
    o_ref[...] = x_ref[...] + _rms(y, g_ref[...])


def _proj_norm_res(a, w, g, x, tm):
    t, k = a.shape
    d = x.shape[1]
    return pl.pallas_call(
        _proj_norm_res_kernel,
        grid=(t // tm,),
        in_specs=[pl.BlockSpec((tm, k), lambda i: (i, 0)), _const_spec(w.shape), _const_spec((1, d)),
                  pl.BlockSpec((tm, d), lambda i: (i, 0))],
        out_specs=pl.BlockSpec((tm, d), lambda i: (i, 0)),
        out_shape=jax.ShapeDtypeStruct((t, d), F32),
        compiler_params=_params(1), name="proj_norm_res",
    )(a, w, g, x)


def _mem_kv_kernel(m_ref, g_ref, wk_ref, wv_ref, k_ref, v_ref):
    h = _rms(m_ref[...], g_ref[...]).astype(BF16)
    k_ref[...] = _dot(h, wk_ref[...])
    v_ref[...] = _dot(h, wv_ref[...])


def _mem_kv(mem, g, wk, wv, tm):
    t, d = mem.shape
    w_spec = pl.BlockSpec((None, d, d), lambda l, i: (l, 0, 0))
    o_spec = pl.BlockSpec((None, tm, d), lambda l, i: (l, i, 0))
    return pl.pallas_call(
        _mem_kv_kernel,
        grid=(DEPTH, t // tm),
        in_specs=[pl.BlockSpec((tm, d), lambda l, i: (i, 0)),
                  pl.BlockSpec((None, 1, d), lambda l, i: (l, 0, 0)), w_spec, w_spec],
        out_specs=[o_spec, o_spec],
        out_shape=[jax.ShapeDtypeStruct((DEPTH, t, d), F32)] * 2,
        compiler_params=_params(2), name="mem_kv",
    )(mem, g, wk, wv)


def _attn_kernel(x_ref, k_ref, v_ref, wq_ref, wo_ref, g_ref, o_ref, q_scr, a_scr, *,
                 rows, hoist, seq_rows):
    i = pl.program_id(0) if hoist else None
    scale = X_HEAD_DIM ** -0.5

    def project_q():
        h = _rms(x_ref[...], g_ref[2:3, :]).astype(BF16)
        q_scr[...] = (_dot(h, wq_ref[...]) * scale).astype(BF16)

    def project_out():
        y = _dot(a_scr[...], wo_ref[...])
        o_ref[...] = x_ref[...] + _rms(y, g_ref[3:4, :])

    if hoist:
        pl.when(i == 0)(project_q)
        r0 = pl.multiple_of(i * rows, rows)
        rsl = pl.ds(r0, rows)
    else:
        project_q()
        rsl = slice(None)

    n_keys = k_ref.shape[0]
    if seq_rows is not None:
        qi = lax.broadcasted_iota(jnp.int32, (rows, n_keys), 0) // seq_rows
        ki = lax.broadcasted_iota(jnp.int32, (rows, n_keys), 1) // N_MEM
        mask = qi == ki
    for h in range(X_HEADS):
        hs = slice(h * X_HEAD_DIM, (h + 1) * X_HEAD_DIM)
        qh = q_scr[rsl, hs]
        s = _dot_nt(qh, k_ref[:, hs].astype(BF16))
        if seq_rows is not None:
            s = jnp.where(mask, s, -jnp.inf)
        e = jnp.exp(s - jnp.max(s, axis=-1, keepdims=True))
        p = e / jnp.sum(e, axis=-1, keepdims=True)
        a_scr[rsl, hs] = _dot(p.astype(BF16), v_ref[:, hs].astype(BF16)).astype(BF16)

    if hoist:
        pl.when(i == pl.num_programs(0) - 1)(project_out)
    else:
        project_out()


def _attn_prompt(x, k, v, layer, wq, wo, g, bsz, seq, tl):
    d = x.shape[1]
    nt = seq // tl
    kv_spec = pl.BlockSpec((None, N_MEM, d), lambda b, i: (layer, b, 0))
    return pl.pallas_call(
        functools.partial(_attn_kernel, rows=tl, hoist=False, seq_rows=None),
        grid=(bsz, nt),
        in_specs=[pl.BlockSpec((tl, d), lambda b, i: (b * nt + i, 0)), kv_spec, kv_spec,
                  _const_spec(wq.shape), _const_spec(wo.shape), _const_spec(g.shape)],
        out_specs=pl.BlockSpec((tl, d), lambda b, i: (b * nt + i, 0)),
        out_shape=jax.ShapeDtypeStruct(x.shape, F32),
        scratch_shapes=[pltpu.VMEM((tl, d), BF16), pltpu.VMEM((tl, d), BF16)],
        compiler_params=_params(2), name="attn_prompt",
    )(x, k, v, wq, wo, g)


def _attn_sample(x, k, v, layer, wq, wo, g, seq, nb):
    t, d = x.shape
    bsz = t // seq
    kv_spec = pl.BlockSpec((None, nb * N_MEM, d), lambda i: (layer, i, 0))
    return pl.pallas_call(
        functools.partial(_attn_kernel, rows=nb * seq, hoist=True, seq_rows=seq),
        grid=(bsz // nb,),
        in_specs=[_const_spec(x.shape), kv_spec, kv_spec,
                  _const_spec(wq.shape), _const_spec(wo.shape), _const_spec(g.shape)],
        out_specs=pl.BlockSpec(x.shape, lambda i: (0, 0)),
        out_shape=jax.ShapeDtypeStruct(x.shape, F32),
        scratch_shapes=[pltpu.VMEM((t, d), BF16), pltpu.VMEM((t, d), BF16)],
        compiler_params=_params(1), name="attn_sample",
    )(x, k, v, wq, wo, g)


def _rg_ab(u3, prev3, cw_ref, cb_ref, wax_ref, ba_ref, bx_ref, lam_ref):
    g, _, c = u3.shape
    r = g * SUBLANES
    conv = _conv8(u3, prev3, cw_ref[...], cb_ref[...]).reshape(r, c)
    conv16 = conv.astype(BF16)
    neg_c_softplus = -LRU_C * jax.nn.softplus(-lam_ref[...])
    a_parts, b_parts = [], []
    for h in range(LRU_BLOCKS):
        sl = slice(h * LRU_BLOCK, (h + 1) * LRU_BLOCK)
        pre = _dot(conv16[:, sl], wax_ref[h])
        rg = jax.nn.sigmoid(pre[:, :LRU_BLOCK] + ba_ref[:, sl])
        ig = jax.nn.sigmoid(pre[:, LRU_BLOCK:] + bx_ref[:, sl])
        a = jnp.exp(rg * neg_c_softplus[:, sl])
        a_parts.append(a)
        b_parts.append(jnp.sqrt(1.0 - a * a) * (ig * conv[:, sl]))
    a3 = jnp.concatenate(a_parts, axis=1).reshape(g, SUBLANES, c)
    b3 = jnp.concatenate(b_parts, axis=1).reshape(g, SUBLANES, c)
    return a3, b3


def _rg_prompt_kernel(proj_ref, cw_ref, cb_ref, wax_ref, ba_ref, bx_ref, lam_ref,
                      y_ref, htail_ref, ubuf, h_scr, hc):
    r = proj_ref.shape[0]
    c = BRANCH
    g = r // SUBLANES

    @pl.when(pl.program_id(1) == 0)
    def _():
        ubuf[0:SUBLANES, :] = jnp.zeros((SUBLANES, c), F32)
        hc[...] = jnp.zeros_like(hc)

    ubuf[SUBLANES:, :] = proj_ref[:, :c]
    u3 = ubuf[SUBLANES:, :].reshape(g, SUBLANES, c)
    prev3 = ubuf[0:r, :].reshape(g, SUBLANES, c)
    a3, b3 = _rg_ab(u3, prev3, cw_ref, cb_ref, wax_ref, ba_ref, bx_ref, lam_ref)
    ubuf[0:SUBLANES, :] = ubuf[r:, :]
    a3, b3 = _scan8(a3, b3)
    hprev = hc[...]
    for j in range(g):
        hj = a3[j] * hprev + b3[j]
        h_scr[j * SUBLANES:(j + 1) * SUBLANES, :] = hj
        hprev = jnp.broadcast_to(hj[SUBLANES - 1:, :], (SUBLANES, c))
    hc[...] = hprev
    htail_ref[...] = hprev
    y_ref[...] = (h_scr[...] * jax.nn.silu(proj_ref[:, c:])).astype(BF16)


def _rg_sample_kernel(proj_ref, prev_ref, h0_ref, cw_ref, cb_ref, wax_ref, ba_ref, bx_ref, lam_ref,
                      y_ref, h_ref):
    r = proj_ref.shape[0]
    c = BRANCH
    g = r // SUBLANES
    u3 = proj_ref[:, :c].reshape(g, SUBLANES, c)
    prev3 = prev_ref[...].reshape(g, SUBLANES, c)
    a3, b3 = _rg_ab(u3, prev3, cw_ref, cb_ref, wax_ref, ba_ref, bx_ref, lam_ref)
    b3 = b3 + a3 * h0_ref[...].reshape(g, SUBLANES, c)
    _, h3 = _scan8(a3, b3)
    h = h3.reshape(r, c)
    h_ref[...] = h
    y_ref[...] = (h * jax.nn.silu(proj_ref[:, c:])).astype(BF16)


def _rg_weight_specs(p):
    return [_const_spec(p["conv_w"].shape), _const_spec(p["conv_b"].shape), _const_spec(p["w_ax"].shape),
            _const_spec(p["b_a"].shape), _const_spec(p["b_x"].shape), _const_spec(p["lam"].shape)]


def _rg_weights(p):
    return (p["conv_w"], p["conv_b"], p["w_ax"], p["b_a"], p["b_x"], p["lam"])


def _rg_core_prompt(proj, p, bsz, seq, tl):
    nt = seq // tl
    c = BRANCH
    return pl.pallas_call(
        _rg_prompt_kernel,
        grid=(bsz, nt),
        in_specs=[pl.BlockSpec((tl, 2 * c), lambda b, i: (b * nt + i, 0))] + _rg_weight_specs(p),
        out_specs=[pl.BlockSpec((tl, c), lambda b, i: (b * nt + i, 0)),
                   pl.BlockSpec((None, SUBLANES, c), lambda b, i: (b, 0, 0))],
        out_shape=[jax.ShapeDtypeStruct((bsz * seq, c), BF16),
                   jax.ShapeDtypeStruct((bsz, SUBLANES, c), F32)],
        scratch_shapes=[pltpu.VMEM((SUBLANES + tl, c), F32), pltpu.VMEM((tl, c), F32),
                        pltpu.VMEM((SUBLANES, c), F32)],
        compiler_params=_params(2), name="rg_core_prompt",
    )(proj, *_rg_weights(p))


def _rg_core_sample(proj, prev8, h0pad, p, tm):
    t = proj.shape[0]
    c = BRANCH
    row = lambda w: pl.BlockSpec((tm, w), lambda i: (i, 0))
    return pl.pallas_call(
        _rg_sample_kernel,
        grid=(t // tm,),
        in_specs=[row(2 * c), row(c), row(c)] + _rg_weight_specs(p),
        out_specs=[row(c), row(c)],
        out_shape=[jax.ShapeDtypeStruct((t, c), BF16), jax.ShapeDtypeStruct((t, c), F32)],
        compiler_params=_params(1), name="rg_core_sample",
    )(proj, prev8, h0pad, *_rg_weights(p))


def _ssd_chunk(z, u3, prev3, dt_raw, s_read, s_write, cw_ref, cb_ref, dtb_ref, alog_ref, dexp_ref,
               ng_ref, acs_carry):
    q = z.shape[0]
    n = SSD_STATE
    xbc = jax.nn.silu(_conv8(u3, prev3, cw_ref[...], cb_ref[...]).reshape(q, SSD_CONV_DIM))
    xs = xbc[:, :BRANCH]
    bm = xbc[:, BRANCH:BRANCH + SSD_GROUPS * n]
    cm = xbc[:, BRANCH + SSD_GROUPS * n:]
    dt = jax.nn.softplus(dt_raw + dtb_ref[...])
    a = -jnp.exp(alog_ref[...])
    da3 = _cumsum8((dt * a).reshape(q // SUBLANES, SUBLANES, LANES))
    rows, carry = [], acs_carry
    for j in range(q // SUBLANES):
        blk = da3[j] + carry
        carry = jnp.broadcast_to(blk[SUBLANES - 1:, :], (SUBLANES, LANES))
        rows.append(blk)
    acs = jnp.concatenate(rows, axis=0) if len(rows) > 1 else rows[0]
    last = acs[q - 1:q, :]
    if q % LANES == 0:
        acs_t, dt_t = acs.T, dt.T
    else:
        pad = jnp.zeros((LANES - q, LANES), F32)
        acs_t = jnp.concatenate([acs, pad], axis=0).T[:, :q]
        dt_t = jnp.concatenate([dt, pad], axis=0).T[:, :q]
    causal = (lax.broadcasted_iota(jnp.int32, (q, q), 0) >= lax.broadcasted_iota(jnp.int32, (q, q), 1))
    lane = lax.broadcasted_iota(jnp.int32, (q, LANES), 1)
    srow = lax.broadcasted_iota(jnp.int32, (LANES, LANES), 0)
    half = SSD_HEAD_DIM
    y_pairs = []
    for g in range(SSD_GROUPS):
        bm_g = bm[:, g * n:(g + 1) * n]
        cm_g = cm[:, g * n:(g + 1) * n]
        cb_g = _dot_nt(cm_g.astype(BF16), bm_g.astype(BF16))
        for jp in range(2):
            pair = 2 * g + jp
            m_parts, cce_parts, bcw_parts, cds = [], [], [], []
            for h in (2 * pair, 2 * pair + 1):
                colb = jnp.broadcast_to(acs[:, h:h + 1], (q, LANES))
                dcol = jnp.broadcast_to(dt[:, h:h + 1], (q, LANES))
                seg = colb[:, :q] - acs_t[h:h + 1, :]
                decay = jnp.exp(jnp.where(causal, seg, -jnp.inf))
                m_parts.append(cb_g * decay * dt_t[h:h + 1, :])
                cce_parts.append(cm_g * jnp.exp(colb))
                lastb = last[:, h:h + 1]
                bcw_parts.append(bm_g * (dcol * jnp.exp(lastb - colb)))
                cds.append(jnp.exp(lastb))
            xs_pair = xs[:, pair * LANES:(pair + 1) * LANES]
            top = jnp.where(lane < half, xs_pair, 0.0)
            bot = jnp.where(lane >= half, xs_pair, 0.0)
            w = jnp.concatenate([top, bot], axis=0).astype(BF16)
            if q % LANES == 0:
                yd = _dot(jnp.concatenate(m_parts, axis=1).astype(BF16), w)
            else:
                yd = _dot(m_parts[0], top) + _dot(m_parts[1], bot)
            s_pair = s_read(pair)
            s_blk = jnp.concatenate([jnp.where(srow < half, s_pair, 0.0),
                                     jnp.where(srow >= half, s_pair, 0.0)], axis=1).astype(BF16)
            yo = _dot_nt(jnp.concatenate(cce_parts, axis=1).astype(BF16), s_blk)
            ds = _dot_tn(w, jnp.concatenate(bcw_parts, axis=0).astype(BF16))
            cd = jnp.where(srow < half, jnp.broadcast_to(cds[0], (LANES, LANES)),
                           jnp.broadcast_to(cds[1], (LANES, LANES)))
            s_write(pair, s_pair * cd + ds)
            y_pairs.append(yd + yo + dexp_ref[:, pair * LANES:(pair + 1) * LANES] * xs_pair)
    gw = BRANCH // SSD_GROUPS
    y_groups = []
    for g in range(SSD_GROUPS):
        yg = jnp.concatenate(y_pairs[2 * g:2 * g + 2], axis=1) * jax.nn.silu(z[:, g * gw:(g + 1) * gw])
        y_groups.append(yg * lax.rsqrt(jnp.mean(yg * yg, axis=-1, keepdims=True) + EPS))
    return jnp.concatenate(y_groups, axis=1) * ng_ref[...], carry


def _ssd_prompt_kernel(zx_ref, dt_ref, cw_ref, cb_ref, dtb_ref, alog_ref, dexp_ref, ng_ref,
                       y_ref, sout_ref, xbuf, s_scr):
    q = zx_ref.shape[0]
    g = q // SUBLANES
    c = SSD_CONV_DIM

    @pl.when(pl.program_id(1) == 0)
    def _():
        xbuf[0:SUBLANES, :] = jnp.zeros((SUBLANES, c), F32)
        s_scr[...] = jnp.zeros_like(s_scr)

    xbuf[SUBLANES:, :] = zx_ref[:, BRANCH:]
    u3 = xbuf[SUBLANES:, :].reshape(g, SUBLANES, c)
    prev3 = xbuf[0:q, :].reshape(g, SUBLANES, c)

    def s_read(pair):
        return s_scr[pair * LANES:(pair + 1) * LANES, :]

    def s_write(pair, val):
        s_scr[pair * LANES:(pair + 1) * LANES, :] = val

    y, _ = _ssd_chunk(zx_ref[:, :BRANCH], u3, prev3, dt_ref[...], s_read, s_write, cw_ref, cb_ref,
                      dtb_ref, alog_ref, dexp_ref, ng_ref, jnp.zeros((SUBLANES, LANES), F32))
    xbuf[0:SUBLANES, :] = xbuf[q:, :]
    y_ref[...] = y.astype(BF16)

    @pl.when(pl.program_id(1) == pl.num_programs(1) - 1)
    def _():
        sout_ref[...] = s_scr[...]


def _ssd_sample_kernel(zx_ref, dt_ref, prev_ref, s0_ref, cw_ref, cb_ref, dtb_ref, alog_ref, dexp_ref,
                       ng_ref, y_ref, sout_ref):
    q = zx_ref.shape[0]
    c = SSD_CONV_DIM
    u3 = zx_ref[:, BRANCH:].reshape(1, q, c)
    prev3 = prev_ref[...].reshape(1, q, c)

    def s_read(pair):
        return s0_ref[pair * LANES:(pair + 1) * LANES, :]

    def s_write(pair, val):
        sout_ref[pair * LANES:(pair + 1) * LANES, :] = val

    y, _ = _ssd_chunk(zx_ref[:, :BRANCH], u3, prev3, dt_ref[...], s_read, s_write, cw_ref, cb_ref,
                      dtb_ref, alog_ref, dexp_ref, ng_ref, jnp.zeros((SUBLANES, LANES), F32))
    y_ref[...] = y.astype(BF16)


def _ssd_weight_specs(p):
    return [_const_spec(p[k].shape) for k in ("conv_w", "conv_b", "dt_bias", "a_log", "d_exp", "norm_g")]


def _ssd_weights(p):
    return tuple(p[k] for k in ("conv_w", "conv_b", "dt_bias", "a_log", "d_exp", "norm_g"))


def _ssd_core_prompt(zx, dt, p, bsz, seq):
    q = SSD_CHUNK
    nt = seq // q
    hp = SSD_HEADS * SSD_HEAD_DIM
    return pl.pallas_call(
        _ssd_prompt_kernel,
        grid=(bsz, nt),
        in_specs=[pl.BlockSpec((q, zx.shape[1]), lambda b, i: (b * nt + i, 0)),
                  pl.BlockSpec((q, LANES), lambda b, i: (b * nt + i, 0))] + _ssd_weight_specs(p),
        out_specs=[pl.BlockSpec((q, BRANCH), lambda b, i: (b * nt + i, 0)),
                   pl.BlockSpec((None, hp, SSD_STATE), lambda b, i: (b, 0, 0))],
        out_shape=[jax.ShapeDtypeStruct((bsz * seq, BRANCH), BF16),
                   jax.ShapeDtypeStruct((bsz, hp, SSD_STATE), F32)],
        scratch_shapes=[pltpu.VMEM((SUBLANES + q, SSD_CONV_DIM), F32), pltpu.VMEM((hp, SSD_STATE), F32)],
        compiler_params=_params(2), name="ssd_core_prompt",
    )(zx, dt, *_ssd_weights(p))


def _ssd_core_sample(zx, dt, prev8, s0, idx, p, seq):
    t = zx.shape[0]
    bsz = t // seq
    hp = SSD_HEADS * SSD_HEAD_DIM
    row = lambda w: pl.BlockSpec((seq, w), lambda i: (i, 0))
    st = pl.BlockSpec((None, hp, SSD_STATE), lambda i: (i, 0, 0))
    st_in = pl.BlockSpec((None, None, hp, SSD_STATE), lambda i: (idx, i, 0, 0))
    return pl.pallas_call(
        _ssd_sample_kernel,
        grid=(bsz,),
        in_specs=[row(zx.shape[1]), row(LANES), row(SSD_CONV_DIM), st_in] + _ssd_weight_specs(p),
        out_specs=[row(BRANCH), st],
        out_shape=[jax.ShapeDtypeStruct((t, BRANCH), BF16), jax.ShapeDtypeStruct((bsz, hp, SSD_STATE), F32)],
        compiler_params=_params(1), name="ssd_core_sample",
    )(zx, dt, prev8, s0, *_ssd_weights(p))


def _hg_kernel(proj_ref, hlb_ref, ng_ref, *refs, layer, chunk, chained):
    if chained:
        y_ref, sout_ref, qg_scr, kg_scr, ke_scr, v_scr, dec_scr, o_scr, st_scr = refs
        s0_ref = None
    else:
        s0_ref, y_ref, sout_ref, qg_scr, kg_scr, ke_scr, v_scr, dec_scr, o_scr, st_scr = refs
    r = proj_ref.shape[0]
    c = BRANCH
    g = r // SUBLANES
    n_chunks = r // chunk
    dk, dv = HGRN_KEY_DIM, HGRN_VAL_DIM

    rows = [hlb_ref[j:j + 1, :] for j in range(DEPTH)]
    mx = functools.reduce(jnp.maximum, rows)
    es = [jnp.exp(x - mx) for x in rows]
    lb = sum(es[1:layer + 1]) / sum(es)

    f = proj_ref[:, c:2 * c]
    forget = lb + (1.0 - lb) * jax.nn.sigmoid(f)
    k = (1.0 - lb) * jax.nn.sigmoid(-f)
    gcum3 = _cumsum8(jnp.log(forget).reshape(g, SUBLANES, c))
    if chunk == 2 * SUBLANES:
        g16 = gcum3.reshape(g // 2, chunk, c)
        first = g16[:, :SUBLANES]
        second = g16[:, SUBLANES:] + jnp.broadcast_to(first[:, SUBLANES - 1:, :], first.shape)
        gcum = jnp.concatenate([first, second], axis=1).reshape(r, c)
        glast = jnp.broadcast_to(second[:, SUBLANES - 1:, :], g16.shape).reshape(r, c)
    else:
        gcum = gcum3.reshape(r, c)
        glast = jnp.broadcast_to(gcum3[:, SUBLANES - 1:, :], gcum3.shape).reshape(r, c)
    qg_scr[...] = jax.nn.silu(proj_ref[:, :c]) * jnp.exp(gcum)
    kg_scr[...] = k * jnp.exp(-gcum)
    ke_scr[...] = k * jnp.exp(glast - gcum)
    dec_scr[...] = jnp.exp(glast)
    v_scr[...] = proj_ref[:, 2 * c:3 * c]

    if chained:
        @pl.when(pl.program_id(1) == 0)
        def _():
            st_scr[...] = jnp.zeros_like(st_scr)

    causal = (lax.broadcasted_iota(jnp.int32, (chunk, chunk), 0)
              >= lax.broadcasted_iota(jnp.int32, (chunk, chunk), 1))

    def chunk_body(ci, carry):
        r0 = pl.multiple_of(ci * chunk, chunk)
        rs = pl.ds(r0, chunk)
        outs = []
        for h in range(HGRN_HEADS):
            ks = slice(h * dk, (h + 1) * dk)
            vs = slice(h * dv, (h + 1) * dv)
            qg = qg_scr[rs, ks].astype(BF16)
            vv = v_scr[rs, vs].astype(BF16)
            if chained:
                st = st_scr[h * dv:(h + 1) * dv, :]
            else:
                st = s0_ref[ci, h].T
            att = jnp.where(causal, _dot_nt(qg, kg_scr[rs, ks].astype(BF16)), 0.0)
            outs.append(_dot(att.astype(BF16), vv) + _dot_nt(qg, st.astype(BF16)))
            dec = dec_scr[pl.ds(r0 + chunk - SUBLANES, SUBLANES), ks][SUBLANES - 1:, :]
            st_new = st * dec + _dot_tn(vv, ke_scr[rs, ks].astype(BF16))
            if chained:
                st_scr[h * dv:(h + 1) * dv, :] = st_new
            else:
                sout_ref[ci, h] = st_new.T
        o_scr[rs, :] = jnp.concatenate(outs, axis=1)
        return carry

    lax.fori_loop(0, n_chunks, chunk_body, 0)

    gate = jax.nn.silu(proj_ref[:, 3 * c:])
    parts = []
    for h in range(HGRN_HEADS):
        o = o_scr[:, h * dv:(h + 1) * dv]
        parts.append(o * lax.rsqrt(jnp.mean(o * o, axis=-1, keepdims=True) + EPS))
    y_ref[...] = (jnp.concatenate(parts, axis=1) * ng_ref[...] * gate).astype(BF16)

    if chained:
        @pl.when(pl.program_id(1) == pl.num_programs(1) - 1)
        def _():
            for h in range(HGRN_HEADS):
                sout_ref[h] = st_scr[h * dv:(h + 1) * dv, :].T


def _hg_scratch(rows):
    c = BRANCH
    return [pltpu.VMEM((rows, c), F32)] * 6 + [pltpu.VMEM((HGRN_HEADS * HGRN_VAL_DIM, HGRN_KEY_DIM), F32)]


def _hg_core_prompt(proj, hlb, ng, layer, bsz, seq, tl):
    nt = seq // tl
    c = BRANCH
    st_shape = (HGRN_HEADS, HGRN_KEY_DIM, HGRN_VAL_DIM)
    return pl.pallas_call(
        functools.partial(_hg_kernel, layer=layer, chunk=HGRN_CHUNK, chained=True),
        grid=(bsz, nt),
        in_specs=[pl.BlockSpec((tl, 4 * c), lambda b, i: (b * nt + i, 0)),
                  _const_spec(hlb.shape), _const_spec(ng.shape)],
        out_specs=[pl.BlockSpec((tl, c), lambda b, i: (b * nt + i, 0)),
                   pl.BlockSpec((None,) + st_shape, lambda b, i: (b, 0, 0, 0))],
        out_shape=[jax.ShapeDtypeStruct((bsz * seq, c), BF16),
                   jax.ShapeDtypeStruct((bsz,) + st_shape, F32)],
        scratch_shapes=_hg_scratch(tl),
        compiler_params=_params(2), name="hg_core_prompt",
    )(proj, hlb, ng)


def _hg_core_sample(proj, s0, idx, hlb, ng, layer, seq, nb):
    t = proj.shape[0]
    bsz = t // seq
    c = BRANCH
    rows = nb * seq
    st_shape = (nb, HGRN_HEADS, HGRN_KEY_DIM, HGRN_VAL_DIM)
    st_spec = pl.BlockSpec(st_shape, lambda i: (i, 0, 0, 0))
    st_in = pl.BlockSpec((None,) + st_shape, lambda i: (idx, i, 0, 0, 0))
    return pl.pallas_call(
        functools.partial(_hg_kernel, layer=layer, chunk=seq, chained=False),
        grid=(bsz // nb,),
        in_specs=[pl.BlockSpec((rows, 4 * c), lambda i: (i, 0)),
                  _const_spec(hlb.shape), _const_spec(ng.shape), st_in],
        out_specs=[pl.BlockSpec((rows, c), lambda i: (i, 0)), st_spec],
        out_shape=[jax.ShapeDtypeStruct((t, c), BF16), jax.ShapeDtypeStruct(s0.shape[1:], F32)],
        scratch_shapes=_hg_scratch(rows),
        compiler_params=_params(1), name="hg_core_sample",
    )(proj, hlb, ng, s0)


def _tile(n, target):
    t = min(n, target)
    assert n % t == 0, (n, target)
    return t


def _pad_groups(state, first_row):
    n, k, c = state.shape
    return jnp.pad(state, ((0, 0), (first_row, SUBLANES - first_row - k), (0, 0))).reshape(n * SUBLANES, c)


def _trunk(x, mem_k, mem_v, states, w, bsz, seq, prompt):
    tm = _tile(x.shape[0], 256)
    rg_conv, rg_h, ssd_conv, ssd_s, hg_s = [], [], [], [], []
    tail = slice(seq - (CONV_W - 1), seq)
    for layer in range(DEPTH):
        kind, idx = layer % N_MIXERS, layer // N_MIXERS
        g = w["norm_g"][layer]
        if kind == 0:
            p = w["rg"][idx]
            (proj,) = _norm_matmul(x, g[0:1], [p["w_in"]], tm)
            if prompt:
                y, htail = _rg_core_prompt(proj, p, bsz, seq, _tile(seq, 256))
                rg_h.append(htail[:, SUBLANES - 1])
            else:
                prev8 = _pad_groups(states["rg_conv"][idx], SUBLANES - (CONV_W - 1))
                h0pad = _pad_groups(states["rg_h"][idx][:, None, :], 0)
                y, h = _rg_core_sample(proj, prev8, h0pad, p, tm)
                rg_h.append(h.reshape(bsz, seq, BRANCH)[:, seq - 1])
            rg_conv.append(proj[:, :BRANCH].reshape(bsz, seq, BRANCH)[:, tail])
        elif kind == 1:
            p = w["ssd"][idx]
            zx, dt = _norm_matmul(x, g[0:1], [p["w_zx"], p["w_dt"]], tm)
            if prompt:
                y, s_new = _ssd_core_prompt(zx, dt, p, bsz, seq)
            else:
                prev8 = _pad_groups(states["ssd_conv"][idx], SUBLANES - (CONV_W - 1))
                s0 = states["ssd_s"].reshape(-1, bsz, SSD_HEADS * SSD_HEAD_DIM, SSD_STATE)
                y, s_new = _ssd_core_sample(zx, dt, prev8, s0, idx, p, seq)
            ssd_s.append(s_new.reshape(bsz, SSD_HEADS, SSD_HEAD_DIM, SSD_STATE))
            ssd_conv.append(zx[:, BRANCH:].reshape(bsz, seq, SSD_CONV_DIM)[:, tail])
        else:
            p = w["hg"][idx]
            (proj,) = _norm_matmul(x, g[0:1], [p["w_in"]], tm)
            if prompt:
                y, s_new = _hg_core_prompt(proj, w["hg_lower_bounds"], p["norm_g"], layer, bsz, seq,
                                           _tile(seq, 256))
            else:
                y, s_new = _hg_core_sample(proj, states["hg_s"], idx, w["hg_lower_bounds"], p["norm_g"],
                                           layer, seq, 8)
            hg_s.append(s_new)
        x = _proj_norm_res(y, p["w_out"], g[1:2], x, tm)
        if prompt:
            x = _attn_prompt(x, mem_k, mem_v, layer, w["x_w_q"][layer], w["x_w_o"][layer], g,
                             bsz, seq, _tile(seq, 512))
        else:
            x = _attn_sample(x, mem_k, mem_v, layer, w["x_w_q"][layer], w["x_w_o"][layer], g,
                             seq, 4)
    return x, jnp.stack(rg_conv), jnp.stack(rg_h), jnp.stack(ssd_conv), jnp.stack(ssd_s), jnp.stack(hg_s)


def kernel(x_prompt, x_sample, mem_prompt, state_rglru_conv, state_rglru_h, state_ssd_conv, state_ssd,
           state_hgrn, cache_mem_k, cache_mem_v, norm_g, mem_norm_g, rg_w_in, rg_conv_w, rg_conv_b, rg_w_a,
           rg_b_a, rg_w_x, rg_b_x, rg_lambda, rg_w_out, ssd_w_in, ssd_conv_w, ssd_conv_b, ssd_dt_bias,
           ssd_a_log, ssd_d, ssd_norm_g, ssd_w_out, hg_w_in, hg_lower_bounds, hg_norm_g, hg_w_out,
           x_w_q, x_w_k, x_w_v, x_w_o):
    bp, sp, d = x_prompt.shape
    bs, ss, _ = x_sample.shape
    n_a, n_b, n_c = rg_w_in.shape[0], ssd_w_in.shape[0], hg_w_in.shape[0]
    pad_heads = lambda v: jnp.pad(v, (0, LANES - SSD_HEADS))[None, :]
    w = {
        "norm_g": norm_g,
        "hg_lower_bounds": hg_lower_bounds,
        "x_w_q": x_w_q.astype(BF16),
        "x_w_o": x_w_o.astype(BF16),
        "rg": [{
            "w_in": rg_w_in[i].astype(BF16),
            "conv_w": rg_conv_w[i], "conv_b": rg_conv_b[i][None, :],
            "w_ax": jnp.concatenate([rg_w_a[i], rg_w_x[i]], axis=-1).astype(BF16),
            "b_a": rg_b_a[i][None, :], "b_x": rg_b_x[i][None, :], "lam": rg_lambda[i][None, :],
            "w_out": rg_w_out[i].astype(BF16),
        } for i in range(n_a)],
        "ssd": [{
            "w_zx": ssd_w_in[i][:, :BRANCH + SSD_CONV_DIM].astype(BF16),
            "w_dt": jnp.pad(ssd_w_in[i][:, BRANCH + SSD_CONV_DIM:], ((0, 0), (0, LANES - SSD_HEADS))).astype(BF16),
            "conv_w": ssd_conv_w[i], "conv_b": ssd_conv_b[i][None, :],
            "dt_bias": pad_heads(ssd_dt_bias[i]), "a_log": pad_heads(ssd_a_log[i]),
            "d_exp": jnp.repeat(ssd_d[i], SSD_HEAD_DIM)[None, :],
            "norm_g": ssd_norm_g[i][None, :],
            "w_out": ssd_w_out[i].astype(BF16),
        } for i in range(n_b)],
        "hg": [{
            "w_in": hg_w_in[i].astype(BF16),
            "norm_g": hg_norm_g[i][None, :],
            "w_out": hg_w_out[i].astype(BF16),
        } for i in range(n_c)],
    }

    mem_k_p, mem_v_p = _mem_kv(mem_prompt.reshape(bp * N_MEM, d), mem_norm_g[:, None, :],
                               x_w_k.astype(BF16), x_w_v.astype(BF16), _tile(bp * N_MEM, 512))
    y_p, rgc_p, rgh_p, sc_p, ss_p, hs_p = _trunk(
        x_prompt.reshape(bp * sp, d), mem_k_p, mem_v_p, None, w, bp, sp, True)
    states = {"rg_conv": state_rglru_conv, "rg_h": state_rglru_h, "ssd_conv": state_ssd_conv,
              "ssd_s": state_ssd, "hg_s": state_hgrn}
    y_s, rgc_s, rgh_s, sc_s, ss_s, hs_s = _trunk(
        x_sample.reshape(bs * ss, d), cache_mem_k.reshape(DEPTH, bs * N_MEM, d),
        cache_mem_v.reshape(DEPTH, bs * N_MEM, d), states, w, bs, ss, False)
    kv_shape = (DEPTH, bp, N_MEM, X_HEADS, X_HEAD_DIM)
    return (y_p.reshape(bp, sp, d), y_s.reshape(bs, ss, d), rgc_p, rgh_p, sc_p, ss_p, hs_p,
            mem_k_p.reshape(kv_shape), mem_v_p.reshape(kv_shape), rgc_s, rgh_s, sc_s, ss_s, hs_s)
```

```python
import functools

import jax
import jax.numpy as jnp
from jax import lax
from jax.experimental import pallas as pl
from jax.experimental.pallas import tpu as pltpu

F32 = jnp.float32
BF16 = jnp.bfloat16

D_MODEL = 1024
DEPTH = 4
N_MIXERS = 3
BRANCH = 2 * D_MODEL
CONV_W = 4
EPS = 1e-6
LRU_BLOCKS = 8
LRU_BLOCK = BRANCH // LRU_BLOCKS
LRU_C = 8.0
SSD_HEAD_DIM = 64
SSD_HEADS = BRANCH // SSD_HEAD_DIM
SSD_STATE = 128
SSD_GROUPS = 8
SSD_CONV_DIM = BRANCH + 2 * SSD_GROUPS * SSD_STATE
SSD_CHUNK = 128
HGRN_KEY_DIM = 128
HGRN_HEADS = BRANCH // HGRN_KEY_DIM
HGRN_VAL_DIM = BRANCH // HGRN_HEADS
HGRN_CHUNK = 16
HGRN_BLOCK = 4 * HGRN_CHUNK
N_MEM = 256
X_HEADS = 4
X_HEAD_DIM = D_MODEL // X_HEADS

SUBLANES = 8
LANES = 128
VMEM_BYTES_V7X = 64 * 1024 * 1024
VMEM_LIMIT = VMEM_BYTES_V7X * 7 // 8

NT_DIMS = (((1,), (1,)), ((), ()))
TN_DIMS = (((0,), (0,)), ((), ()))


def _params(n_grid_dims):
    return pltpu.CompilerParams(
        dimension_semantics=("arbitrary",) * n_grid_dims, vmem_limit_bytes=VMEM_LIMIT)


def _const_spec(shape):
    nd = len(shape)
    return pl.BlockSpec(shape, lambda *_: (0,) * nd, pipeline_mode=pl.Buffered(1))


def _rms(x, g):
    return x * lax.rsqrt(jnp.mean(x * x, axis=-1, keepdims=True) + EPS) * g


def _dot(a, b):
    return jnp.dot(a, b, preferred_element_type=F32)


def _dot_nt(a, b):
    return lax.dot_general(a, b, NT_DIMS, preferred_element_type=F32)


def _dot_tn(a, b):
    return lax.dot_general(a, b, TN_DIMS, preferred_element_type=F32)


def _group_iota(width):
    return lax.broadcasted_iota(jnp.int32, (1, SUBLANES, width), 1)


def _conv8(u3, prev3, cw, cb):
    t = _group_iota(u3.shape[-1])
    acc = cb + cw[CONV_W - 1:CONV_W, :] * u3
    for k in range(1, CONV_W):
        shifted = pltpu.roll(jnp.where(t >= SUBLANES - k, prev3, u3), k, 1)
        acc = acc + cw[CONV_W - 1 - k:CONV_W - k, :] * shifted
    return acc


def _scan8(a3, b3):
    t = _group_iota(a3.shape[-1])
    for s in (1, 2, 4):
        m = t >= s
        a_sh = pltpu.roll(a3, s, 1)
        b_sh = pltpu.roll(b3, s, 1)
        b3 = jnp.where(m, a3 * b_sh + b3, b3)
        a3 = jnp.where(m, a3 * a_sh, a3)
    return a3, b3


def _cumsum8(x3):
    t = _group_iota(x3.shape[-1])
    for s in (1, 2, 4):
        x3 = x3 + jnp.where(t >= s, pltpu.roll(x3, s, 1), 0.0)
    return x3


def _norm_matmul_kernel(x_ref, g_ref, *refs, n_chunk):
    n_w = len(refs) // 2
    h = _rms(x_ref[...], g_ref[...]).astype(BF16)
    for w_ref, o_ref in zip(refs[:n_w], refs[n_w:]):
        n = o_ref.shape[-1]
        step = min(n_chunk, n)
        for c in range(0, n, step):
            o_ref[:, c:c + step] = _dot(h, w_ref[:, c:c + step])


def _norm_matmul(x, g, ws, tm):
    t, d = x.shape
    grid = (t // tm,)
    in_specs = [pl.BlockSpec((tm, d), lambda i: (i, 0)), _const_spec((1, d))]
    in_specs += [_const_spec(w.shape) for w in ws]
    out_specs = [pl.BlockSpec((tm, w.shape[1]), lambda i: (i, 0)) for w in ws]
    out_shape = [jax.ShapeDtypeStruct((t, w.shape[1]), F32) for w in ws]
    return pl.pallas_call(
        functools.partial(_norm_matmul_kernel, n_chunk=512),
        grid=grid, in_specs=in_specs, out_specs=out_specs, out_shape=out_shape,
        compiler_params=_params(1), name="norm_matmul",
    )(x, g, *ws)


def _proj_norm_res_kernel(a_ref, w_ref, g_ref, x_ref, o_ref):
    y = _dot(a_ref[...], w_ref[...])
    o_ref[...] = x_ref[...] + _rms(y, g_ref[...])


def _proj_norm_res(a, w, g, x, tm):
    t, k = a.shape
    d = x.shape[1]
    return pl.pallas_call(
        _proj_norm_res_kernel,
        grid=(t // tm,),
        in_specs=[pl.BlockSpec((tm, k), lambda i: (i, 0)), _const_spec(w.shape), _const_spec((1, d)),
                  pl.BlockSpec((tm, d), lambda i: (i, 0))],
        out_specs=pl.BlockSpec((tm, d), lambda i: (i, 0)),
        out_shape=jax.ShapeDtypeStruct((t, d), F32),
        compiler_params=_params(1), name="proj_norm_res",
    )(a, w, g, x)


def _mem_kv_kernel(m_ref, g_ref, wk_ref, wv_ref, k_ref, v_ref):
    h = _rms(m_ref[...], g_ref[...]).astype(BF16)
    nb = k_ref.shape[0]
    for w_ref, o_ref in ((wk_ref, k_ref), (wv_ref, v_ref)):
        y = _dot(h, w_ref[...])
        for hd in range(X_HEADS):
            o_ref[:, :, hd, :] = y[:, hd * X_HEAD_DIM:(hd + 1) * X_HEAD_DIM].reshape(nb, N_MEM, X_HEAD_DIM)


def _mem_kv(mem, g, wk, wv, nb):
    t, d = mem.shape
    bsz = t // N_MEM
    tm = nb * N_MEM
    w_spec = pl.BlockSpec((None, d, d), lambda l, i: (l, 0, 0))
    o_spec = pl.BlockSpec((None, nb, N_MEM, X_HEADS, X_HEAD_DIM), lambda l, i: (l, i, 0, 0, 0))
    return pl.pallas_call(
        _mem_kv_kernel,
        grid=(DEPTH, bsz // nb),
        in_specs=[pl.BlockSpec((tm, d), lambda l, i: (i, 0)),
                  pl.BlockSpec((None, 1, d), lambda l, i: (l, 0, 0)), w_spec, w_spec],
        out_specs=[o_spec, o_spec],
        out_shape=[jax.ShapeDtypeStruct((DEPTH, bsz, N_MEM, X_HEADS, X_HEAD_DIM), F32)] * 2,
        compiler_params=_params(2), name="mem_kv",
    )(mem, g, wk, wv)


def _attn_kernel(x_ref, k_ref, v_ref, wq_ref, wo_ref, g_ref, o_ref, q_scr, a_scr, *,
                 rows, hoist, seq_rows):
    i = pl.program_id(0) if hoist else None
    scale = X_HEAD_DIM ** -0.5

    def project_q():
        h = _rms(x_ref[...], g_ref[2:3, :]).astype(BF16)
        q_scr[...] = (_dot(h, wq_ref[...]) * scale).astype(BF16)

    def project_out():
        y = _dot(a_scr[...], wo_ref[...])
        o_ref[...] = x_ref[...] + _rms(y, g_ref[3:4, :])

    if hoist:
        pl.when(i == 0)(project_q)
        r0 = pl.multiple_of(i * rows, rows)
        rsl = pl.ds(r0, rows)
    else:
        project_q()
        rsl = slice(None)

    n_keys = N_MEM if seq_rows is None else k_ref.shape[0] * N_MEM

    def head_rows(ref, h):
        if seq_rows is None:
            return ref[:, h, :].astype(BF16)
        return ref[:, :, h, :].reshape(n_keys, X_HEAD_DIM).astype(BF16)

    if seq_rows is not None:
        qi = lax.broadcasted_iota(jnp.int32, (rows, n_keys), 0) // seq_rows
        ki = lax.broadcasted_iota(jnp.int32, (rows, n_keys), 1) // N_MEM
        mask = qi == ki
    for h in range(X_HEADS):
        hs = slice(h * X_HEAD_DIM, (h + 1) * X_HEAD_DIM)
        qh = q_scr[rsl, hs]
        s = _dot_nt(qh, head_rows(k_ref, h))
        if seq_rows is not None:
            s = jnp.where(mask, s, -jnp.inf)
        e = jnp.exp(s - jnp.max(s, axis=-1, keepdims=True))
        p = e / jnp.sum(e, axis=-1, keepdims=True)
        a_scr[rsl, hs] = _dot(p.astype(BF16), head_rows(v_ref, h)).astype(BF16)

    if hoist:
        pl.when(i == pl.num_programs(0) - 1)(project_out)
    else:
        project_out()


def _attn_prompt(x, k, v, layer, wq, wo, g, bsz, seq, tl):
    d = x.shape[1]
    nt = seq // tl
    kv_spec = pl.BlockSpec((None, None, N_MEM, X_HEADS, X_HEAD_DIM), lambda b, i: (layer, b, 0, 0, 0))
    return pl.pallas_call(
        functools.partial(_attn_kernel, rows=tl, hoist=False, seq_rows=None),
        grid=(bsz, nt),
        in_specs=[pl.BlockSpec((tl, d), lambda b, i: (b * nt + i, 0)), kv_spec, kv_spec,
                  _const_spec(wq.shape), _const_spec(wo.shape), _const_spec(g.shape)],
        out_specs=pl.BlockSpec((tl, d), lambda b, i: (b * nt + i, 0)),
        out_shape=jax.ShapeDtypeStruct(x.shape, F32),
        scratch_shapes=[pltpu.VMEM((tl, d), BF16), pltpu.VMEM((tl, d), BF16)],
        compiler_params=_params(2), name="attn_prompt",
    )(x, k, v, wq, wo, g)


def _attn_sample(x, k, v, layer, wq, wo, g, seq, nb):
    t, d = x.shape
    bsz = t // seq
    kv_spec = pl.BlockSpec((None, nb, N_MEM, X_HEADS, X_HEAD_DIM), lambda i: (layer, i, 0, 0, 0))
    return pl.pallas_call(
        functools.partial(_attn_kernel, rows=nb * seq, hoist=True, seq_rows=seq),
        grid=(bsz // nb,),
        in_specs=[_const_spec(x.shape), kv_spec, kv_spec,
                  _const_spec(wq.shape), _const_spec(wo.shape), _const_spec(g.shape)],
        out_specs=pl.BlockSpec(x.shape, lambda i: (0, 0)),
        out_shape=jax.ShapeDtypeStruct(x.shape, F32),
        scratch_shapes=[pltpu.VMEM((t, d), BF16), pltpu.VMEM((t, d), BF16)],
        compiler_params=_params(1), name="attn_sample",
    )(x, k, v, wq, wo, g)


def _rg_ab(u3, prev3, cw_ref, cb_ref, wax_ref, ba_ref, bx_ref, lam_ref):
    g, _, c = u3.shape
    r = g * SUBLANES
    conv = _conv8(u3, prev3, cw_ref[...], cb_ref[...]).reshape(r, c)
    conv16 = conv.astype(BF16)
    neg_c_softplus = -LRU_C * jax.nn.softplus(-lam_ref[...])
    a_parts, b_parts = [], []
    for h in range(LRU_BLOCKS):
        sl = slice(h * LRU_BLOCK, (h + 1) * LRU_BLOCK)
        pre = _dot(conv16[:, sl], wax_ref[h])
        rg = jax.nn.sigmoid(pre[:, :LRU_BLOCK] + ba_ref[:, sl])
        ig = jax.nn.sigmoid(pre[:, LRU_BLOCK:] + bx_ref[:, sl])
        a = jnp.exp(rg * neg_c_softplus[:, sl])
        a_parts.append(a)
        b_parts.append(jnp.sqrt(1.0 - a * a) * (ig * conv[:, sl]))
    a3 = jnp.concatenate(a_parts, axis=1).reshape(g, SUBLANES, c)
    b3 = jnp.concatenate(b_parts, axis=1).reshape(g, SUBLANES, c)
    return a3, b3


def _rg_prompt_kernel(proj_ref, cw_ref, cb_ref, wax_ref, ba_ref, bx_ref, lam_ref,
                      y_ref, htail_ref, ubuf, h_scr, hc):
    r = proj_ref.shape[0]
    c = BRANCH
    g = r // SUBLANES

    @pl.when(pl.program_id(1) == 0)
    def _():
        ubuf[0:SUBLANES, :] = jnp.zeros((SUBLANES, c), F32)
        hc[...] = jnp.zeros_like(hc)

    ubuf[SUBLANES:, :] = proj_ref[:, :c]
    u3 = ubuf[SUBLANES:, :].reshape(g, SUBLANES, c)
    prev3 = ubuf[0:r, :].reshape(g, SUBLANES, c)
    a3, b3 = _rg_ab(u3, prev3, cw_ref, cb_ref, wax_ref, ba_ref, bx_ref, lam_ref)
    ubuf[0:SUBLANES, :] = ubuf[r:, :]
    a3, b3 = _scan8(a3, b3)
    hprev = hc[...]
    for j in range(g):
        hj = a3[j] * hprev + b3[j]
        h_scr[j * SUBLANES:(j + 1) * SUBLANES, :] = hj
        hprev = jnp.broadcast_to(hj[SUBLANES - 1:, :], (SUBLANES, c))
    hc[...] = hprev
    htail_ref[...] = hprev
    y_ref[...] = (h_scr[...] * jax.nn.silu(proj_ref[:, c:])).astype(BF16)


def _rg_sample_kernel(proj_ref, prev_ref, h0_ref, cw_ref, cb_ref, wax_ref, ba_ref, bx_ref, lam_ref,
                      y_ref, h_ref):
    r = proj_ref.shape[0]
    c = BRANCH
    g = r // SUBLANES
    u3 = proj_ref[:, :c].reshape(g, SUBLANES, c)
    prev3 = prev_ref[...].reshape(g, SUBLANES, c)
    a3, b3 = _rg_ab(u3, prev3, cw_ref, cb_ref, wax_ref, ba_ref, bx_ref, lam_ref)
    b3 = b3 + a3 * h0_ref[...].reshape(g, SUBLANES, c)
    _, h3 = _scan8(a3, b3)
    h = h3.reshape(r, c)
    h_ref[...] = h
    y_ref[...] = (h * jax.nn.silu(proj_ref[:, c:])).astype(BF16)


def _rg_weight_specs(p):
    return [_const_spec(p["conv_w"].shape), _const_spec(p["conv_b"].shape), _const_spec(p["w_ax"].shape),
            _const_spec(p["b_a"].shape), _const_spec(p["b_x"].shape), _const_spec(p["lam"].shape)]


def _rg_weights(p):
    return (p["conv_w"], p["conv_b"], p["w_ax"], p["b_a"], p["b_x"], p["lam"])


def _rg_core_prompt(proj, p, bsz, seq, tl):
    nt = seq // tl
    c = BRANCH
    return pl.pallas_call(
        _rg_prompt_kernel,
        grid=(bsz, nt),
        in_specs=[pl.BlockSpec((tl, 2 * c), lambda b, i: (b * nt + i, 0))] + _rg_weight_specs(p),
        out_specs=[pl.BlockSpec((tl, c), lambda b, i: (b * nt + i, 0)),
                   pl.BlockSpec((None, SUBLANES, c), lambda b, i: (b, 0, 0))],
        out_shape=[jax.ShapeDtypeStruct((bsz * seq, c), BF16),
                   jax.ShapeDtypeStruct((bsz, SUBLANES, c), F32)],
        scratch_shapes=[pltpu.VMEM((SUBLANES + tl, c), F32), pltpu.VMEM((tl, c), F32),
                        pltpu.VMEM((SUBLANES, c), F32)],
        compiler_params=_params(2), name="rg_core_prompt",
    )(proj, *_rg_weights(p))


def _rg_core_sample(proj, prev8, h0pad, p, tm):
    t = proj.shape[0]
    c = BRANCH
    row = lambda w: pl.BlockSpec((tm, w), lambda i: (i, 0))
    return pl.pallas_call(
        _rg_sample_kernel,
        grid=(t // tm,),
        in_specs=[row(2 * c), row(c), row(c)] + _rg_weight_specs(p),
        out_specs=[row(c), row(c)],
        out_shape=[jax.ShapeDtypeStruct((t, c), BF16), jax.ShapeDtypeStruct((t, c), F32)],
        compiler_params=_params(1), name="rg_core_sample",
    )(proj, prev8, h0pad, *_rg_weights(p))


def _ssd_chunk(z, u3, prev3, dt_raw, s_read, s_write, cw_ref, cb_ref, dtb_ref, alog_ref, dexp_ref,
               ng_ref, acs_carry):
    q = z.shape[0]
    n = SSD_STATE
    xbc = jax.nn.silu(_conv8(u3, prev3, cw_ref[...], cb_ref[...]).reshape(q, SSD_CONV_DIM))
    xs = xbc[:, :BRANCH]
    bm = xbc[:, BRANCH:BRANCH + SSD_GROUPS * n]
    cm = xbc[:, BRANCH + SSD_GROUPS * n:]
    dt = jax.nn.softplus(dt_raw + dtb_ref[...])
    a = -jnp.exp(alog_ref[...])
    da3 = _cumsum8((dt * a).reshape(q // SUBLANES, SUBLANES, LANES))
    rows, carry = [], acs_carry
    for j in range(q // SUBLANES):
        blk = da3[j] + carry
        carry = jnp.broadcast_to(blk[SUBLANES - 1:, :], (SUBLANES, LANES))
        rows.append(blk)
    acs = jnp.concatenate(rows, axis=0) if len(rows) > 1 else rows[0]
    last = acs[q - 1:q, :]
    if q % LANES == 0:
        acs_t, dt_t = acs.T, dt.T
    else:
        pad = jnp.zeros((LANES - q, LANES), F32)
        acs_t = jnp.concatenate([acs, pad], axis=0).T[:, :q]
        dt_t = jnp.concatenate([dt, pad], axis=0).T[:, :q]
    causal = (lax.broadcasted_iota(jnp.int32, (q, q), 0) >= lax.broadcasted_iota(jnp.int32, (q, q), 1))
    lane = lax.broadcasted_iota(jnp.int32, (q, LANES), 1)
    srow = lax.broadcasted_iota(jnp.int32, (LANES, LANES), 0)
    half = SSD_HEAD_DIM
    y_pairs = []
    for g in range(SSD_GROUPS):
        bm_g = bm[:, g * n:(g + 1) * n]
        cm_g = cm[:, g * n:(g + 1) * n]
        cb_g = _dot_nt(cm_g.astype(BF16), bm_g.astype(BF16))
        for jp in range(2):
            pair = 2 * g + jp
            m_parts, cce_parts, bcw_parts, cds = [], [], [], []
            for h in (2 * pair, 2 * pair + 1):
                colb = jnp.broadcast_to(acs[:, h:h + 1], (q, LANES))
                dcol = jnp.broadcast_to(dt[:, h:h + 1], (q, LANES))
                seg = colb[:, :q] - acs_t[h:h + 1, :]
                decay = jnp.exp(jnp.where(causal, seg, -jnp.inf))
                m_parts.append(cb_g * decay * dt_t[h:h + 1, :])
                cce_parts.append(cm_g * jnp.exp(colb))
                lastb = last[:, h:h + 1]
                bcw_parts.append(bm_g * (dcol * jnp.exp(lastb - colb)))
                cds.append(jnp.exp(lastb))
            xs_pair = xs[:, pair * LANES:(pair + 1) * LANES]
            top = jnp.where(lane < half, xs_pair, 0.0)
            bot = jnp.where(lane >= half, xs_pair, 0.0)
            w = jnp.concatenate([top, bot], axis=0).astype(BF16)
            if q % LANES == 0:
                yd = _dot(jnp.concatenate(m_parts, axis=1).astype(BF16), w)
            else:
                yd = _dot(m_parts[0], top) + _dot(m_parts[1], bot)
            s_pair = s_read(pair)
            s_blk = jnp.concatenate([jnp.where(srow < half, s_pair, 0.0),
                                     jnp.where(srow >= half, s_pair, 0.0)], axis=1).astype(BF16)
            yo = _dot_nt(jnp.concatenate(cce_parts, axis=1).astype(BF16), s_blk)
            ds = _dot_tn(w, jnp.concatenate(bcw_parts, axis=0).astype(BF16))
            cd = jnp.where(srow < half, jnp.broadcast_to(cds[0], (LANES, LANES)),
                           jnp.broadcast_to(cds[1], (LANES, LANES)))
            s_write(pair, s_pair * cd + ds)
            y_pairs.append(yd + yo + dexp_ref[:, pair * LANES:(pair + 1) * LANES] * xs_pair)
    gw = BRANCH // SSD_GROUPS
    y_groups = []
    for g in range(SSD_GROUPS):
        yg = jnp.concatenate(y_pairs[2 * g:2 * g + 2], axis=1) * jax.nn.silu(z[:, g * gw:(g + 1) * gw])
        y_groups.append(yg * lax.rsqrt(jnp.mean(yg * yg, axis=-1, keepdims=True) + EPS))
    return jnp.concatenate(y_groups, axis=1) * ng_ref[...], carry


def _ssd_prompt_kernel(zx_ref, dt_ref, cw_ref, cb_ref, dtb_ref, alog_ref, dexp_ref, ng_ref,
                       y_ref, sout_ref, xbuf, s_scr):
    q = zx_ref.shape[0]
    g = q // SUBLANES
    c = SSD_CONV_DIM

    @pl.when(pl.program_id(1) == 0)
    def _():
        xbuf[0:SUBLANES, :] = jnp.zeros((SUBLANES, c), F32)
        s_scr[...] = jnp.zeros_like(s_scr)

    xbuf[SUBLANES:, :] = zx_ref[:, BRANCH:]
    u3 = xbuf[SUBLANES:, :].reshape(g, SUBLANES, c)
    prev3 = xbuf[0:q, :].reshape(g, SUBLANES, c)

    def s_read(pair):
        return s_scr[pair * LANES:(pair + 1) * LANES, :]

    def s_write(pair, val):
        s_scr[pair * LANES:(pair + 1) * LANES, :] = val

    y, _ = _ssd_chunk(zx_ref[:, :BRANCH], u3, prev3, dt_ref[...], s_read, s_write, cw_ref, cb_ref,
                      dtb_ref, alog_ref, dexp_ref, ng_ref, jnp.zeros((SUBLANES, LANES), F32))
    xbuf[0:SUBLANES, :] = xbuf[q:, :]
    y_ref[...] = y.astype(BF16)

    @pl.when(pl.program_id(1) == pl.num_programs(1) - 1)
    def _():
        sout_ref[...] = s_scr[...]


def _ssd_sample_kernel(zx_ref, dt_ref, prev_ref, s0_ref, cw_ref, cb_ref, dtb_ref, alog_ref, dexp_ref,
                       ng_ref, y_ref, sout_ref):
    q = zx_ref.shape[0]
    c = SSD_CONV_DIM
    u3 = zx_ref[:, BRANCH:].reshape(1, q, c)
    prev3 = prev_ref[...].reshape(1, q, c)

    def s_read(pair):
        return s0_ref[pair * LANES:(pair + 1) * LANES, :]

    def s_write(pair, val):
        sout_ref[pair * LANES:(pair + 1) * LANES, :] = val

    y, _ = _ssd_chunk(zx_ref[:, :BRANCH], u3, prev3, dt_ref[...], s_read, s_write, cw_ref, cb_ref,
                      dtb_ref, alog_ref, dexp_ref, ng_ref, jnp.zeros((SUBLANES, LANES), F32))
    y_ref[...] = y.astype(BF16)


def _ssd_weight_specs(p):
    return [_const_spec(p[k].shape) for k in ("conv_w", "conv_b", "dt_bias", "a_log", "d_exp", "norm_g")]


def _ssd_weights(p):
    return tuple(p[k] for k in ("conv_w", "conv_b", "dt_bias", "a_log", "d_exp", "norm_g"))


def _ssd_core_prompt(zx, dt, p, bsz, seq):
    q = SSD_CHUNK
    nt = seq // q
    hp = SSD_HEADS * SSD_HEAD_DIM
    return pl.pallas_call(
        _ssd_prompt_kernel,
        grid=(bsz, nt),
        in_specs=[pl.BlockSpec((q, zx.shape[1]), lambda b, i: (b * nt + i, 0)),
                  pl.BlockSpec((q, LANES), lambda b, i: (b * nt + i, 0))] + _ssd_weight_specs(p),
        out_specs=[pl.BlockSpec((q, BRANCH), lambda b, i: (b * nt + i, 0)),
                   pl.BlockSpec((None, hp, SSD_STATE), lambda b, i: (b, 0, 0))],
        out_shape=[jax.ShapeDtypeStruct((bsz * seq, BRANCH), BF16),
                   jax.ShapeDtypeStruct((bsz, hp, SSD_STATE), F32)],
        scratch_shapes=[pltpu.VMEM((SUBLANES + q, SSD_CONV_DIM), F32), pltpu.VMEM((hp, SSD_STATE), F32)],
        compiler_params=_params(2), name="ssd_core_prompt",
    )(zx, dt, *_ssd_weights(p))


def _ssd_core_sample(zx, dt, prev8, s0, idx, p, seq):
    t = zx.shape[0]
    bsz = t // seq
    hp = SSD_HEADS * SSD_HEAD_DIM
    row = lambda w: pl.BlockSpec((seq, w), lambda i: (i, 0))
    st = pl.BlockSpec((None, hp, SSD_STATE), lambda i: (i, 0, 0))
    st_in = pl.BlockSpec((None, None, hp, SSD_STATE), lambda i: (idx, i, 0, 0))
    return pl.pallas_call(
        _ssd_sample_kernel,
        grid=(bsz,),
        in_specs=[row(zx.shape[1]), row(LANES), row(SSD_CONV_DIM), st_in] + _ssd_weight_specs(p),
        out_specs=[row(BRANCH), st],
        out_shape=[jax.ShapeDtypeStruct((t, BRANCH), BF16), jax.ShapeDtypeStruct((bsz, hp, SSD_STATE), F32)],
        compiler_params=_params(1), name="ssd_core_sample",
    )(zx, dt, prev8, s0, *_ssd_weights(p))


def _hg_lower_bound(rows, layer):
    mx = functools.reduce(jnp.maximum, rows)
    es = [jnp.exp(x - mx) for x in rows]
    return sum(es[1:layer + 1]) / sum(es)


def _hg_prompt_kernel(proj_ref, hlb_ref, ng_ref, y_ref, sout_ref, st_scr, *, layer):
    r = proj_ref.shape[0]
    c = BRANCH
    dk = HGRN_KEY_DIM
    blk, sub = HGRN_BLOCK, HGRN_CHUNK
    nblk, nsub = r // blk, blk // sub

    @pl.when(pl.program_id(1) == 0)
    def _():
        st_scr[...] = jnp.zeros_like(st_scr)

    causal = (lax.broadcasted_iota(jnp.int32, (blk, blk), 0)
              >= lax.broadcasted_iota(jnp.int32, (blk, blk), 1))

    def head_body(h, carry):
        def lanes(part):
            return pl.ds(pl.multiple_of(part * c + h * dk, dk), dk)

        lb = _hg_lower_bound([hlb_ref[j:j + 1, lanes(0)] for j in range(DEPTH)], layer)
        f = proj_ref[:, lanes(1)]
        forget = lb + (1.0 - lb) * jax.nn.sigmoid(f)
        k = ((1.0 - lb) * jax.nn.sigmoid(-f)).reshape(nblk, blk, dk)
        x = _cumsum8(jnp.log(forget).reshape(r // SUBLANES, SUBLANES, dk)).reshape(nblk, blk, dk)
        parts, carry_row = [], None
        for j in range(blk // SUBLANES):
            part = x[:, j * SUBLANES:(j + 1) * SUBLANES, :]
            if carry_row is not None:
                part = part + carry_row
            carry_row = jnp.broadcast_to(part[:, SUBLANES - 1:, :], part.shape)
            parts.append(part)
        gc = jnp.concatenate(parts, axis=1)
        ends = [gc[:, (i + 1) * sub - 1:(i + 1) * sub, :] for i in range(nsub)]
        starts = [jnp.zeros_like(ends[0])] + ends[:-1]
        spread = lambda rows_: jnp.concatenate(
            [jnp.broadcast_to(x_, (nblk, sub, dk)) for x_ in rows_], axis=1)
        b_prev, b_next = spread(starts), spread(ends)
        qi = jax.nn.silu(proj_ref[:, lanes(0)]).reshape(nblk, blk, dk) * jnp.exp(gc - b_prev)
        qc = qi * jnp.exp(b_prev)
        kd = k * jnp.exp(b_prev - gc)
        ke = kd * jnp.exp(b_next - b_prev)
        kend = ke * jnp.exp(ends[-1] - b_next)
        v = proj_ref[:, lanes(2)].reshape(nblk, blk, dk)

        st = st_scr[h]
        outs = []
        for b in range(nblk):
            qi_b = qi[b].astype(BF16)
            att_rows = []
            for i in range(nsub):
                keys = []
                for j in range(nsub):
                    rs = slice(j * sub, (j + 1) * sub)
                    if j == i:
                        keys.append(kd[b, rs])
                    elif j < i - 1:
                        keys.append(ke[b, rs] * jnp.exp(starts[i][b] - ends[j][b]))
                    else:
                        keys.append(ke[b, rs])
                keys = jnp.concatenate(keys, axis=0).astype(BF16)
                att_rows.append(_dot_nt(qi_b[i * sub:(i + 1) * sub], keys))
            att = jnp.where(causal, jnp.concatenate(att_rows, axis=0), 0.0).astype(BF16)
            vb = v[b].astype(BF16)
            outs.append(_dot(att, vb) + _dot_nt(qc[b].astype(BF16), st.astype(BF16)))
            st = st * jnp.exp(ends[-1][b]) + _dot_tn(vb, kend[b].astype(BF16))
        st_scr[h] = st
        o = jnp.concatenate(outs, axis=0)
        o = o * lax.rsqrt(jnp.mean(o * o, axis=-1, keepdims=True) + EPS)
        gate = jax.nn.silu(proj_ref[:, lanes(3)])
        y_ref[:, lanes(0)] = (o * ng_ref[:, lanes(0)] * gate).astype(BF16)
        return carry

    lax.fori_loop(0, HGRN_HEADS, head_body, 0, unroll=2)

    @pl.when(pl.program_id(1) == pl.num_programs(1) - 1)
    def _():
        for h in range(HGRN_HEADS):
            sout_ref[h] = st_scr[h].T


def _hg_seq_kernel(proj_ref, hlb_ref, ng_ref, s0_ref, y_ref, sout_ref,
                   qg_scr, kg_scr, ke_scr, v_scr, dec_scr, o_scr, *, layer, chunk):
    r = proj_ref.shape[0]
    c = BRANCH
    g = r // SUBLANES
    n_chunks = r // chunk
    dk, dv = HGRN_KEY_DIM, HGRN_VAL_DIM
    assert chunk == SUBLANES

    lb = _hg_lower_bound([hlb_ref[j:j + 1, :] for j in range(DEPTH)], layer)
    f = proj_ref[:, c:2 * c]
    forget = lb + (1.0 - lb) * jax.nn.sigmoid(f)
    k = (1.0 - lb) * jax.nn.sigmoid(-f)
    gcum3 = _cumsum8(jnp.log(forget).reshape(g, SUBLANES, c))
    gcum = gcum3.reshape(r, c)
    glast = jnp.broadcast_to(gcum3[:, SUBLANES - 1:, :], gcum3.shape).reshape(r, c)
    qg_scr[...] = jax.nn.silu(proj_ref[:, :c]) * jnp.exp(gcum)
    kg_scr[...] = k * jnp.exp(-gcum)
    ke_scr[...] = k * jnp.exp(glast - gcum)
    dec_scr[...] = jnp.exp(glast)
    v_scr[...] = proj_ref[:, 2 * c:3 * c]

    causal = (lax.broadcasted_iota(jnp.int32, (chunk, chunk), 0)
              >= lax.broadcasted_iota(jnp.int32, (chunk, chunk), 1))

    def chunk_body(ci, carry):
        rs = pl.ds(pl.multiple_of(ci * chunk, chunk), chunk)
        outs = []
        for h in range(HGRN_HEADS):
            ks = slice(h * dk, (h + 1) * dk)
            vs = slice(h * dv, (h + 1) * dv)
            qg = qg_scr[rs, ks].astype(BF16)
            vv = v_scr[rs, vs].astype(BF16)
            st = s0_ref[ci, h].T
            att = jnp.where(causal, _dot_nt(qg, kg_scr[rs, ks].astype(BF16)), 0.0)
            outs.append(_dot(att.astype(BF16), vv) + _dot_nt(qg, st.astype(BF16)))
            dec = dec_scr[rs, ks][chunk - 1:, :]
            sout_ref[ci, h] = (st * dec + _dot_tn(vv, ke_scr[rs, ks].astype(BF16))).T
        o_scr[rs, :] = jnp.concatenate(outs, axis=1)
        return carry

    lax.fori_loop(0, n_chunks, chunk_body, 0)

    gate = jax.nn.silu(proj_ref[:, 3 * c:])
    parts = []
    for h in range(HGRN_HEADS):
        o = o_scr[:, h * dv:(h + 1) * dv]
        parts.append(o * lax.rsqrt(jnp.mean(o * o, axis=-1, keepdims=True) + EPS))
    y_ref[...] = (jnp.concatenate(parts, axis=1) * ng_ref[...] * gate).astype(BF16)


def _hg_core_prompt(proj, hlb, ng, layer, bsz, seq, tl):
    nt = seq // tl
    c = BRANCH
    st_shape = (HGRN_HEADS, HGRN_KEY_DIM, HGRN_VAL_DIM)
    return pl.pallas_call(
        functools.partial(_hg_prompt_kernel, layer=layer),
        grid=(bsz, nt),
        in_specs=[pl.BlockSpec((tl, 4 * c), lambda b, i: (b * nt + i, 0)),
                  _const_spec(hlb.shape), _const_spec(ng.shape)],
        out_specs=[pl.BlockSpec((tl, c), lambda b, i: (b * nt + i, 0)),
                   pl.BlockSpec((None,) + st_shape, lambda b, i: (b, 0, 0, 0))],
        out_shape=[jax.ShapeDtypeStruct((bsz * seq, c), BF16),
                   jax.ShapeDtypeStruct((bsz,) + st_shape, F32)],
        scratch_shapes=[pltpu.VMEM((HGRN_HEADS, HGRN_VAL_DIM, HGRN_KEY_DIM), F32)],
        compiler_params=_params(2), name="hg_core_prompt",
    )(proj, hlb, ng)


def _hg_core_sample(proj, s0, idx, hlb, ng, layer, seq, nb):
    t = proj.shape[0]
    bsz = t // seq
    c = BRANCH
    rows = nb * seq
    st_shape = (nb, HGRN_HEADS, HGRN_KEY_DIM, HGRN_VAL_DIM)
    st_spec = pl.BlockSpec(st_shape, lambda i: (i, 0, 0, 0))
    st_in = pl.BlockSpec((None,) + st_shape, lambda i: (idx, i, 0, 0, 0))
    return pl.pallas_call(
        functools.partial(_hg_seq_kernel, layer=layer, chunk=seq),
        grid=(bsz // nb,),
        in_specs=[pl.BlockSpec((rows, 4 * c), lambda i: (i, 0)),
                  _const_spec(hlb.shape), _const_spec(ng.shape), st_in],
        out_specs=[pl.BlockSpec((rows, c), lambda i: (i, 0)), st_spec],
        out_shape=[jax.ShapeDtypeStruct((t, c), BF16), jax.ShapeDtypeStruct(s0.shape[1:], F32)],
        scratch_shapes=[pltpu.VMEM((rows, c), F32)] * 6,
        compiler_params=_params(1), name="hg_core_sample",
    )(proj, hlb, ng, s0)


def _tile(n, target):
    t = min(n, target)
    assert n % t == 0, (n, target)
    return t


def _pad_groups(state, first_row):
    n, k, c = state.shape
    return jnp.pad(state, ((0, 0), (first_row, SUBLANES - first_row - k), (0, 0))).reshape(n * SUBLANES, c)


def _trunk(x, mem_k, mem_v, states, w, bsz, seq, prompt):
    tm = _tile(x.shape[0], 256)
    rg_conv, rg_h, ssd_conv, ssd_s, hg_s = [], [], [], [], []
    tail = slice(seq - (CONV_W - 1), seq)
    for layer in range(DEPTH):
        kind, idx = layer % N_MIXERS, layer // N_MIXERS
        g = w["norm_g"][layer]
        if kind == 0:
            p = w["rg"][idx]
            (proj,) = _norm_matmul(x, g[0:1], [p["w_in"]], tm)
            if prompt:
                y, htail = _rg_core_prompt(proj, p, bsz, seq, _tile(seq, 256))
                rg_h.append(htail[:, SUBLANES - 1])
            else:
                prev8 = _pad_groups(states["rg_conv"][idx], SUBLANES - (CONV_W - 1))
                h0pad = _pad_groups(states["rg_h"][idx][:, None, :], 0)
                y, h = _rg_core_sample(proj, prev8, h0pad, p, tm)
                rg_h.append(h.reshape(bsz, seq, BRANCH)[:, seq - 1])
            rg_conv.append(proj.reshape(bsz, seq, 2 * BRANCH)[:, tail, :BRANCH])
        elif kind == 1:
            p = w["ssd"][idx]
            zx, dt = _norm_matmul(x, g[0:1], [p["w_zx"], p["w_dt"]], tm)
            if prompt:
                y, s_new = _ssd_core_prompt(zx, dt, p, bsz, seq)
            else:
                prev8 = _pad_groups(states["ssd_conv"][idx], SUBLANES - (CONV_W - 1))
                s0 = states["ssd_s"].reshape(-1, bsz, SSD_HEADS * SSD_HEAD_DIM, SSD_STATE)
                y, s_new = _ssd_core_sample(zx, dt, prev8, s0, idx, p, seq)
            ssd_s.append(s_new.reshape(bsz, SSD_HEADS, SSD_HEAD_DIM, SSD_STATE))
            ssd_conv.append(zx.reshape(bsz, seq, BRANCH + SSD_CONV_DIM)[:, tail, BRANCH:])
        else:
            p = w["hg"][idx]
            (proj,) = _norm_matmul(x, g[0:1], [p["w_in"]], tm)
            if prompt:
                y, s_new = _hg_core_prompt(proj, w["hg_lower_bounds"], p["norm_g"], layer, bsz, seq,
                                           _tile(seq, 256))
            else:
                y, s_new = _hg_core_sample(proj, states["hg_s"], idx, w["hg_lower_bounds"], p["norm_g"],
                                           layer, seq, 8)
            hg_s.append(s_new)
        x = _proj_norm_res(y, p["w_out"], g[1:2], x, tm)
        if prompt:
            x = _attn_prompt(x, mem_k, mem_v, layer, w["x_w_q"][layer], w["x_w_o"][layer], g,
                             bsz, seq, _tile(seq, 512))
        else:
            x = _attn_sample(x, mem_k, mem_v, layer, w["x_w_q"][layer], w["x_w_o"][layer], g,
                             seq, 4)
    return x, jnp.stack(rg_conv), jnp.stack(rg_h), jnp.stack(ssd_conv), jnp.stack(ssd_s), jnp.stack(hg_s)


def kernel(x_prompt, x_sample, mem_prompt, state_rglru_conv, state_rglru_h, state_ssd_conv, state_ssd,
           state_hgrn, cache_mem_k, cache_mem_v, norm_g, mem_norm_g, rg_w_in, rg_conv_w, rg_conv_b, rg_w_a,
           rg_b_a, rg_w_x, rg_b_x, rg_lambda, rg_w_out, ssd_w_in, ssd_conv_w, ssd_conv_b, ssd_dt_bias,
           ssd_a_log, ssd_d, ssd_norm_g, ssd_w_out, hg_w_in, hg_lower_bounds, hg_norm_g, hg_w_out,
           x_w_q, x_w_k, x_w_v, x_w_o):
    bp, sp, d = x_prompt.shape
    bs, ss, _ = x_sample.shape
    n_a, n_b, n_c = rg_w_in.shape[0], ssd_w_in.shape[0], hg_w_in.shape[0]
    pad_heads = lambda v: jnp.pad(v, (0, LANES - SSD_HEADS))[None, :]
    w = {
        "norm_g": norm_g,
        "hg_lower_bounds": hg_lower_bounds,
        "x_w_q": x_w_q.astype(BF16),
        "x_w_o": x_w_o.astype(BF16),
        "rg": [{
            "w_in": rg_w_in[i].astype(BF16),
            "conv_w": rg_conv_w[i], "conv_b": rg_conv_b[i][None, :],
            "w_ax": jnp.concatenate([rg_w_a[i], rg_w_x[i]], axis=-1).astype(BF16),
            "b_a": rg_b_a[i][None, :], "b_x": rg_b_x[i][None, :], "lam": rg_lambda[i][None, :],
            "w_out": rg_w_out[i].astype(BF16),
        } for i in range(n_a)],
        "ssd": [{
            "w_zx": ssd_w_in[i][:, :BRANCH + SSD_CONV_DIM].astype(BF16),
            "w_dt": jnp.pad(ssd_w_in[i][:, BRANCH + SSD_CONV_DIM:], ((0, 0), (0, LANES - SSD_HEADS))).astype(BF16),
            "conv_w": ssd_conv_w[i], "conv_b": ssd_conv_b[i][None, :],
            "dt_bias": pad_heads(ssd_dt_bias[i]), "a_log": pad_heads(ssd_a_log[i]),
            "d_exp": jnp.repeat(ssd_d[i], SSD_HEAD_DIM)[None, :],
            "norm_g": ssd_norm_g[i][None, :],
            "w_out": ssd_w_out[i].astype(BF16),
        } for i in range(n_b)],
        "hg": [{
            "w_in": hg_w_in[i].astype(BF16),
            "norm_g": hg_norm_g[i][None, :],
            "w_out": hg_w_out[i].astype(BF16),
        } for i in range(n_c)],
    }

    mem_k_p, mem_v_p = _mem_kv(mem_prompt.reshape(bp * N_MEM, d), mem_norm_g[:, None, :],
                               x_w_k.astype(BF16), x_w_v.astype(BF16), _tile(bp, 2))
    y_p, rgc_p, rgh_p, sc_p, ss_p, hs_p = _trunk(
        x_prompt.reshape(bp * sp, d), mem_k_p, mem_v_p, None, w, bp, sp, True)
    states = {"rg_conv": state_rglru_conv, "rg_h": state_rglru_h, "ssd_conv": state_ssd_conv,
              "ssd_s": state_ssd, "hg_s": state_hgrn}
    y_s, rgc_s, rgh_s, sc_s, ss_s, hs_s = _trunk(
        x_sample.reshape(bs * ss, d), cache_mem_k, cache_mem_v, states, w, bs, ss, False)
    return (y_p.reshape(bp, sp, d), y_s.reshape(bs, ss, d), rgc_p, rgh_p, sc_p, ss_p, hs_p,
            mem_k_p, mem_v_p, rgc_s, rgh_s, sc_s, ss_s, hs_s)
```

```python
import functools

import jax
import jax.numpy as jnp
from jax import lax
from jax.experimental import pallas as pl
from jax.experimental.pallas import tpu as pltpu

F32 = jnp.float32
BF16 = jnp.bfloat16

D_MODEL = 1024
DEPTH = 4
N_MIXERS = 3
BRANCH = 2 * D_MODEL
CONV_W = 4
EPS = 1e-6
LRU_BLOCKS = 8
LRU_BLOCK = BRANCH // LRU_BLOCKS
LRU_C = 8.0
SSD_HEAD_DIM = 64
SSD_HEADS = BRANCH // SSD_HEAD_DIM
SSD_STATE = 128
SSD_GROUPS = 8
SSD_CONV_DIM = BRANCH + 2 * SSD_GROUPS * SSD_STATE
SSD_CHUNK = 128
HGRN_KEY_DIM = 128
HGRN_HEADS = BRANCH // HGRN_KEY_DIM
HGRN_VAL_DIM = BRANCH // HGRN_HEADS
HGRN_CHUNK = 16
HGRN_BLOCK = 4 * HGRN_CHUNK
N_MEM = 256
X_HEADS = 4
X_HEAD_DIM = D_MODEL // X_HEADS

SUBLANES = 8
LANES = 128
VMEM_BYTES_V7X = 64 * 1024 * 1024
VMEM_LIMIT = VMEM_BYTES_V7X * 7 // 8

NT_DIMS = (((1,), (1,)), ((), ()))
TN_DIMS = (((0,), (0,)), ((), ()))


def _params(n_grid_dims):
    return pltpu.CompilerParams(
        dimension_semantics=("arbitrary",) * n_grid_dims, vmem_limit_bytes=VMEM_LIMIT)


def _const_spec(shape):
    nd = len(shape)
    return pl.BlockSpec(shape, lambda *_: (0,) * nd, pipeline_mode=pl.Buffered(1))


def _rms(x, g):
    return x * lax.rsqrt(jnp.mean(x * x, axis=-1, keepdims=True) + EPS) * g


def _dot(a, b):
    return jnp.dot(a, b, preferred_element_type=F32)


def _dot_nt(a, b):
    return lax.dot_general(a, b, NT_DIMS, preferred_element_type=F32)


def _dot_tn(a, b):
    return lax.dot_general(a, b, TN_DIMS, preferred_element_type=F32)


def _group_iota(width):
    return lax.broadcasted_iota(jnp.int32, (1, SUBLANES, width), 1)


def _conv8(u3, prev3, cw, cb):
    t = _group_iota(u3.shape[-1])
    acc = cb + cw[CONV_W - 1:CONV_W, :] * u3
    for k in range(1, CONV_W):
        shifted = pltpu.roll(jnp.where(t >= SUBLANES - k, prev3, u3), k, 1)
        acc = acc + cw[CONV_W - 1 - k:CONV_W - k, :] * shifted
    return acc


def _scan8(a3, b3):
    t = _group_iota(a3.shape[-1])
    for s in (1, 2, 4):
        m = t >= s
        a_sh = pltpu.roll(a3, s, 1)
        b_sh = pltpu.roll(b3, s, 1)
        b3 = jnp.where(m, a3 * b_sh + b3, b3)
        a3 = jnp.where(m, a3 * a_sh, a3)
    return a3, b3


def _cumsum8(x3):
    t = _group_iota(x3.shape[-1])
    for s in (1, 2, 4):
        x3 = x3 + jnp.where(t >= s, pltpu.roll(x3, s, 1), 0.0)
    return x3


def _norm_matmul_kernel(x_ref, g_ref, *refs, n_chunk):
    n_w = len(refs) // 2
    h = _rms(x_ref[...], g_ref[...]).astype(BF16)
    for w_ref, o_ref in zip(refs[:n_w], refs[n_w:]):
        n = o_ref.shape[-1]
        step = min(n_chunk, n)
        for c in range(0, n, step):
            o_ref[:, c:c + step] = _dot(h, w_ref[:, c:c + step])


def _norm_matmul(x, g, ws, tm):
    t, d = x.shape
    grid = (t // tm,)
    in_specs = [pl.BlockSpec((tm, d), lambda i: (i, 0)), _const_spec((1, d))]
    in_specs += [_const_spec(w.shape) for w in ws]
    out_specs = [pl.BlockSpec((tm, w.shape[1]), lambda i: (i, 0)) for w in ws]
    out_shape = [jax.ShapeDtypeStruct((t, w.shape[1]), F32) for w in ws]
    return pl.pallas_call(
        functools.partial(_norm_matmul_kernel, n_chunk=512),
        grid=grid, in_specs=in_specs, out_specs=out_specs, out_shape=out_shape,
        compiler_params=_params(1), name="norm_matmul",
    )(x, g, *ws)


def _proj_norm_res_kernel(a_ref, w_ref, g_ref, x_ref, o_ref):
    y = _dot(a_ref[...], w_ref[...])
    o_ref[...] = x_ref[...] + _rms(y, g_ref[...])


def _proj_norm_res(a, w, g, x, tm):
    t, k = a.shape
    d = x.shape[1]
    return pl.pallas_call(
        _proj_norm_res_kernel,
        grid=(t // tm,),
        in_specs=[pl.BlockSpec((tm, k), lambda i: (i, 0)), _const_spec(w.shape), _const_spec((1, d)),
                  pl.BlockSpec((tm, d), lambda i: (i, 0))],
        out_specs=pl.BlockSpec((tm, d), lambda i: (i, 0)),
        out_shape=jax.ShapeDtypeStruct((t, d), F32),
        compiler_params=_params(1), name="proj_norm_res",
    )(a, w, g, x)


KV_LANE_TILES = X_HEAD_DIM // LANES
KV_ROWS = KV_LANE_TILES * X_HEADS


def _kv_flat(kv):
    lead = kv.shape[:-3]
    x = kv.reshape(lead + (N_MEM, X_HEADS, KV_LANE_TILES, LANES))
    return jnp.swapaxes(x, -3, -2).reshape(lead + (N_MEM * KV_ROWS, LANES))


def _kv_unflat(flat):
    lead = flat.shape[:-2]
    x = flat.reshape(lead + (N_MEM, KV_LANE_TILES, X_HEADS, LANES))
    return jnp.swapaxes(x, -3, -2).reshape(lead + (N_MEM, X_HEADS, X_HEAD_DIM))


def _mem_kv_kernel(m_ref, g_ref, wk_ref, wv_ref, k_ref, v_ref):
    h = _rms(m_ref[...], g_ref[...]).astype(BF16)
    tm = m_ref.shape[0]
    for w_ref, o_ref in ((wk_ref, k_ref), (wv_ref, v_ref)):
        y = _dot(h, w_ref[...])
        pieces = [y[:, hd * X_HEAD_DIM + t * LANES:hd * X_HEAD_DIM + (t + 1) * LANES][None]
                  for t in range(KV_LANE_TILES) for hd in range(X_HEADS)]
        rows = jnp.swapaxes(jnp.concatenate(pieces, axis=0), 0, 1)
        o_ref[...] = rows.reshape(o_ref.shape)


def _mem_kv(mem, g, wk, wv, nb):
    t, d = mem.shape
    bsz = t // N_MEM
    tm = nb * N_MEM
    w_spec = pl.BlockSpec((None, d, d), lambda l, i: (l, 0, 0))
    o_spec = pl.BlockSpec((None, nb, N_MEM * KV_ROWS, LANES), lambda l, i: (l, i, 0, 0))
    return pl.pallas_call(
        _mem_kv_kernel,
        grid=(DEPTH, bsz // nb),
        in_specs=[pl.BlockSpec((tm, d), lambda l, i: (i, 0)),
                  pl.BlockSpec((None, 1, d), lambda l, i: (l, 0, 0)), w_spec, w_spec],
        out_specs=[o_spec, o_spec],
        out_shape=[jax.ShapeDtypeStruct((DEPTH, bsz, N_MEM * KV_ROWS, LANES), F32)] * 2,
        compiler_params=_params(2), name="mem_kv",
    )(mem, g, wk, wv)


def _attn_kernel(x_ref, k_ref, v_ref, wq_ref, wo_ref, g_ref, o_ref, q_scr, a_scr, k_scr, v_scr, *,
                 rows, hoist, seq_rows):
    i = pl.program_id(0) if hoist else None
    scale = X_HEAD_DIM ** -0.5
    n_keys = k_scr.shape[1]

    def split_heads():
        for ref, scr in ((k_ref, k_scr), (v_ref, v_scr)):
            xs = jnp.swapaxes(ref[...].reshape(n_keys, KV_ROWS, LANES), 0, 1)
            for h in range(X_HEADS):
                scr[h] = jnp.concatenate([xs[t * X_HEADS + h] for t in range(KV_LANE_TILES)],
                                         axis=1).astype(BF16)

    if hoist:
        split_heads()
    else:
        pl.when(pl.program_id(1) == 0)(split_heads)

    def project_q():
        h = _rms(x_ref[...], g_ref[2:3, :]).astype(BF16)
        q_scr[...] = (_dot(h, wq_ref[...]) * scale).astype(BF16)

    def project_out():
        y = _dot(a_scr[...], wo_ref[...])
        o_ref[...] = x_ref[...] + _rms(y, g_ref[3:4, :])

    if hoist:
        pl.when(i == 0)(project_q)
        r0 = pl.multiple_of(i * rows, rows)
        rsl = pl.ds(r0, rows)
    else:
        project_q()
        rsl = slice(None)

    if seq_rows is not None:
        qi = lax.broadcasted_iota(jnp.int32, (rows, n_keys), 0) // seq_rows
        ki = lax.broadcasted_iota(jnp.int32, (rows, n_keys), 1) // N_MEM
        mask = qi == ki
    for h in range(X_HEADS):
        hs = slice(h * X_HEAD_DIM, (h + 1) * X_HEAD_DIM)
        qh = q_scr[rsl, hs]
        s = _dot_nt(qh, k_scr[h])
        if seq_rows is not None:
            s = jnp.where(mask, s, -jnp.inf)
        e = jnp.exp(s - jnp.max(s, axis=-1, keepdims=True))
        p = e / jnp.sum(e, axis=-1, keepdims=True)
        a_scr[rsl, hs] = _dot(p.astype(BF16), v_scr[h]).astype(BF16)

    if hoist:
        pl.when(i == pl.num_programs(0) - 1)(project_out)
    else:
        project_out()


def _attn_prompt(x, k, v, layer, wq, wo, g, bsz, seq, tl):
    d = x.shape[1]
    nt = seq // tl
    kv_spec = pl.BlockSpec((None, None, N_MEM * KV_ROWS, LANES), lambda b, i: (layer, b, 0, 0))
    kv_scr = pltpu.VMEM((X_HEADS, N_MEM, X_HEAD_DIM), BF16)
    return pl.pallas_call(
        functools.partial(_attn_kernel, rows=tl, hoist=False, seq_rows=None),
        grid=(bsz, nt),
        in_specs=[pl.BlockSpec((tl, d), lambda b, i: (b * nt + i, 0)), kv_spec, kv_spec,
                  _const_spec(wq.shape), _const_spec(wo.shape), _const_spec(g.shape)],
        out_specs=pl.BlockSpec((tl, d), lambda b, i: (b * nt + i, 0)),
        out_shape=jax.ShapeDtypeStruct(x.shape, F32),
        scratch_shapes=[pltpu.VMEM((tl, d), BF16), pltpu.VMEM((tl, d), BF16), kv_scr, kv_scr],
        compiler_params=_params(2), name="attn_prompt",
    )(x, k, v, wq, wo, g)


def _attn_sample(x, k, v, layer, wq, wo, g, seq, nb):
    t, d = x.shape
    bsz = t // seq
    kv_spec = pl.BlockSpec((None, nb, N_MEM * KV_ROWS, LANES), lambda i: (layer, i, 0, 0))
    kv_scr = pltpu.VMEM((X_HEADS, nb * N_MEM, X_HEAD_DIM), BF16)
    return pl.pallas_call(
        functools.partial(_attn_kernel, rows=nb * seq, hoist=True, seq_rows=seq),
        grid=(bsz // nb,),
        in_specs=[_const_spec(x.shape), kv_spec, kv_spec,
                  _const_spec(wq.shape), _const_spec(wo.shape), _const_spec(g.shape)],
        out_specs=pl.BlockSpec(x.shape, lambda i: (0, 0)),
        out_shape=jax.ShapeDtypeStruct(x.shape, F32),
        scratch_shapes=[pltpu.VMEM((t, d), BF16), pltpu.VMEM((t, d), BF16), kv_scr, kv_scr],
        compiler_params=_params(1), name="attn_sample",
    )(x, k, v, wq, wo, g)


def _rg_block_ab(u3, prev3, blk, cw_ref, cb_ref, wax_ref, ba_ref, bx_ref, lam_ref):
    g = u3.shape[0]
    r = g * SUBLANES
    sl = slice(blk * LRU_BLOCK, (blk + 1) * LRU_BLOCK)
    conv = _conv8(u3, prev3, cw_ref[:, sl], cb_ref[:, sl]).reshape(r, LRU_BLOCK)
    pre = _dot(conv.astype(BF16), wax_ref[blk])
    rg = jax.nn.sigmoid(pre[:, :LRU_BLOCK] + ba_ref[:, sl])
    ig = jax.nn.sigmoid(pre[:, LRU_BLOCK:] + bx_ref[:, sl])
    a = jnp.exp(rg * (-LRU_C * jax.nn.softplus(-lam_ref[:, sl])))
    b = jnp.sqrt(1.0 - a * a) * (ig * conv)
    return a.reshape(g, SUBLANES, LRU_BLOCK), b.reshape(g, SUBLANES, LRU_BLOCK)


def _rg_ab(u3, prev3, cw_ref, cb_ref, wax_ref, ba_ref, bx_ref, lam_ref):
    parts = [_rg_block_ab(u3[:, :, blk * LRU_BLOCK:(blk + 1) * LRU_BLOCK],
                          prev3[:, :, blk * LRU_BLOCK:(blk + 1) * LRU_BLOCK],
                          blk, cw_ref, cb_ref, wax_ref, ba_ref, bx_ref, lam_ref)
             for blk in range(LRU_BLOCKS)]
    return (jnp.concatenate([p[0] for p in parts], axis=2), jnp.concatenate([p[1] for p in parts], axis=2))


def _rg_layer_prompt_kernel(x_ref, g_ref, win_ref, cw_ref, cb_ref, wax_ref, ba_ref, bx_ref, lam_ref,
                            wout_ref, o_ref, utail_ref, htail_ref, ubuf, y_scr, hc):
    r = x_ref.shape[0]
    c = BRANCH
    g = r // SUBLANES

    @pl.when(pl.program_id(1) == 0)
    def _():
        ubuf[0:SUBLANES, :] = jnp.zeros((SUBLANES, c), F32)
        hc[...] = jnp.zeros_like(hc)

    x = x_ref[...]
    xn = _rms(x, g_ref[0:1, :]).astype(BF16)
    for blk in range(LRU_BLOCKS):
        sl = slice(blk * LRU_BLOCK, (blk + 1) * LRU_BLOCK)
        ubuf[SUBLANES:, sl] = _dot(xn, win_ref[:, sl])
        gate = _dot(xn, win_ref[:, c + blk * LRU_BLOCK:c + (blk + 1) * LRU_BLOCK])
        u3 = ubuf[SUBLANES:, sl].reshape(g, SUBLANES, LRU_BLOCK)
        prev3 = ubuf[0:r, sl].reshape(g, SUBLANES, LRU_BLOCK)
        a3, b3 = _rg_block_ab(u3, prev3, blk, cw_ref, cb_ref, wax_ref, ba_ref, bx_ref, lam_ref)
        tail = ubuf[r:, sl]
        ubuf[0:SUBLANES, sl] = tail
        utail_ref[:, sl] = tail
        a3, b3 = _scan8(a3, b3)
        hprev = hc[:, sl]
        hs = []
        for j in range(g):
            hj = a3[j] * hprev + b3[j]
            hs.append(hj)
            hprev = jnp.broadcast_to(hj[SUBLANES - 1:, :], (SUBLANES, LRU_BLOCK))
        hc[:, sl] = hprev
        htail_ref[:, sl] = hprev
        y_scr[:, sl] = (jnp.concatenate(hs, axis=0) * jax.nn.silu(gate)).astype(BF16)
    o_ref[...] = x + _rms(_dot(y_scr[...], wout_ref[...]), g_ref[1:2, :])


def _rg_sample_kernel(proj_ref, prev_ref, h0_ref, cw_ref, cb_ref, wax_ref, ba_ref, bx_ref, lam_ref,
                      y_ref, h_ref):
    r = proj_ref.shape[0]
    c = BRANCH
    g = r // SUBLANES
    u3 = proj_ref[:, :c].reshape(g, SUBLANES, c)
    prev3 = prev_ref[...].reshape(g, SUBLANES, c)
    a3, b3 = _rg_ab(u3, prev3, cw_ref, cb_ref, wax_ref, ba_ref, bx_ref, lam_ref)
    b3 = b3 + a3 * h0_ref[...].reshape(g, SUBLANES, c)
    _, h3 = _scan8(a3, b3)
    h = h3.reshape(r, c)
    h_ref[...] = h
    y_ref[...] = (h * jax.nn.silu(proj_ref[:, c:])).astype(BF16)


def _rg_weight_specs(p):
    return [_const_spec(p["conv_w"].shape), _const_spec(p["conv_b"].shape), _const_spec(p["w_ax"].shape),
            _const_spec(p["b_a"].shape), _const_spec(p["b_x"].shape), _const_spec(p["lam"].shape)]


def _rg_weights(p):
    return (p["conv_w"], p["conv_b"], p["w_ax"], p["b_a"], p["b_x"], p["lam"])


def _rg_layer_prompt(x, g, p, bsz, seq, tl):
    nt = seq // tl
    c = BRANCH
    d = x.shape[1]
    x_spec = pl.BlockSpec((tl, d), lambda b, i: (b * nt + i, 0))
    tail_spec = pl.BlockSpec((None, SUBLANES, c), lambda b, i: (b, 0, 0))
    tail_shape = jax.ShapeDtypeStruct((bsz, SUBLANES, c), F32)
    return pl.pallas_call(
        _rg_layer_prompt_kernel,
        grid=(bsz, nt),
        in_specs=[x_spec, _const_spec(g.shape), _const_spec(p["w_in"].shape)] + _rg_weight_specs(p)
                 + [_const_spec(p["w_out"].shape)],
        out_specs=[x_spec, tail_spec, tail_spec],
        out_shape=[jax.ShapeDtypeStruct(x.shape, F32), tail_shape, tail_shape],
        scratch_shapes=[pltpu.VMEM((SUBLANES + tl, c), F32), pltpu.VMEM((tl, c), BF16),
                        pltpu.VMEM((SUBLANES, c), F32)],
        compiler_params=_params(2), name="rg_layer_prompt",
    )(x, g, p["w_in"], *_rg_weights(p), p["w_out"])


def _rg_core_sample(proj, prev8, h0pad, p, tm):
    t = proj.shape[0]
    c = BRANCH
    row = lambda w: pl.BlockSpec((tm, w), lambda i: (i, 0))
    return pl.pallas_call(
        _rg_sample_kernel,
        grid=(t // tm,),
        in_specs=[row(2 * c), row(c), row(c)] + _rg_weight_specs(p),
        out_specs=[row(c), row(c)],
        out_shape=[jax.ShapeDtypeStruct((t, c), BF16), jax.ShapeDtypeStruct((t, c), F32)],
        compiler_params=_params(1), name="rg_core_sample",
    )(proj, prev8, h0pad, *_rg_weights(p))


def _ssd_chunk(z, u3, prev3, dt_raw, s_read, s_write, cw_ref, cb_ref, dtb_ref, alog_ref, dexp_ref,
               ng_ref, acs_carry):
    q = z.shape[0]
    n = SSD_STATE
    xbc = jax.nn.silu(_conv8(u3, prev3, cw_ref[...], cb_ref[...]).reshape(q, SSD_CONV_DIM))
    xs = xbc[:, :BRANCH]
    bm = xbc[:, BRANCH:BRANCH + SSD_GROUPS * n]
    cm = xbc[:, BRANCH + SSD_GROUPS * n:]
    dt = jax.nn.softplus(dt_raw + dtb_ref[...])
    a = -jnp.exp(alog_ref[...])
    da3 = _cumsum8((dt * a).reshape(q // SUBLANES, SUBLANES, LANES))
    rows, carry = [], acs_carry
    for j in range(q // SUBLANES):
        blk = da3[j] + carry
        carry = jnp.broadcast_to(blk[SUBLANES - 1:, :], (SUBLANES, LANES))
        rows.append(blk)
    acs = jnp.concatenate(rows, axis=0) if len(rows) > 1 else rows[0]
    last = acs[q - 1:q, :]
    if q % LANES == 0:
        acs_t, dt_t = acs.T, dt.T
    else:
        pad = jnp.zeros((LANES - q, LANES), F32)
        acs_t = jnp.concatenate([acs, pad], axis=0).T[:, :q]
        dt_t = jnp.concatenate([dt, pad], axis=0).T[:, :q]
    causal = (lax.broadcasted_iota(jnp.int32, (q, q), 0) >= lax.broadcasted_iota(jnp.int32, (q, q), 1))
    lane = lax.broadcasted_iota(jnp.int32, (q, LANES), 1)
    srow = lax.broadcasted_iota(jnp.int32, (LANES, LANES), 0)
    half = SSD_HEAD_DIM
    y_pairs = []
    for g in range(SSD_GROUPS):
        bm_g = bm[:, g * n:(g + 1) * n]
        cm_g = cm[:, g * n:(g + 1) * n]
        cb_g = _dot_nt(cm_g.astype(BF16), bm_g.astype(BF16))
        for jp in range(2):
            pair = 2 * g + jp
            m_parts, cce_parts, bcw_parts, cds = [], [], [], []
            for h in (2 * pair, 2 * pair + 1):
                colb = jnp.broadcast_to(acs[:, h:h + 1], (q, LANES))
                dcol = jnp.broadcast_to(dt[:, h:h + 1], (q, LANES))
                seg = colb[:, :q] - acs_t[h:h + 1, :]
                decay = jnp.exp(jnp.where(causal, seg, -jnp.inf))
                m_parts.append(cb_g * decay * dt_t[h:h + 1, :])
                cce_parts.append(cm_g * jnp.exp(colb))
                lastb = last[:, h:h + 1]
                bcw_parts.append(bm_g * (dcol * jnp.exp(lastb - colb)))
                cds.append(jnp.exp(lastb))
            xs_pair = xs[:, pair * LANES:(pair + 1) * LANES]
            top = jnp.where(lane < half, xs_pair, 0.0)
            bot = jnp.where(lane >= half, xs_pair, 0.0)
            w = jnp.concatenate([top, bot], axis=0).astype(BF16)
            if q % LANES == 0:
                yd = _dot(jnp.concatenate(m_parts, axis=1).astype(BF16), w)
            else:
                yd = _dot(m_parts[0], top) + _dot(m_parts[1], bot)
            s_pair = s_read(pair)
            s_blk = jnp.concatenate([jnp.where(srow < half, s_pair, 0.0),
                                     jnp.where(srow >= half, s_pair, 0.0)], axis=1).astype(BF16)
            yo = _dot_nt(jnp.concatenate(cce_parts, axis=1).astype(BF16), s_blk)
            ds = _dot_tn(w, jnp.concatenate(bcw_parts, axis=0).astype(BF16))
            cd = jnp.where(srow < half, jnp.broadcast_to(cds[0], (LANES, LANES)),
                           jnp.broadcast_to(cds[1], (LANES, LANES)))
            s_write(pair, s_pair * cd + ds)
            y_pairs.append(yd + yo + dexp_ref[:, pair * LANES:(pair + 1) * LANES] * xs_pair)
    gw = BRANCH // SSD_GROUPS
    y_groups = []
    for g in range(SSD_GROUPS):
        yg = jnp.concatenate(y_pairs[2 * g:2 * g + 2], axis=1) * jax.nn.silu(z[:, g * gw:(g + 1) * gw])
        y_groups.append(yg * lax.rsqrt(jnp.mean(yg * yg, axis=-1, keepdims=True) + EPS))
    return jnp.concatenate(y_groups, axis=1) * ng_ref[...], carry


def _ssd_prompt_kernel(zx_ref, dt_ref, cw_ref, cb_ref, dtb_ref, alog_ref, dexp_ref, ng_ref,
                       y_ref, sout_ref, xbuf, s_scr):
    q = zx_ref.shape[0]
    g = q // SUBLANES
    c = SSD_CONV_DIM

    @pl.when(pl.program_id(1) == 0)
    def _():
        xbuf[0:SUBLANES, :] = jnp.zeros((SUBLANES, c), F32)
        s_scr[...] = jnp.zeros_like(s_scr)

    xbuf[SUBLANES:, :] = zx_ref[:, BRANCH:]
    u3 = xbuf[SUBLANES:, :].reshape(g, SUBLANES, c)
    prev3 = xbuf[0:q, :].reshape(g, SUBLANES, c)

    def s_read(pair):
        return s_scr[pair * LANES:(pair + 1) * LANES, :]

    def s_write(pair, val):
        s_scr[pair * LANES:(pair + 1) * LANES, :] = val

    y, _ = _ssd_chunk(zx_ref[:, :BRANCH], u3, prev3, dt_ref[...], s_read, s_write, cw_ref, cb_ref,
                      dtb_ref, alog_ref, dexp_ref, ng_ref, jnp.zeros((SUBLANES, LANES), F32))
    xbuf[0:SUBLANES, :] = xbuf[q:, :]
    y_ref[...] = y.astype(BF16)

    @pl.when(pl.program_id(1) == pl.num_programs(1) - 1)
    def _():
        sout_ref[...] = s_scr[...]


def _ssd_sample_kernel(zx_ref, dt_ref, prev_ref, s0_ref, cw_ref, cb_ref, dtb_ref, alog_ref, dexp_ref,
                       ng_ref, y_ref, sout_ref):
    q = zx_ref.shape[0]
    c = SSD_CONV_DIM
    u3 = zx_ref[:, BRANCH:].reshape(1, q, c)
    prev3 = prev_ref[...].reshape(1, q, c)

    def s_read(pair):
        return s0_ref[pair * LANES:(pair + 1) * LANES, :]

    def s_write(pair, val):
        sout_ref[pair * LANES:(pair + 1) * LANES, :] = val

    y, _ = _ssd_chunk(zx_ref[:, :BRANCH], u3, prev3, dt_ref[...], s_read, s_write, cw_ref, cb_ref,
                      dtb_ref, alog_ref, dexp_ref, ng_ref, jnp.zeros((SUBLANES, LANES), F32))
    y_ref[...] = y.astype(BF16)


def _ssd_weight_specs(p):
    return [_const_spec(p[k].shape) for k in ("conv_w", "conv_b", "dt_bias", "a_log", "d_exp", "norm_g")]


def _ssd_weights(p):
    return tuple(p[k] for k in ("conv_w", "conv_b", "dt_bias", "a_log", "d_exp", "norm_g"))


def _ssd_core_prompt(zx, dt, p, bsz, seq):
    q = SSD_CHUNK
    nt = seq // q
    hp = SSD_HEADS * SSD_HEAD_DIM
    return pl.pallas_call(
        _ssd_prompt_kernel,
        grid=(bsz, nt),
        in_specs=[pl.BlockSpec((q, zx.shape[1]), lambda b, i: (b * nt + i, 0)),
                  pl.BlockSpec((q, LANES), lambda b, i: (b * nt + i, 0))] + _ssd_weight_specs(p),
        out_specs=[pl.BlockSpec((q, BRANCH), lambda b, i: (b * nt + i, 0)),
                   pl.BlockSpec((None, hp, SSD_STATE), lambda b, i: (b, 0, 0))],
        out_shape=[jax.ShapeDtypeStruct((bsz * seq, BRANCH), BF16),
                   jax.ShapeDtypeStruct((bsz, hp, SSD_STATE), F32)],
        scratch_shapes=[pltpu.VMEM((SUBLANES + q, SSD_CONV_DIM), F32), pltpu.VMEM((hp, SSD_STATE), F32)],
        compiler_params=_params(2), name="ssd_core_prompt",
    )(zx, dt, *_ssd_weights(p))


def _ssd_core_sample(zx, dt, prev8, s0, idx, p, seq):
    t = zx.shape[0]
    bsz = t // seq
    hp = SSD_HEADS * SSD_HEAD_DIM
    row = lambda w: pl.BlockSpec((seq, w), lambda i: (i, 0))
    st = pl.BlockSpec((None, hp, SSD_STATE), lambda i: (i, 0, 0))
    st_in = pl.BlockSpec((None, None, hp, SSD_STATE), lambda i: (idx, i, 0, 0))
    return pl.pallas_call(
        _ssd_sample_kernel,
        grid=(bsz,),
        in_specs=[row(zx.shape[1]), row(LANES), row(SSD_CONV_DIM), st_in] + _ssd_weight_specs(p),
        out_specs=[row(BRANCH), st],
        out_shape=[jax.ShapeDtypeStruct((t, BRANCH), BF16), jax.ShapeDtypeStruct((bsz, hp, SSD_STATE), F32)],
        compiler_params=_params(1), name="ssd_core_sample",
    )(zx, dt, prev8, s0, *_ssd_weights(p))


def _hg_lower_bound(rows, layer):
    mx = functools.reduce(jnp.maximum, rows)
    es = [jnp.exp(x - mx) for x in rows]
    return sum(es[1:layer + 1]) / sum(es)


def _hg_prompt_kernel(proj_ref, hlb_ref, ng_ref, y_ref, sout_ref, st_scr, *, layer):
    r = proj_ref.shape[0]
    c = BRANCH
    dk = HGRN_KEY_DIM
    blk, sub = HGRN_BLOCK, HGRN_CHUNK
    nblk, nsub = r // blk, blk // sub

    @pl.when(pl.program_id(1) == 0)
    def _():
        st_scr[...] = jnp.zeros_like(st_scr)

    causal = (lax.broadcasted_iota(jnp.int32, (blk, blk), 0)
              >= lax.broadcasted_iota(jnp.int32, (blk, blk), 1))

    def head_body(h, carry):
        def lanes(part):
            return pl.ds(pl.multiple_of(part * c + h * dk, dk), dk)

        lb = _hg_lower_bound([hlb_ref[j:j + 1, lanes(0)] for j in range(DEPTH)], layer)
        f = proj_ref[:, lanes(1)]
        forget = lb + (1.0 - lb) * jax.nn.sigmoid(f)
        k = ((1.0 - lb) * jax.nn.sigmoid(-f)).reshape(nblk, blk, dk)
        x = _cumsum8(jnp.log(forget).reshape(r // SUBLANES, SUBLANES, dk)).reshape(nblk, blk, dk)
        parts, carry_row = [], None
        for j in range(blk // SUBLANES):
            part = x[:, j * SUBLANES:(j + 1) * SUBLANES, :]
            if carry_row is not None:
                part = part + carry_row
            carry_row = jnp.broadcast_to(part[:, SUBLANES - 1:, :], part.shape)
            parts.append(part)
        gc = jnp.concatenate(parts, axis=1)
        ends = [gc[:, (i + 1) * sub - 1:(i + 1) * sub, :] for i in range(nsub)]
        starts = [jnp.zeros_like(ends[0])] + ends[:-1]
        spread = lambda rows_: jnp.concatenate(
            [jnp.broadcast_to(x_, (nblk, sub, dk)) for x_ in rows_], axis=1)
        b_prev, b_next = spread(starts), spread(ends)
        qi = jax.nn.silu(proj_ref[:, lanes(0)]).reshape(nblk, blk, dk) * jnp.exp(gc - b_prev)
        qc = qi * jnp.exp(b_prev)
        kd = k * jnp.exp(b_prev - gc)
        ke = kd * jnp.exp(b_next - b_prev)
        kend = ke * jnp.exp(ends[-1] - b_next)
        v = proj_ref[:, lanes(2)].reshape(nblk, blk, dk)

        st = st_scr[h]
        outs = []
        for b in range(nblk):
            qi_b = qi[b].astype(BF16)
            att_rows = []
            for i in range(nsub):
                keys = []
                for j in range(nsub):
                    rs = slice(j * sub, (j + 1) * sub)
                    if j == i:
                        keys.append(kd[b, rs])
                    elif j < i - 1:
                        keys.append(ke[b, rs] * jnp.exp(starts[i][b] - ends[j][b]))
                    else:
                        keys.append(ke[b, rs])
                keys = jnp.concatenate(keys, axis=0).astype(BF16)
                att_rows.append(_dot_nt(qi_b[i * sub:(i + 1) * sub], keys))
            att = jnp.where(causal, jnp.concatenate(att_rows, axis=0), 0.0).astype(BF16)
            vb = v[b].astype(BF16)
            outs.append(_dot(att, vb) + _dot_nt(qc[b].astype(BF16), st.astype(BF16)))
            st = st * jnp.exp(ends[-1][b]) + _dot_tn(vb, kend[b].astype(BF16))
        st_scr[h] = st
        o = jnp.concatenate(outs, axis=0)
        o = o * lax.rsqrt(jnp.mean(o * o, axis=-1, keepdims=True) + EPS)
        gate = jax.nn.silu(proj_ref[:, lanes(3)])
        y_ref[:, lanes(0)] = (o * ng_ref[:, lanes(0)] * gate).astype(BF16)
        return carry

    lax.fori_loop(0, HGRN_HEADS, head_body, 0, unroll=2)

    @pl.when(pl.program_id(1) == pl.num_programs(1) - 1)
    def _():
        for h in range(HGRN_HEADS):
            sout_ref[h] = st_scr[h].T


def _hg_seq_kernel(proj_ref, hlb_ref, ng_ref, s0_ref, y_ref, sout_ref,
                   qg_scr, kg_scr, ke_scr, v_scr, dec_scr, o_scr, *, layer, chunk):
    r = proj_ref.shape[0]
    c = BRANCH
    g = r // SUBLANES
    n_chunks = r // chunk
    dk, dv = HGRN_KEY_DIM, HGRN_VAL_DIM
    assert chunk == SUBLANES

    lb = _hg_lower_bound([hlb_ref[j:j + 1, :] for j in range(DEPTH)], layer)
    f = proj_ref[:, c:2 * c]
    forget = lb + (1.0 - lb) * jax.nn.sigmoid(f)
    k = (1.0 - lb) * jax.nn.sigmoid(-f)
    gcum3 = _cumsum8(jnp.log(forget).reshape(g, SUBLANES, c))
    gcum = gcum3.reshape(r, c)
    glast = jnp.broadcast_to(gcum3[:, SUBLANES - 1:, :], gcum3.shape).reshape(r, c)
    qg_scr[...] = jax.nn.silu(proj_ref[:, :c]) * jnp.exp(gcum)
    kg_scr[...] = k * jnp.exp(-gcum)
    ke_scr[...] = k * jnp.exp(glast - gcum)
    dec_scr[...] = jnp.exp(glast)
    v_scr[...] = proj_ref[:, 2 * c:3 * c]

    causal = (lax.broadcasted_iota(jnp.int32, (chunk, chunk), 0)
              >= lax.broadcasted_iota(jnp.int32, (chunk, chunk), 1))

    def chunk_body(ci, carry):
        rs = pl.ds(pl.multiple_of(ci * chunk, chunk), chunk)
        outs = []
        for h in range(HGRN_HEADS):
            ks = slice(h * dk, (h + 1) * dk)
            vs = slice(h * dv, (h + 1) * dv)
            qg = qg_scr[rs, ks].astype(BF16)
            vv = v_scr[rs, vs].astype(BF16)
            st = s0_ref[ci, h].T
            att = jnp.where(causal, _dot_nt(qg, kg_scr[rs, ks].astype(BF16)), 0.0)
            outs.append(_dot(att.astype(BF16), vv) + _dot_nt(qg, st.astype(BF16)))
            dec = dec_scr[rs, ks][chunk - 1:, :]
            sout_ref[ci, h] = (st * dec + _dot_tn(vv, ke_scr[rs, ks].astype(BF16))).T
        o_scr[rs, :] = jnp.concatenate(outs, axis=1)
        return carry

    lax.fori_loop(0, n_chunks, chunk_body, 0)

    gate = jax.nn.silu(proj_ref[:, 3 * c:])
    parts = []
    for h in range(HGRN_HEADS):
        o = o_scr[:, h * dv:(h + 1) * dv]
        parts.append(o * lax.rsqrt(jnp.mean(o * o, axis=-1, keepdims=True) + EPS))
    y_ref[...] = (jnp.concatenate(parts, axis=1) * ng_ref[...] * gate).astype(BF16)


def _hg_core_prompt(proj, hlb, ng, layer, bsz, seq, tl):
    nt = seq // tl
    c = BRANCH
    st_shape = (HGRN_HEADS, HGRN_KEY_DIM, HGRN_VAL_DIM)
    return pl.pallas_call(
        functools.partial(_hg_prompt_kernel, layer=layer),
        grid=(bsz, nt),
        in_specs=[pl.BlockSpec((tl, 4 * c), lambda b, i: (b * nt + i, 0)),
                  _const_spec(hlb.shape), _const_spec(ng.shape)],
        out_specs=[pl.BlockSpec((tl, c), lambda b, i: (b * nt + i, 0)),
                   pl.BlockSpec((None,) + st_shape, lambda b, i: (b, 0, 0, 0))],
        out_shape=[jax.ShapeDtypeStruct((bsz * seq, c), BF16),
                   jax.ShapeDtypeStruct((bsz,) + st_shape, F32)],
        scratch_shapes=[pltpu.VMEM((HGRN_HEADS, HGRN_VAL_DIM, HGRN_KEY_DIM), F32)],
        compiler_params=_params(2), name="hg_core_prompt",
    )(proj, hlb, ng)


def _hg_core_sample(proj, s0, idx, hlb, ng, layer, seq, nb):
    t = proj.shape[0]
    bsz = t // seq
    c = BRANCH
    rows = nb * seq
    st_shape = (nb, HGRN_HEADS, HGRN_KEY_DIM, HGRN_VAL_DIM)
    st_spec = pl.BlockSpec(st_shape, lambda i: (i, 0, 0, 0))
    st_in = pl.BlockSpec((None,) + st_shape, lambda i: (idx, i, 0, 0, 0))
    return pl.pallas_call(
        functools.partial(_hg_seq_kernel, layer=layer, chunk=seq),
        grid=(bsz // nb,),
        in_specs=[pl.BlockSpec((rows, 4 * c), lambda i: (i, 0)),
                  _const_spec(hlb.shape), _const_spec(ng.shape), st_in],
        out_specs=[pl.BlockSpec((rows, c), lambda i: (i, 0)), st_spec],
        out_shape=[jax.ShapeDtypeStruct((t, c), BF16), jax.ShapeDtypeStruct(s0.shape[1:], F32)],
        scratch_shapes=[pltpu.VMEM((rows, c), F32)] * 6,
        compiler_params=_params(1), name="hg_core_sample",
    )(proj, hlb, ng, s0)


def _tile(n, target):
    t = min(n, target)
    assert n % t == 0, (n, target)
    return t


def _pad_groups(state, first_row):
    n, k, c = state.shape
    return jnp.pad(state, ((0, 0), (first_row, SUBLANES - first_row - k), (0, 0))).reshape(n * SUBLANES, c)


def _trunk(x, mem_k, mem_v, states, w, bsz, seq, prompt):
    tm = _tile(x.shape[0], 256)
    rg_conv, rg_h, ssd_conv, ssd_s, hg_s = [], [], [], [], []
    tail = slice(seq - (CONV_W - 1), seq)
    for layer in range(DEPTH):
        kind, idx = layer % N_MIXERS, layer // N_MIXERS
        g = w["norm_g"][layer]
        if kind == 0:
            p = w["rg"][idx]
            if prompt:
                x, utail, htail = _rg_layer_prompt(x, g, p, bsz, seq, _tile(seq, 256))
                rg_h.append(htail[:, SUBLANES - 1])
                rg_conv.append(utail[:, SUBLANES - (CONV_W - 1):])
                y = None
            else:
                (proj,) = _norm_matmul(x, g[0:1], [p["w_in"]], tm)
                prev8 = _pad_groups(states["rg_conv"][idx], SUBLANES - (CONV_W - 1))
                h0pad = _pad_groups(states["rg_h"][idx][:, None, :], 0)
                y, h = _rg_core_sample(proj, prev8, h0pad, p, tm)
                rg_h.append(h.reshape(bsz, seq, BRANCH)[:, seq - 1])
                rg_conv.append(proj.reshape(bsz, seq, 2 * BRANCH)[:, tail, :BRANCH])
        elif kind == 1:
            p = w["ssd"][idx]
            zx, dt = _norm_matmul(x, g[0:1], [p["w_zx"], p["w_dt"]], tm)
            if prompt:
                y, s_new = _ssd_core_prompt(zx, dt, p, bsz, seq)
            else:
                prev8 = _pad_groups(states["ssd_conv"][idx], SUBLANES - (CONV_W - 1))
                s0 = states["ssd_s"].reshape(-1, bsz, SSD_HEADS * SSD_HEAD_DIM, SSD_STATE)
                y, s_new = _ssd_core_sample(zx, dt, prev8, s0, idx, p, seq)
            ssd_s.append(s_new.reshape(bsz, SSD_HEADS, SSD_HEAD_DIM, SSD_STATE))
            ssd_conv.append(zx.reshape(bsz, seq, BRANCH + SSD_CONV_DIM)[:, tail, BRANCH:])
        else:
            p = w["hg"][idx]
            (proj,) = _norm_matmul(x, g[0:1], [p["w_in"]], tm)
            if prompt:
                y, s_new = _hg_core_prompt(proj, w["hg_lower_bounds"], p["norm_g"], layer, bsz, seq,
                                           _tile(seq, 256))
            else:
                y, s_new = _hg_core_sample(proj, states["hg_s"], idx, w["hg_lower_bounds"], p["norm_g"],
                                           layer, seq, 8)
            hg_s.append(s_new)
        if y is not None:
            x = _proj_norm_res(y, p["w_out"], g[1:2], x, tm)
        if prompt:
            x = _attn_prompt(x, mem_k, mem_v, layer, w["x_w_q"][layer], w["x_w_o"][layer], g,
                             bsz, seq, _tile(seq, 512))
        else:
            x = _attn_sample(x, mem_k, mem_v, layer, w["x_w_q"][layer], w["x_w_o"][layer], g,
                             seq, 4)
    return x, jnp.stack(rg_conv), jnp.stack(rg_h), jnp.stack(ssd_conv), jnp.stack(ssd_s), jnp.stack(hg_s)


def kernel(x_prompt, x_sample, mem_prompt, state_rglru_conv, state_rglru_h, state_ssd_conv, state_ssd,
           state_hgrn, cache_mem_k, cache_mem_v, norm_g, mem_norm_g, rg_w_in, rg_conv_w, rg_conv_b, rg_w_a,
           rg_b_a, rg_w_x, rg_b_x, rg_lambda, rg_w_out, ssd_w_in, ssd_conv_w, ssd_conv_b, ssd_dt_bias,
           ssd_a_log, ssd_d, ssd_norm_g, ssd_w_out, hg_w_in, hg_lower_bounds, hg_norm_g, hg_w_out,
           x_w_q, x_w_k, x_w_v, x_w_o):
    bp, sp, d = x_prompt.shape
    bs, ss, _ = x_sample.shape
    n_a, n_b, n_c = rg_w_in.shape[0], ssd_w_in.shape[0], hg_w_in.shape[0]
    pad_heads = lambda v: jnp.pad(v, (0, LANES - SSD_HEADS))[None, :]
    w = {
        "norm_g": norm_g,
        "hg_lower_bounds": hg_lower_bounds,
        "x_w_q": x_w_q.astype(BF16),
        "x_w_o": x_w_o.astype(BF16),
        "rg": [{
            "w_in": rg_w_in[i].astype(BF16),
            "conv_w": rg_conv_w[i], "conv_b": rg_conv_b[i][None, :],
            "w_ax": jnp.concatenate([rg_w_a[i], rg_w_x[i]], axis=-1).astype(BF16),
            "b_a": rg_b_a[i][None, :], "b_x": rg_b_x[i][None, :], "lam": rg_lambda[i][None, :],
            "w_out": rg_w_out[i].astype(BF16),
        } for i in range(n_a)],
        "ssd": [{
            "w_zx": ssd_w_in[i][:, :BRANCH + SSD_CONV_DIM].astype(BF16),
            "w_dt": jnp.pad(ssd_w_in[i][:, BRANCH + SSD_CONV_DIM:], ((0, 0), (0, LANES - SSD_HEADS))).astype(BF16),
            "conv_w": ssd_conv_w[i], "conv_b": ssd_conv_b[i][None, :],
            "dt_bias": pad_heads(ssd_dt_bias[i]), "a_log": pad_heads(ssd_a_log[i]),
            "d_exp": jnp.repeat(ssd_d[i], SSD_HEAD_DIM)[None, :],
            "norm_g": ssd_norm_g[i][None, :],
            "w_out": ssd_w_out[i].astype(BF16),
        } for i in range(n_b)],
        "hg": [{
            "w_in": hg_w_in[i].astype(BF16),
            "norm_g": hg_norm_g[i][None, :],
            "w_out": hg_w_out[i].astype(BF16),
        } for i in range(n_c)],
    }

    mem_k_p, mem_v_p = _mem_kv(mem_prompt.reshape(bp * N_MEM, d), mem_norm_g[:, None, :],
                               x_w_k.astype(BF16), x_w_v.astype(BF16), _tile(bp, 2))
    y_p, rgc_p, rgh_p, sc_p, ss_p, hs_p = _trunk(
        x_prompt.reshape(bp * sp, d), mem_k_p, mem_v_p, None, w, bp, sp, True)
    states = {"rg_conv": state_rglru_conv, "rg_h": state_rglru_h, "ssd_conv": state_ssd_conv,
              "ssd_s": state_ssd, "hg_s": state_hgrn}
    y_s, rgc_s, rgh_s, sc_s, ss_s, hs_s = _trunk(
        x_sample.reshape(bs * ss, d), _kv_flat(cache_mem_k), _kv_flat(cache_mem_v), states, w, bs, ss, False)
    return (y_p.reshape(bp, sp, d), y_s.reshape(bs, ss, d), rgc_p, rgh_p, sc_p, ss_p, hs_p,
            _kv_unflat(mem_k_p), _kv_unflat(mem_v_p), rgc_s, rgh_s, sc_s, ss_s, hs_s)
```

```python
import functools

import jax
import jax.numpy as jnp
from jax import lax
from jax.experimental import pallas as pl
from jax.experimental.pallas import tpu as pltpu

F32 = jnp.float32
BF16 = jnp.bfloat16

D_MODEL = 1024
DEPTH = 4
N_MIXERS = 3
BRANCH = 2 * D_MODEL
CONV_W = 4
EPS = 1e-6
LRU_BLOCKS = 8
LRU_BLOCK = BRANCH // LRU_BLOCKS
LRU_C = 8.0
SSD_HEAD_DIM = 64
SSD_HEADS = BRANCH // SSD_HEAD_DIM
SSD_STATE = 128
SSD_GROUPS = 8
SSD_CONV_DIM = BRANCH + 2 * SSD_GROUPS * SSD_STATE
SSD_CHUNK = 128
HGRN_KEY_DIM = 128
HGRN_HEADS = BRANCH // HGRN_KEY_DIM
HGRN_VAL_DIM = BRANCH // HGRN_HEADS
HGRN_CHUNK = 16
HGRN_BLOCK = 4 * HGRN_CHUNK
N_MEM = 256
X_HEADS = 4
X_HEAD_DIM = D_MODEL // X_HEADS

SUBLANES = 8
LANES = 128
VMEM_BYTES_V7X = 64 * 1024 * 1024
VMEM_LIMIT = VMEM_BYTES_V7X * 7 // 8

NT_DIMS = (((1,), (1,)), ((), ()))
TN_DIMS = (((0,), (0,)), ((), ()))


def _params(n_grid_dims):
    return pltpu.CompilerParams(
        dimension_semantics=("arbitrary",) * n_grid_dims, vmem_limit_bytes=VMEM_LIMIT)


def _const_spec(shape):
    nd = len(shape)
    return pl.BlockSpec(shape, lambda *_: (0,) * nd, pipeline_mode=pl.Buffered(1))


def _rms(x, g):
    return x * lax.rsqrt(jnp.mean(x * x, axis=-1, keepdims=True) + EPS) * g


def _dot(a, b):
    return jnp.dot(a, b, preferred_element_type=F32)


def _dot_nt(a, b):
    return lax.dot_general(a, b, NT_DIMS, preferred_element_type=F32)


def _dot_tn(a, b):
    return lax.dot_general(a, b, TN_DIMS, preferred_element_type=F32)


def _group_iota(width):
    return lax.broadcasted_iota(jnp.int32, (1, SUBLANES, width), 1)


def _conv8(u3, prev3, cw, cb):
    t = _group_iota(u3.shape[-1])
    acc = cb + cw[CONV_W - 1:CONV_W, :] * u3
    for k in range(1, CONV_W):
        shifted = pltpu.roll(jnp.where(t >= SUBLANES - k, prev3, u3), k, 1)
        acc = acc + cw[CONV_W - 1 - k:CONV_W - k, :] * shifted
    return acc


def _scan8(a3, b3):
    t = _group_iota(a3.shape[-1])
    for s in (1, 2, 4):
        m = t >= s
        a_sh = pltpu.roll(a3, s, 1)
        b_sh = pltpu.roll(b3, s, 1)
        b3 = jnp.where(m, a3 * b_sh + b3, b3)
        a3 = jnp.where(m, a3 * a_sh, a3)
    return a3, b3


def _cumsum8(x3):
    t = _group_iota(x3.shape[-1])
    for s in (1, 2, 4):
        x3 = x3 + jnp.where(t >= s, pltpu.roll(x3, s, 1), 0.0)
    return x3


def _norm_matmul_kernel(x_ref, g_ref, *refs, n_chunk):
    n_w = len(refs) // 2
    h = _rms(x_ref[...], g_ref[...]).astype(BF16)
    for w_ref, o_ref in zip(refs[:n_w], refs[n_w:]):
        n = o_ref.shape[-1]
        step = min(n_chunk, n)
        for c in range(0, n, step):
            o_ref[:, c:c + step] = _dot(h, w_ref[:, c:c + step])


def _norm_matmul(x, g, ws, tm):
    t, d = x.shape
    grid = (t // tm,)
    in_specs = [pl.BlockSpec((tm, d), lambda i: (i, 0)), _const_spec((1, d))]
    in_specs += [_const_spec(w.shape) for w in ws]
    out_specs = [pl.BlockSpec((tm, w.shape[1]), lambda i: (i, 0)) for w in ws]
    out_shape = [jax.ShapeDtypeStruct((t, w.shape[1]), F32) for w in ws]
    return pl.pallas_call(
        functools.partial(_norm_matmul_kernel, n_chunk=512),
        grid=grid, in_specs=in_specs, out_specs=out_specs, out_shape=out_shape,
        compiler_params=_params(1), name="norm_matmul",
    )(x, g, *ws)


def _proj_norm_res_kernel(a_ref, w_ref, g_ref, x_ref, o_ref):
    y = _dot(a_ref[...], w_ref[...])
    o_ref[...] = x_ref[...] + _rms(y, g_ref[...])


def _proj_norm_res(a, w, g, x, tm):
    t, k = a.shape
    d = x.shape[1]
    return pl.pallas_call(
        _proj_norm_res_kernel,
        grid=(t // tm,),
        in_specs=[pl.BlockSpec((tm, k), lambda i: (i, 0)), _const_spec(w.shape), _const_spec((1, d)),
                  pl.BlockSpec((tm, d), lambda i: (i, 0))],
        out_specs=pl.BlockSpec((tm, d), lambda i: (i, 0)),
        out_shape=jax.ShapeDtypeStruct((t, d), F32),
        compiler_params=_params(1), name="proj_norm_res",
    )(a, w, g, x)


KV_LANE_TILES = X_HEAD_DIM // LANES
KV_ROWS = KV_LANE_TILES * X_HEADS


def _kv_flat(kv):
    lead = kv.shape[:-3]
    x = kv.reshape(lead + (N_MEM, X_HEADS, KV_LANE_TILES, LANES))
    return jnp.swapaxes(x, -3, -2).reshape(lead + (N_MEM * KV_ROWS, LANES))


def _kv_unflat(flat):
    lead = flat.shape[:-2]
    x = flat.reshape(lead + (N_MEM, KV_LANE_TILES, X_HEADS, LANES))
    return jnp.swapaxes(x, -3, -2).reshape(lead + (N_MEM, X_HEADS, X_HEAD_DIM))


def _mem_kv_kernel(m_ref, g_ref, wk_ref, wv_ref, k_ref, v_ref):
    h = _rms(m_ref[...], g_ref[...]).astype(BF16)
    tm = m_ref.shape[0]
    for w_ref, o_ref in ((wk_ref, k_ref), (wv_ref, v_ref)):
        y = _dot(h, w_ref[...])
        pieces = [y[:, hd * X_HEAD_DIM + t * LANES:hd * X_HEAD_DIM + (t + 1) * LANES][None]
                  for t in range(KV_LANE_TILES) for hd in range(X_HEADS)]
        rows = jnp.swapaxes(jnp.concatenate(pieces, axis=0), 0, 1)
        o_ref[...] = rows.reshape(o_ref.shape)


def _mem_kv(mem, g, wk, wv, nb):
    t, d = mem.shape
    bsz = t // N_MEM
    tm = nb * N_MEM
    w_spec = pl.BlockSpec((None, d, d), lambda l, i: (l, 0, 0))
    o_spec = pl.BlockSpec((None, nb, N_MEM * KV_ROWS, LANES), lambda l, i: (l, i, 0, 0))
    return pl.pallas_call(
        _mem_kv_kernel,
        grid=(DEPTH, bsz // nb),
        in_specs=[pl.BlockSpec((tm, d), lambda l, i: (i, 0)),
                  pl.BlockSpec((None, 1, d), lambda l, i: (l, 0, 0)), w_spec, w_spec],
        out_specs=[o_spec, o_spec],
        out_shape=[jax.ShapeDtypeStruct((DEPTH, bsz, N_MEM * KV_ROWS, LANES), F32)] * 2,
        compiler_params=_params(2), name="mem_kv",
    )(mem, g, wk, wv)


def _attn_kernel(x_ref, k_ref, v_ref, wq_ref, wo_ref, g_ref, o_ref, q_scr, a_scr, k_scr, v_scr, *,
                 rows, hoist, seq_rows):
    i = pl.program_id(0) if hoist else None
    scale = X_HEAD_DIM ** -0.5
    n_keys = k_scr.shape[1]

    def split_heads():
        for ref, scr in ((k_ref, k_scr), (v_ref, v_scr)):
            xs = jnp.swapaxes(ref[...].reshape(n_keys, KV_ROWS, LANES), 0, 1)
            for h in range(X_HEADS):
                scr[h] = jnp.concatenate([xs[t * X_HEADS + h] for t in range(KV_LANE_TILES)],
                                         axis=1).astype(BF16)

    if hoist:
        split_heads()
    else:
        pl.when(pl.program_id(1) == 0)(split_heads)

    def project_q():
        h = _rms(x_ref[...], g_ref[2:3, :]).astype(BF16)
        q_scr[...] = (_dot(h, wq_ref[...]) * scale).astype(BF16)

    def project_out():
        y = _dot(a_scr[...], wo_ref[...])
        o_ref[...] = x_ref[...] + _rms(y, g_ref[3:4, :])

    if hoist:
        pl.when(i == 0)(project_q)
        r0 = pl.multiple_of(i * rows, rows)
        rsl = pl.ds(r0, rows)
    else:
        project_q()
        rsl = slice(None)

    if seq_rows is not None:
        qi = lax.broadcasted_iota(jnp.int32, (rows, n_keys), 0) // seq_rows
        ki = lax.broadcasted_iota(jnp.int32, (rows, n_keys), 1) // N_MEM
        mask = qi == ki
    for h in range(X_HEADS):
        hs = slice(h * X_HEAD_DIM, (h + 1) * X_HEAD_DIM)
        qh = q_scr[rsl, hs]
        s = _dot_nt(qh, k_scr[h])
        if seq_rows is not None:
            s = jnp.where(mask, s, -jnp.inf)
        e = jnp.exp(s - jnp.max(s, axis=-1, keepdims=True))
        p = e / jnp.sum(e, axis=-1, keepdims=True)
        a_scr[rsl, hs] = _dot(p.astype(BF16), v_scr[h]).astype(BF16)

    if hoist:
        pl.when(i == pl.num_programs(0) - 1)(project_out)
    else:
        project_out()


def _attn_prompt(x, k, v, layer, wq, wo, g, bsz, seq, tl):
    d = x.shape[1]
    nt = seq // tl
    kv_spec = pl.BlockSpec((None, None, N_MEM * KV_ROWS, LANES), lambda b, i: (layer, b, 0, 0))
    kv_scr = pltpu.VMEM((X_HEADS, N_MEM, X_HEAD_DIM), BF16)
    return pl.pallas_call(
        functools.partial(_attn_kernel, rows=tl, hoist=False, seq_rows=None),
        grid=(bsz, nt),
        in_specs=[pl.BlockSpec((tl, d), lambda b, i: (b * nt + i, 0)), kv_spec, kv_spec,
                  _const_spec(wq.shape), _const_spec(wo.shape), _const_spec(g.shape)],
        out_specs=pl.BlockSpec((tl, d), lambda b, i: (b * nt + i, 0)),
        out_shape=jax.ShapeDtypeStruct(x.shape, F32),
        scratch_shapes=[pltpu.VMEM((tl, d), BF16), pltpu.VMEM((tl, d), BF16), kv_scr, kv_scr],
        compiler_params=_params(2), name="attn_prompt",
    )(x, k, v, wq, wo, g)


def _attn_sample(x, k, v, layer, wq, wo, g, seq, nb):
    t, d = x.shape
    bsz = t // seq
    kv_spec = pl.BlockSpec((None, nb, N_MEM * KV_ROWS, LANES), lambda i: (layer, i, 0, 0))
    kv_scr = pltpu.VMEM((X_HEADS, nb * N_MEM, X_HEAD_DIM), BF16)
    return pl.pallas_call(
        functools.partial(_attn_kernel, rows=nb * seq, hoist=True, seq_rows=seq),
        grid=(bsz // nb,),
        in_specs=[_const_spec(x.shape), kv_spec, kv_spec,
                  _const_spec(wq.shape), _const_spec(wo.shape), _const_spec(g.shape)],
        out_specs=pl.BlockSpec(x.shape, lambda i: (0, 0)),
        out_shape=jax.ShapeDtypeStruct(x.shape, F32),
        scratch_shapes=[pltpu.VMEM((t, d), BF16), pltpu.VMEM((t, d), BF16), kv_scr, kv_scr],
        compiler_params=_params(1), name="attn_sample",
    )(x, k, v, wq, wo, g)


def _rg_gates(conv, blk, wax_ref, ba_ref, bx_ref, lam_ref):
    sl = slice(blk * LRU_BLOCK, (blk + 1) * LRU_BLOCK)
    pre = _dot(conv.astype(BF16), wax_ref[blk])
    rg = jax.nn.sigmoid(pre[:, :LRU_BLOCK] + ba_ref[:, sl])
    ig = jax.nn.sigmoid(pre[:, LRU_BLOCK:] + bx_ref[:, sl])
    a = jnp.exp(rg * (-LRU_C * jax.nn.softplus(-lam_ref[:, sl])))
    b = jnp.exp(0.5 * jnp.log(1.0 - a * a)) * (ig * conv)
    return a, b


def _rg_block_ab(u3, prev3, blk, cw_ref, cb_ref, wax_ref, ba_ref, bx_ref, lam_ref):
    g = u3.shape[0]
    sl = slice(blk * LRU_BLOCK, (blk + 1) * LRU_BLOCK)
    conv = _conv8(u3, prev3, cw_ref[:, sl], cb_ref[:, sl]).reshape(g * SUBLANES, LRU_BLOCK)
    a, b = _rg_gates(conv, blk, wax_ref, ba_ref, bx_ref, lam_ref)
    return a.reshape(g, SUBLANES, LRU_BLOCK), b.reshape(g, SUBLANES, LRU_BLOCK)


def _rg_ab(u3, prev3, cw_ref, cb_ref, wax_ref, ba_ref, bx_ref, lam_ref):
    parts = [_rg_block_ab(u3[:, :, blk * LRU_BLOCK:(blk + 1) * LRU_BLOCK],
                          prev3[:, :, blk * LRU_BLOCK:(blk + 1) * LRU_BLOCK],
                          blk, cw_ref, cb_ref, wax_ref, ba_ref, bx_ref, lam_ref)
             for blk in range(LRU_BLOCKS)]
    return (jnp.concatenate([p[0] for p in parts], axis=2), jnp.concatenate([p[1] for p in parts], axis=2))


def _rg_layer_prompt_kernel(x_ref, g_ref, win_ref, cw_ref, cb_ref, wax_ref, ba_ref, bx_ref, lam_ref,
                            wout_ref, o_ref, utail_ref, htail_ref, tail_scr, y_scr, hc):
    r, d = x_ref.shape
    c = BRANCH
    n = r // SUBLANES
    taps = CONV_W - 1

    @pl.when(pl.program_id(1) == 0)
    def _():
        tail_scr[...] = jnp.zeros_like(tail_scr)
        hc[...] = jnp.zeros_like(hc)

    xs = jnp.swapaxes(x_ref[...].reshape(SUBLANES, n, d), 0, 1).reshape(r, d)
    xn = _rms(xs, g_ref[0:1, :]).astype(BF16)
    first_seg = lax.broadcasted_iota(jnp.int32, (SUBLANES, LRU_BLOCK), 0) == 0
    for blk in range(LRU_BLOCKS):
        sl = slice(blk * LRU_BLOCK, (blk + 1) * LRU_BLOCK)
        u3 = _dot(xn, win_ref[:, sl]).reshape(n, SUBLANES, LRU_BLOCK)
        gate = _dot(xn, win_ref[:, c + blk * LRU_BLOCK:c + (blk + 1) * LRU_BLOCK])
        hist = []
        for j in range(taps):
            prev_seg = pltpu.roll(u3[n - taps + j], 1, 0)
            hist.append(jnp.where(first_seg, tail_scr[j, :, sl], prev_seg)[None])
            tail_scr[j, :, sl] = prev_seg
            utail_ref[j, :, sl] = prev_seg
        uext = jnp.concatenate(hist + [u3], axis=0)
        conv = cb_ref[:, sl] + cw_ref[taps:CONV_W, sl] * u3
        for k in range(1, CONV_W):
            conv = conv + cw_ref[taps - k:CONV_W - k, sl] * uext[taps - k:taps - k + n]
        a, b = _rg_gates(conv.reshape(r, LRU_BLOCK), blk, wax_ref, ba_ref, bx_ref, lam_ref)
        a3 = a.reshape(n, SUBLANES, LRU_BLOCK)
        b3 = b.reshape(n, SUBLANES, LRU_BLOCK)
        hs, prods = [b3[0]], [a3[0]]
        for s in range(1, n):
            hs.append(a3[s] * hs[-1] + b3[s])
            prods.append(a3[s] * prods[-1])
        h_in = hc[0:1, sl]
        carry_rows = []
        for q in range(SUBLANES):
            carry_rows.append(h_in)
            h_in = hs[-1][q:q + 1] + prods[-1][q:q + 1] * h_in
        carry = jnp.concatenate(carry_rows, axis=0)
        h_last = jnp.broadcast_to(h_in, (SUBLANES, LRU_BLOCK))
        hc[:, sl] = h_last
        htail_ref[:, sl] = h_last
        h = jnp.concatenate([(hs[s] + prods[s] * carry)[None] for s in range(n)], axis=0)
        y_scr[:, sl] = (h.reshape(r, LRU_BLOCK) * jax.nn.silu(gate)).astype(BF16)
    out = xs + _rms(_dot(y_scr[...], wout_ref[...]), g_ref[1:2, :])
    o_ref[...] = jnp.swapaxes(out.reshape(n, SUBLANES, d), 0, 1).reshape(r, d)


def _rg_sample_kernel(proj_ref, prev_ref, h0_ref, cw_ref, cb_ref, wax_ref, ba_ref, bx_ref, lam_ref,
                      y_ref, h_ref):
    r = proj_ref.shape[0]
    c = BRANCH
    g = r // SUBLANES
    u3 = proj_ref[:, :c].reshape(g, SUBLANES, c)
    prev3 = prev_ref[...].reshape(g, SUBLANES, c)
    a3, b3 = _rg_ab(u3, prev3, cw_ref, cb_ref, wax_ref, ba_ref, bx_ref, lam_ref)
    b3 = b3 + a3 * h0_ref[...].reshape(g, SUBLANES, c)
    _, h3 = _scan8(a3, b3)
    h = h3.reshape(r, c)
    h_ref[...] = h
    y_ref[...] = (h * jax.nn.silu(proj_ref[:, c:])).astype(BF16)


def _rg_weight_specs(p):
    return [_const_spec(p["conv_w"].shape), _const_spec(p["conv_b"].shape), _const_spec(p["w_ax"].shape),
            _const_spec(p["b_a"].shape), _const_spec(p["b_x"].shape), _const_spec(p["lam"].shape)]


def _rg_weights(p):
    return (p["conv_w"], p["conv_b"], p["w_ax"], p["b_a"], p["b_x"], p["lam"])


def _rg_layer_prompt(x, g, p, bsz, seq, tl):
    nt = seq // tl
    c = BRANCH
    d = x.shape[1]
    taps = CONV_W - 1
    x_spec = pl.BlockSpec((tl, d), lambda b, i: (b * nt + i, 0))
    return pl.pallas_call(
        _rg_layer_prompt_kernel,
        grid=(bsz, nt),
        in_specs=[x_spec, _const_spec(g.shape), _const_spec(p["w_in"].shape)] + _rg_weight_specs(p)
                 + [_const_spec(p["w_out"].shape)],
        out_specs=[x_spec, pl.BlockSpec((None, taps, SUBLANES, c), lambda b, i: (b, 0, 0, 0)),
                   pl.BlockSpec((None, SUBLANES, c), lambda b, i: (b, 0, 0))],
        out_shape=[jax.ShapeDtypeStruct(x.shape, F32), jax.ShapeDtypeStruct((bsz, taps, SUBLANES, c), F32),
                   jax.ShapeDtypeStruct((bsz, SUBLANES, c), F32)],
        scratch_shapes=[pltpu.VMEM((taps, SUBLANES, c), F32), pltpu.VMEM((tl, c), BF16),
                        pltpu.VMEM((SUBLANES, c), F32)],
        compiler_params=_params(2), name="rg_layer_prompt",
    )(x, g, p["w_in"], *_rg_weights(p), p["w_out"])


def _rg_core_sample(proj, prev8, h0pad, p, tm):
    t = proj.shape[0]
    c = BRANCH
    row = lambda w: pl.BlockSpec((tm, w), lambda i: (i, 0))
    return pl.pallas_call(
        _rg_sample_kernel,
        grid=(t // tm,),
        in_specs=[row(2 * c), row(c), row(c)] + _rg_weight_specs(p),
        out_specs=[row(c), row(c)],
        out_shape=[jax.ShapeDtypeStruct((t, c), BF16), jax.ShapeDtypeStruct((t, c), F32)],
        compiler_params=_params(1), name="rg_core_sample",
    )(proj, prev8, h0pad, *_rg_weights(p))


def _ssd_chunk(z, u3, prev3, dt_raw, s_read, s_write, cw_ref, cb_ref, dtb_ref, alog_ref, dexp_ref,
               ng_ref, acs_carry):
    q = z.shape[0]
    n = SSD_STATE
    xbc = jax.nn.silu(_conv8(u3, prev3, cw_ref[...], cb_ref[...]).reshape(q, SSD_CONV_DIM))
    xs = xbc[:, :BRANCH]
    bm = xbc[:, BRANCH:BRANCH + SSD_GROUPS * n]
    cm = xbc[:, BRANCH + SSD_GROUPS * n:]
    dt = jax.nn.softplus(dt_raw + dtb_ref[...])
    a = -jnp.exp(alog_ref[...])
    da3 = _cumsum8((dt * a).reshape(q // SUBLANES, SUBLANES, LANES))
    rows, carry = [], acs_carry
    for j in range(q // SUBLANES):
        blk = da3[j] + carry
        carry = jnp.broadcast_to(blk[SUBLANES - 1:, :], (SUBLANES, LANES))
        rows.append(blk)
    acs = jnp.concatenate(rows, axis=0) if len(rows) > 1 else rows[0]
    last = acs[q - 1:q, :]
    if q % LANES == 0:
        acs_t, dt_t = acs.T, dt.T
    else:
        pad = jnp.zeros((LANES - q, LANES), F32)
        acs_t = jnp.concatenate([acs, pad], axis=0).T[:, :q]
        dt_t = jnp.concatenate([dt, pad], axis=0).T[:, :q]
    causal = (lax.broadcasted_iota(jnp.int32, (q, q), 0) >= lax.broadcasted_iota(jnp.int32, (q, q), 1))
    lane = lax.broadcasted_iota(jnp.int32, (q, LANES), 1)
    srow = lax.broadcasted_iota(jnp.int32, (LANES, LANES), 0)
    half = SSD_HEAD_DIM
    y_pairs = []
    for g in range(SSD_GROUPS):
        bm_g = bm[:, g * n:(g + 1) * n]
        cm_g = cm[:, g * n:(g + 1) * n]
        cb_g = _dot_nt(cm_g.astype(BF16), bm_g.astype(BF16))
        for jp in range(2):
            pair = 2 * g + jp
            m_parts, cce_parts, bcw_parts, cds = [], [], [], []
            for h in (2 * pair, 2 * pair + 1):
                colb = jnp.broadcast_to(acs[:, h:h + 1], (q, LANES))
                dcol = jnp.broadcast_to(dt[:, h:h + 1], (q, LANES))
                seg = colb[:, :q] - acs_t[h:h + 1, :]
                decay = jnp.exp(jnp.where(causal, seg, -jnp.inf))
                m_parts.append(cb_g * decay * dt_t[h:h + 1, :])
                cce_parts.append(cm_g * jnp.exp(colb))
                lastb = last[:, h:h + 1]
                bcw_parts.append(bm_g * (dcol * jnp.exp(lastb - colb)))
                cds.append(jnp.exp(lastb))
            xs_pair = xs[:, pair * LANES:(pair + 1) * LANES]
            top = jnp.where(lane < half, xs_pair, 0.0)
            bot = jnp.where(lane >= half, xs_pair, 0.0)
            w = jnp.concatenate([top, bot], axis=0).astype(BF16)
            if q % LANES == 0:
                yd = _dot(jnp.concatenate(m_parts, axis=1).astype(BF16), w)
            else:
                yd = _dot(m_parts[0], top) + _dot(m_parts[1], bot)
            s_pair = s_read(pair)
            s_blk = jnp.concatenate([jnp.where(srow < half, s_pair, 0.0),
                                     jnp.where(srow >= half, s_pair, 0.0)], axis=1).astype(BF16)
            yo = _dot_nt(jnp.concatenate(cce_parts, axis=1).astype(BF16), s_blk)
            ds = _dot_tn(w, jnp.concatenate(bcw_parts, axis=0).astype(BF16))
            cd = jnp.where(srow < half, jnp.broadcast_to(cds[0], (LANES, LANES)),
                           jnp.broadcast_to(cds[1], (LANES, LANES)))
            s_write(pair, s_pair * cd + ds)
            y_pairs.append(yd + yo + dexp_ref[:, pair * LANES:(pair + 1) * LANES] * xs_pair)
    gw = BRANCH // SSD_GROUPS
    y_groups = []
    for g in range(SSD_GROUPS):
        yg = jnp.concatenate(y_pairs[2 * g:2 * g + 2], axis=1) * jax.nn.silu(z[:, g * gw:(g + 1) * gw])
        y_groups.append(yg * lax.rsqrt(jnp.mean(yg * yg, axis=-1, keepdims=True) + EPS))
    return jnp.concatenate(y_groups, axis=1) * ng_ref[...], carry


def _ssd_prompt_kernel(zx_ref, dt_ref, cw_ref, cb_ref, dtb_ref, alog_ref, dexp_ref, ng_ref,
                       y_ref, sout_ref, xbuf, s_scr):
    q = zx_ref.shape[0]
    g = q // SUBLANES
    c = SSD_CONV_DIM

    @pl.when(pl.program_id(1) == 0)
    def _():
        xbuf[0:SUBLANES, :] = jnp.zeros((SUBLANES, c), F32)
        s_scr[...] = jnp.zeros_like(s_scr)

    xbuf[SUBLANES:, :] = zx_ref[:, BRANCH:]
    u3 = xbuf[SUBLANES:, :].reshape(g, SUBLANES, c)
    prev3 = xbuf[0:q, :].reshape(g, SUBLANES, c)

    def s_read(pair):
        return s_scr[pair * LANES:(pair + 1) * LANES, :]

    def s_write(pair, val):
        s_scr[pair * LANES:(pair + 1) * LANES, :] = val

    y, _ = _ssd_chunk(zx_ref[:, :BRANCH], u3, prev3, dt_ref[...], s_read, s_write, cw_ref, cb_ref,
                      dtb_ref, alog_ref, dexp_ref, ng_ref, jnp.zeros((SUBLANES, LANES), F32))
    xbuf[0:SUBLANES, :] = xbuf[q:, :]
    y_ref[...] = y.astype(BF16)

    @pl.when(pl.program_id(1) == pl.num_programs(1) - 1)
    def _():
        sout_ref[...] = s_scr[...]


def _ssd_sample_kernel(zx_ref, dt_ref, prev_ref, s0_ref, cw_ref, cb_ref, dtb_ref, alog_ref, dexp_ref,
                       ng_ref, y_ref, sout_ref):
    q = zx_ref.shape[0]
    c = SSD_CONV_DIM
    u3 = zx_ref[:, BRANCH:].reshape(1, q, c)
    prev3 = prev_ref[...].reshape(1, q, c)

    def s_read(pair):
        return s0_ref[pair * LANES:(pair + 1) * LANES, :]

    def s_write(pair, val):
        sout_ref[pair * LANES:(pair + 1) * LANES, :] = val

    y, _ = _ssd_chunk(zx_ref[:, :BRANCH], u3, prev3, dt_ref[...], s_read, s_write, cw_ref, cb_ref,
                      dtb_ref, alog_ref, dexp_ref, ng_ref, jnp.zeros((SUBLANES, LANES), F32))
    y_ref[...] = y.astype(BF16)


def _ssd_weight_specs(p):
    return [_const_spec(p[k].shape) for k in ("conv_w", "conv_b", "dt_bias", "a_log", "d_exp", "norm_g")]


def _ssd_weights(p):
    return tuple(p[k] for k in ("conv_w", "conv_b", "dt_bias", "a_log", "d_exp", "norm_g"))


def _ssd_core_prompt(zx, dt, p, bsz, seq):
    q = SSD_CHUNK
    nt = seq // q
    hp = SSD_HEADS * SSD_HEAD_DIM
    return pl.pallas_call(
        _ssd_prompt_kernel,
        grid=(bsz, nt),
        in_specs=[pl.BlockSpec((q, zx.shape[1]), lambda b, i: (b * nt + i, 0)),
                  pl.BlockSpec((q, LANES), lambda b, i: (b * nt + i, 0))] + _ssd_weight_specs(p),
        out_specs=[pl.BlockSpec((q, BRANCH), lambda b, i: (b * nt + i, 0)),
                   pl.BlockSpec((None, hp, SSD_STATE), lambda b, i: (b, 0, 0))],
        out_shape=[jax.ShapeDtypeStruct((bsz * seq, BRANCH), BF16),
                   jax.ShapeDtypeStruct((bsz, hp, SSD_STATE), F32)],
        scratch_shapes=[pltpu.VMEM((SUBLANES + q, SSD_CONV_DIM), F32), pltpu.VMEM((hp, SSD_STATE), F32)],
        compiler_params=_params(2), name="ssd_core_prompt",
    )(zx, dt, *_ssd_weights(p))


def _ssd_core_sample(zx, dt, prev8, s0, idx, p, seq):
    t = zx.shape[0]
    bsz = t // seq
    hp = SSD_HEADS * SSD_HEAD_DIM
    row = lambda w: pl.BlockSpec((seq, w), lambda i: (i, 0))
    st = pl.BlockSpec((None, hp, SSD_STATE), lambda i: (i, 0, 0))
    st_in = pl.BlockSpec((None, None, hp, SSD_STATE), lambda i: (idx, i, 0, 0))
    return pl.pallas_call(
        _ssd_sample_kernel,
        grid=(bsz,),
        in_specs=[row(zx.shape[1]), row(LANES), row(SSD_CONV_DIM), st_in] + _ssd_weight_specs(p),
        out_specs=[row(BRANCH), st],
        out_shape=[jax.ShapeDtypeStruct((t, BRANCH), BF16), jax.ShapeDtypeStruct((bsz, hp, SSD_STATE), F32)],
        compiler_params=_params(1), name="ssd_core_sample",
    )(zx, dt, prev8, s0, *_ssd_weights(p))


def _hg_lower_bound(rows, layer):
    mx = functools.reduce(jnp.maximum, rows)
    es = [jnp.exp(x - mx) for x in rows]
    return sum(es[1:layer + 1]) / sum(es)


def _hg_prompt_kernel(proj_ref, hlb_ref, ng_ref, y_ref, sout_ref, st_scr, *, layer):
    r = proj_ref.shape[0]
    c = BRANCH
    dk = HGRN_KEY_DIM
    blk, sub = HGRN_BLOCK, HGRN_CHUNK
    nblk, nsub = r // blk, blk // sub

    @pl.when(pl.program_id(1) == 0)
    def _():
        st_scr[...] = jnp.zeros_like(st_scr)

    causal = (lax.broadcasted_iota(jnp.int32, (blk, blk), 0)
              >= lax.broadcasted_iota(jnp.int32, (blk, blk), 1))

    def head_body(h, carry):
        def lanes(part):
            return pl.ds(pl.multiple_of(part * c + h * dk, dk), dk)

        lb = _hg_lower_bound([hlb_ref[j:j + 1, lanes(0)] for j in range(DEPTH)], layer)
        f = proj_ref[:, lanes(1)]
        forget = lb + (1.0 - lb) * jax.nn.sigmoid(f)
        k = ((1.0 - lb) * jax.nn.sigmoid(-f)).reshape(nblk, blk, dk)
        x = _cumsum8(jnp.log(forget).reshape(r // SUBLANES, SUBLANES, dk)).reshape(nblk, blk, dk)
        parts, carry_row = [], None
        for j in range(blk // SUBLANES):
            part = x[:, j * SUBLANES:(j + 1) * SUBLANES, :]
            if carry_row is not None:
                part = part + carry_row
            carry_row = jnp.broadcast_to(part[:, SUBLANES - 1:, :], part.shape)
            parts.append(part)
        gc = jnp.concatenate(parts, axis=1)
        ends = [gc[:, (i + 1) * sub - 1:(i + 1) * sub, :] for i in range(nsub)]
        starts = [jnp.zeros_like(ends[0])] + ends[:-1]
        spread = lambda rows_: jnp.concatenate(
            [jnp.broadcast_to(x_, (nblk, sub, dk)) for x_ in rows_], axis=1)
        b_prev, b_next = spread(starts), spread(ends)
        qi = jax.nn.silu(proj_ref[:, lanes(0)]).reshape(nblk, blk, dk) * jnp.exp(gc - b_prev)
        qc = qi * jnp.exp(b_prev)
        kd = k * jnp.exp(b_prev - gc)
        ke = kd * jnp.exp(b_next - b_prev)
        kend = ke * jnp.exp(ends[-1] - b_next)
        v = proj_ref[:, lanes(2)].reshape(nblk, blk, dk)

        st = st_scr[h]
        outs = []
        for b in range(nblk):
            qi_b = qi[b].astype(BF16)
            att_rows = []
            for i in range(nsub):
                keys = []
                for j in range(nsub):
                    rs = slice(j * sub, (j + 1) * sub)
                    if j == i:
                        keys.append(kd[b, rs])
                    elif j < i - 1:
                        keys.append(ke[b, rs] * jnp.exp(starts[i][b] - ends[j][b]))
                    else:
                        keys.append(ke[b, rs])
                keys = jnp.concatenate(keys, axis=0).astype(BF16)
                att_rows.append(_dot_nt(qi_b[i * sub:(i + 1) * sub], keys))
            att = jnp.where(causal, jnp.concatenate(att_rows, axis=0), 0.0).astype(BF16)
            vb = v[b].astype(BF16)
            outs.append(_dot(att, vb) + _dot_nt(qc[b].astype(BF16), st.astype(BF16)))
            st = st * jnp.exp(ends[-1][b]) + _dot_tn(vb, kend[b].astype(BF16))
        st_scr[h] = st
        o = jnp.concatenate(outs, axis=0)
        o = o * lax.rsqrt(jnp.mean(o * o, axis=-1, keepdims=True) + EPS)
        gate = jax.nn.silu(proj_ref[:, lanes(3)])
        y_ref[:, lanes(0)] = (o * ng_ref[:, lanes(0)] * gate).astype(BF16)
        return carry

    lax.fori_loop(0, HGRN_HEADS, head_body, 0, unroll=4)

    @pl.when(pl.program_id(1) == pl.num_programs(1) - 1)
    def _():
        for h in range(HGRN_HEADS):
            sout_ref[h] = st_scr[h].T


def _hg_seq_kernel(proj_ref, hlb_ref, ng_ref, s0_ref, y_ref, sout_ref,
                   qg_scr, kg_scr, ke_scr, v_scr, dec_scr, o_scr, *, layer, chunk):
    r = proj_ref.shape[0]
    c = BRANCH
    g = r // SUBLANES
    n_chunks = r // chunk
    dk, dv = HGRN_KEY_DIM, HGRN_VAL_DIM
    assert chunk == SUBLANES

    lb = _hg_lower_bound([hlb_ref[j:j + 1, :] for j in range(DEPTH)], layer)
    f = proj_ref[:, c:2 * c]
    forget = lb + (1.0 - lb) * jax.nn.sigmoid(f)
    k = (1.0 - lb) * jax.nn.sigmoid(-f)
    gcum3 = _cumsum8(jnp.log(forget).reshape(g, SUBLANES, c))
    gcum = gcum3.reshape(r, c)
    glast = jnp.broadcast_to(gcum3[:, SUBLANES - 1:, :], gcum3.shape).reshape(r, c)
    qg_scr[...] = jax.nn.silu(proj_ref[:, :c]) * jnp.exp(gcum)
    kg_scr[...] = k * jnp.exp(-gcum)
    ke_scr[...] = k * jnp.exp(glast - gcum)
    dec_scr[...] = jnp.exp(glast)
    v_scr[...] = proj_ref[:, 2 * c:3 * c]

    causal = (lax.broadcasted_iota(jnp.int32, (chunk, chunk), 0)
              >= lax.broadcasted_iota(jnp.int32, (chunk, chunk), 1))

    def chunk_body(ci, carry):
        rs = pl.ds(pl.multiple_of(ci * chunk, chunk), chunk)
        outs = []
        for h in range(HGRN_HEADS):
            ks = slice(h * dk, (h + 1) * dk)
            vs = slice(h * dv, (h + 1) * dv)
            qg = qg_scr[rs, ks].astype(BF16)
            vv = v_scr[rs, vs].astype(BF16)
            st = s0_ref[ci, h].T
            att = jnp.where(causal, _dot_nt(qg, kg_scr[rs, ks].astype(BF16)), 0.0)
            outs.append(_dot(att.astype(BF16), vv) + _dot_nt(qg, st.astype(BF16)))
            dec = dec_scr[rs, ks][chunk - 1:, :]
            sout_ref[ci, h] = (st * dec + _dot_tn(vv, ke_scr[rs, ks].astype(BF16))).T
        o_scr[rs, :] = jnp.concatenate(outs, axis=1)
        return carry

    lax.fori_loop(0, n_chunks, chunk_body, 0)

    gate = jax.nn.silu(proj_ref[:, 3 * c:])
    parts = []
    for h in range(HGRN_HEADS):
        o = o_scr[:, h * dv:(h + 1) * dv]
        parts.append(o * lax.rsqrt(jnp.mean(o * o, axis=-1, keepdims=True) + EPS))
    y_ref[...] = (jnp.concatenate(parts, axis=1) * ng_ref[...] * gate).astype(BF16)


def _hg_core_prompt(proj, hlb, ng, layer, bsz, seq, tl):
    nt = seq // tl
    c = BRANCH
    st_shape = (HGRN_HEADS, HGRN_KEY_DIM, HGRN_VAL_DIM)
    return pl.pallas_call(
        functools.partial(_hg_prompt_kernel, layer=layer),
        grid=(bsz, nt),
        in_specs=[pl.BlockSpec((tl, 4 * c), lambda b, i: (b * nt + i, 0)),
                  _const_spec(hlb.shape), _const_spec(ng.shape)],
        out_specs=[pl.BlockSpec((tl, c), lambda b, i: (b * nt + i, 0)),
                   pl.BlockSpec((None,) + st_shape, lambda b, i: (b, 0, 0, 0))],
        out_shape=[jax.ShapeDtypeStruct((bsz * seq, c), BF16),
                   jax.ShapeDtypeStruct((bsz,) + st_shape, F32)],
        scratch_shapes=[pltpu.VMEM((HGRN_HEADS, HGRN_VAL_DIM, HGRN_KEY_DIM), F32)],
        compiler_params=_params(2), name="hg_core_prompt",
    )(proj, hlb, ng)


def _hg_core_sample(proj, s0, idx, hlb, ng, layer, seq, nb):
    t = proj.shape[0]
    bsz = t // seq
    c = BRANCH
    rows = nb * seq
    st_shape = (nb, HGRN_HEADS, HGRN_KEY_DIM, HGRN_VAL_DIM)
    st_spec = pl.BlockSpec(st_shape, lambda i: (i, 0, 0, 0))
    st_in = pl.BlockSpec((None,) + st_shape, lambda i: (idx, i, 0, 0, 0))
    return pl.pallas_call(
        functools.partial(_hg_seq_kernel, layer=layer, chunk=seq),
        grid=(bsz // nb,),
        in_specs=[pl.BlockSpec((rows, 4 * c), lambda i: (i, 0)),
                  _const_spec(hlb.shape), _const_spec(ng.shape), st_in],
        out_specs=[pl.BlockSpec((rows, c), lambda i: (i, 0)), st_spec],
        out_shape=[jax.ShapeDtypeStruct((t, c), BF16), jax.ShapeDtypeStruct(s0.shape[1:], F32)],
        scratch_shapes=[pltpu.VMEM((rows, c), F32)] * 6,
        compiler_params=_params(1), name="hg_core_sample",
    )(proj, hlb, ng, s0)


def _tile(n, target):
    t = min(n, target)
    assert n % t == 0, (n, target)
    return t


def _pad_groups(state, first_row):
    n, k, c = state.shape
    return jnp.pad(state, ((0, 0), (first_row, SUBLANES - first_row - k), (0, 0))).reshape(n * SUBLANES, c)


def _trunk(x, mem_k, mem_v, states, w, bsz, seq, prompt):
    tm = _tile(x.shape[0], 256)
    rg_conv, rg_h, ssd_conv, ssd_s, hg_s = [], [], [], [], []
    tail = slice(seq - (CONV_W - 1), seq)
    for layer in range(DEPTH):
        kind, idx = layer % N_MIXERS, layer // N_MIXERS
        g = w["norm_g"][layer]
        if kind == 0:
            p = w["rg"][idx]
            if prompt:
                x, utail, htail = _rg_layer_prompt(x, g, p, bsz, seq, _tile(seq, 256))
                rg_h.append(htail[:, SUBLANES - 1])
                rg_conv.append(utail[:, :, 0])
                y = None
            else:
                (proj,) = _norm_matmul(x, g[0:1], [p["w_in"]], tm)
                prev8 = _pad_groups(states["rg_conv"][idx], SUBLANES - (CONV_W - 1))
                h0pad = _pad_groups(states["rg_h"][idx][:, None, :], 0)
                y, h = _rg_core_sample(proj, prev8, h0pad, p, tm)
                rg_h.append(h.reshape(bsz, seq, BRANCH)[:, seq - 1])
                rg_conv.append(proj.reshape(bsz, seq, 2 * BRANCH)[:, tail, :BRANCH])
        elif kind == 1:
            p = w["ssd"][idx]
            zx, dt = _norm_matmul(x, g[0:1], [p["w_zx"], p["w_dt"]], tm)
            if prompt:
                y, s_new = _ssd_core_prompt(zx, dt, p, bsz, seq)
            else:
                prev8 = _pad_groups(states["ssd_conv"][idx], SUBLANES - (CONV_W - 1))
                s0 = states["ssd_s"].reshape(-1, bsz, SSD_HEADS * SSD_HEAD_DIM, SSD_STATE)
                y, s_new = _ssd_core_sample(zx, dt, prev8, s0, idx, p, seq)
            ssd_s.append(s_new.reshape(bsz, SSD_HEADS, SSD_HEAD_DIM, SSD_STATE))
            ssd_conv.append(zx.reshape(bsz, seq, BRANCH + SSD_CONV_DIM)[:, tail, BRANCH:])
        else:
            p = w["hg"][idx]
            (proj,) = _norm_matmul(x, g[0:1], [p["w_in"]], tm)
            if prompt:
                y, s_new = _hg_core_prompt(proj, w["hg_lower_bounds"], p["norm_g"], layer, bsz, seq,
                                           _tile(seq, 256))
            else:
                y, s_new = _hg_core_sample(proj, states["hg_s"], idx, w["hg_lower_bounds"], p["norm_g"],
                                           layer, seq, 8)
            hg_s.append(s_new)
        if y is not None:
            x = _proj_norm_res(y, p["w_out"], g[1:2], x, tm)
        if prompt:
            x = _attn_prompt(x, mem_k, mem_v, layer, w["x_w_q"][layer], w["x_w_o"][layer], g,
                             bsz, seq, _tile(seq, 1024))
        else:
            x = _attn_sample(x, mem_k, mem_v, layer, w["x_w_q"][layer], w["x_w_o"][layer], g,
                             seq, 4)
    return x, jnp.stack(rg_conv), jnp.stack(rg_h), jnp.stack(ssd_conv), jnp.stack(ssd_s), jnp.stack(hg_s)


def kernel(x_prompt, x_sample, mem_prompt, state_rglru_conv, state_rglru_h, state_ssd_conv, state_ssd,
           state_hgrn, cache_mem_k, cache_mem_v, norm_g, mem_norm_g, rg_w_in, rg_conv_w, rg_conv_b, rg_w_a,
           rg_b_a, rg_w_x, rg_b_x, rg_lambda, rg_w_out, ssd_w_in, ssd_conv_w, ssd_conv_b, ssd_dt_bias,
           ssd_a_log, ssd_d, ssd_norm_g, ssd_w_out, hg_w_in, hg_lower_bounds, hg_norm_g, hg_w_out,
           x_w_q, x_w_k, x_w_v, x_w_o):
    bp, sp, d = x_prompt.shape
    bs, ss, _ = x_sample.shape
    n_a, n_b, n_c = rg_w_in.shape[0], ssd_w_in.shape[0], hg_w_in.shape[0]
    pad_heads = lambda v: jnp.pad(v, (0, LANES - SSD_HEADS))[None, :]
    w = {
        "norm_g": norm_g,
        "hg_lower_bounds": hg_lower_bounds,
        "x_w_q": x_w_q.astype(BF16),
        "x_w_o": x_w_o.astype(BF16),
        "rg": [{
            "w_in": rg_w_in[i].astype(BF16),
            "conv_w": rg_conv_w[i], "conv_b": rg_conv_b[i][None, :],
            "w_ax": jnp.concatenate([rg_w_a[i], rg_w_x[i]], axis=-1).astype(BF16),
            "b_a": rg_b_a[i][None, :], "b_x": rg_b_x[i][None, :], "lam": rg_lambda[i][None, :],
            "w_out": rg_w_out[i].astype(BF16),
        } for i in range(n_a)],
        "ssd": [{
            "w_zx": ssd_w_in[i][:, :BRANCH + SSD_CONV_DIM].astype(BF16),
            "w_dt": jnp.pad(ssd_w_in[i][:, BRANCH + SSD_CONV_DIM:], ((0, 0), (0, LANES - SSD_HEADS))).astype(BF16),
            "conv_w": ssd_conv_w[i], "conv_b": ssd_conv_b[i][None, :],
            "dt_bias": pad_heads(ssd_dt_bias[i]), "a_log": pad_heads(ssd_a_log[i]),
            "d_exp": jnp.repeat(ssd_d[i], SSD_HEAD_DIM)[None, :],
            "norm_g": ssd_norm_g[i][None, :],
            "w_out": ssd_w_out[i].astype(BF16),
        } for i in range(n_b)],
        "hg": [{
            "w_in": hg_w_in[i].astype(BF16),
            "norm_g": hg_norm_g[i][None, :],
            "w_out": hg_w_out[i].astype(BF16),
        } for i in range(n_c)],
    }

    mem_k_p, mem_v_p = _mem_kv(mem_prompt.reshape(bp * N_MEM, d), mem_norm_g[:, None, :],
                               x_w_k.astype(BF16), x_w_v.astype(BF16), _tile(bp, 2))
    y_p, rgc_p, rgh_p, sc_p, ss_p, hs_p = _trunk(
        x_prompt.reshape(bp * sp, d), mem_k_p, mem_v_p, None, w, bp, sp, True)
    states = {"rg_conv": state_rglru_conv, "rg_h": state_rglru_h, "ssd_conv": state_ssd_conv,
              "ssd_s": state_ssd, "hg_s": state_hgrn}
    y_s, rgc_s, rgh_s, sc_s, ss_s, hs_s = _trunk(
        x_sample.reshape(bs * ss, d), _kv_flat(cache_mem_k), _kv_flat(cache_mem_v), states, w, bs, ss, False)
    return (y_p.reshape(bp, sp, d), y_s.reshape(bs, ss, d), rgc_p, rgh_p, sc_p, ss_p, hs_p,
            _kv_unflat(mem_k_p), _kv_unflat(mem_v_p), rgc_s, rgh_s, sc_s, ss_s, hs_s)
```

```python
import functools

import jax
import jax.numpy as jnp
from jax import lax
from jax.experimental import pallas as pl
from jax.experimental.pallas import tpu as pltpu

F32 = jnp.float32
BF16 = jnp.bfloat16

D_MODEL = 1024
DEPTH = 4
N_MIXERS = 3
BRANCH = 2 * D_MODEL
CONV_W = 4
EPS = 1e-6
LRU_BLOCKS = 8
LRU_BLOCK = BRANCH // LRU_BLOCKS
LRU_C = 8.0
SSD_HEAD_DIM = 64
SSD_HEADS = BRANCH // SSD_HEAD_DIM
SSD_STATE = 128
SSD_GROUPS = 8
SSD_CONV_DIM = BRANCH + 2 * SSD_GROUPS * SSD_STATE
SSD_CHUNK = 128
HGRN_KEY_DIM = 128
HGRN_HEADS = BRANCH // HGRN_KEY_DIM
HGRN_VAL_DIM = BRANCH // HGRN_HEADS
HGRN_CHUNK = 16
HGRN_BLOCK = 4 * HGRN_CHUNK
N_MEM = 256
X_HEADS = 4
X_HEAD_DIM = D_MODEL // X_HEADS

SUBLANES = 8
LANES = 128
VMEM_BYTES_V7X = 64 * 1024 * 1024
VMEM_LIMIT = VMEM_BYTES_V7X * 7 // 8

NT_DIMS = (((1,), (1,)), ((), ()))
TN_DIMS = (((0,), (0,)), ((), ()))


def _params(n_grid_dims):
    return pltpu.CompilerParams(
        dimension_semantics=("arbitrary",) * n_grid_dims, vmem_limit_bytes=VMEM_LIMIT)


def _const_spec(shape):
    nd = len(shape)
    return pl.BlockSpec(shape, lambda *_: (0,) * nd, pipeline_mode=pl.Buffered(1))


def _rms(x, g):
    return x * lax.rsqrt(jnp.mean(x * x, axis=-1, keepdims=True) + EPS) * g


def _dot(a, b):
    return jnp.dot(a, b, preferred_element_type=F32)


def _dot_nt(a, b):
    return lax.dot_general(a, b, NT_DIMS, preferred_element_type=F32)


def _dot_tn(a, b):
    return lax.dot_general(a, b, TN_DIMS, preferred_element_type=F32)


def _group_iota(width):
    return lax.broadcasted_iota(jnp.int32, (1, SUBLANES, width), 1)


def _conv8(u3, prev3, cw, cb):
    t = _group_iota(u3.shape[-1])
    acc = cb + cw[CONV_W - 1:CONV_W, :] * u3
    for k in range(1, CONV_W):
        shifted = pltpu.roll(jnp.where(t >= SUBLANES - k, prev3, u3), k, 1)
        acc = acc + cw[CONV_W - 1 - k:CONV_W - k, :] * shifted
    return acc


def _scan8(a3, b3):
    t = _group_iota(a3.shape[-1])
    for s in (1, 2, 4):
        m = t >= s
        a_sh = pltpu.roll(a3, s, 1)
        b_sh = pltpu.roll(b3, s, 1)
        b3 = jnp.where(m, a3 * b_sh + b3, b3)
        a3 = jnp.where(m, a3 * a_sh, a3)
    return a3, b3


def _cumsum8(x3):
    t = _group_iota(x3.shape[-1])
    for s in (1, 2, 4):
        x3 = x3 + jnp.where(t >= s, pltpu.roll(x3, s, 1), 0.0)
    return x3


def _norm_matmul_kernel(x_ref, g_ref, *refs, n_chunk):
    n_w = len(refs) // 2
    h = _rms(x_ref[...], g_ref[...]).astype(BF16)
    for w_ref, o_ref in zip(refs[:n_w], refs[n_w:]):
        n = o_ref.shape[-1]
        step = min(n_chunk, n)
        for c in range(0, n, step):
            o_ref[:, c:c + step] = _dot(h, w_ref[:, c:c + step])


def _norm_matmul(x, g, ws, tm):
    t, d = x.shape
    grid = (t // tm,)
    in_specs = [pl.BlockSpec((tm, d), lambda i: (i, 0)), _const_spec((1, d))]
    in_specs += [_const_spec(w.shape) for w in ws]
    out_specs = [pl.BlockSpec((tm, w.shape[1]), lambda i: (i, 0)) for w in ws]
    out_shape = [jax.ShapeDtypeStruct((t, w.shape[1]), F32) for w in ws]
    return pl.pallas_call(
        functools.partial(_norm_matmul_kernel, n_chunk=512),
        grid=grid, in_specs=in_specs, out_specs=out_specs, out_shape=out_shape,
        compiler_params=_params(1), name="norm_matmul",
    )(x, g, *ws)


def _proj_norm_res_kernel(a_ref, w_ref, g_ref, x_ref, o_ref):
    y = _dot(a_ref[...], w_ref[...])
    o_ref[...] = x_ref[...] + _rms(y, g_ref[...])


def _proj_norm_res(a, w, g, x, tm):
    t, k = a.shape
    d = x.shape[1]
    return pl.pallas_call(
        _proj_norm_res_kernel,
        grid=(t // tm,),
        in_specs=[pl.BlockSpec((tm, k), lambda i: (i, 0)), _const_spec(w.shape), _const_spec((1, d)),
                  pl.BlockSpec((tm, d), lambda i: (i, 0))],
        out_specs=pl.BlockSpec((tm, d), lambda i: (i, 0)),
        out_shape=jax.ShapeDtypeStruct((t, d), F32),
        compiler_params=_params(1), name="proj_norm_res",
    )(a, w, g, x)


KV_LANE_TILES = X_HEAD_DIM // LANES
KV_ROWS = KV_LANE_TILES * X_HEADS


def _kv_flat(kv):
    lead = kv.shape[:-3]
    x = kv.reshape(lead + (N_MEM, X_HEADS, KV_LANE_TILES, LANES))
    return jnp.swapaxes(x, -3, -2).reshape(lead + (N_MEM * KV_ROWS, LANES))


def _kv_unflat(flat):
    lead = flat.shape[:-2]
    x = flat.reshape(lead + (N_MEM, KV_LANE_TILES, X_HEADS, LANES))
    return jnp.swapaxes(x, -3, -2).reshape(lead + (N_MEM, X_HEADS, X_HEAD_DIM))


def _mem_kv_kernel(m_ref, g_ref, wk_ref, wv_ref, k_ref, v_ref):
    h = _rms(m_ref[...], g_ref[...]).astype(BF16)
    tm = m_ref.shape[0]
    for w_ref, o_ref in ((wk_ref, k_ref), (wv_ref, v_ref)):
        y = _dot(h, w_ref[...])
        pieces = [y[:, hd * X_HEAD_DIM + t * LANES:hd * X_HEAD_DIM + (t + 1) * LANES][None]
                  for t in range(KV_LANE_TILES) for hd in range(X_HEADS)]
        rows = jnp.swapaxes(jnp.concatenate(pieces, axis=0), 0, 1)
        o_ref[...] = rows.reshape(o_ref.shape)


def _mem_kv(mem, g, wk, wv, nb):
    t, d = mem.shape
    bsz = t // N_MEM
    tm = nb * N_MEM
    w_spec = pl.BlockSpec((None, d, d), lambda l, i: (l, 0, 0))
    o_spec = pl.BlockSpec((None, nb, N_MEM * KV_ROWS, LANES), lambda l, i: (l, i, 0, 0))
    return pl.pallas_call(
        _mem_kv_kernel,
        grid=(DEPTH, bsz // nb),
        in_specs=[pl.BlockSpec((tm, d), lambda l, i: (i, 0)),
                  pl.BlockSpec((None, 1, d), lambda l, i: (l, 0, 0)), w_spec, w_spec],
        out_specs=[o_spec, o_spec],
        out_shape=[jax.ShapeDtypeStruct((DEPTH, bsz, N_MEM * KV_ROWS, LANES), F32)] * 2,
        compiler_params=_params(2), name="mem_kv",
    )(mem, g, wk, wv)


def _attn_kernel(x_ref, k_ref, v_ref, wq_ref, wo_ref, g_ref, o_ref, q_scr, a_scr, k_scr, v_scr, *,
                 rows, hoist, seq_rows):
    i = pl.program_id(0) if hoist else None
    scale = X_HEAD_DIM ** -0.5
    n_keys = k_scr.shape[1]

    def split_heads():
        for ref, scr in ((k_ref, k_scr), (v_ref, v_scr)):
            xs = jnp.swapaxes(ref[...].reshape(n_keys, KV_ROWS, LANES), 0, 1)
            for h in range(X_HEADS):
                scr[h] = jnp.concatenate([xs[t * X_HEADS + h] for t in range(KV_LANE_TILES)],
                                         axis=1).astype(BF16)

    if hoist:
        split_heads()
    else:
        pl.when(pl.program_id(1) == 0)(split_heads)

    def project_q():
        h = _rms(x_ref[...], g_ref[2:3, :]).astype(BF16)
        q_scr[...] = (_dot(h, wq_ref[...]) * scale).astype(BF16)

    def project_out():
        y = _dot(a_scr[...], wo_ref[...])
        o_ref[...] = x_ref[...] + _rms(y, g_ref[3:4, :])

    if hoist:
        pl.when(i == 0)(project_q)
        r0 = pl.multiple_of(i * rows, rows)
        rsl = pl.ds(r0, rows)
    else:
        project_q()
        rsl = slice(None)

    if seq_rows is not None:
        qi = lax.broadcasted_iota(jnp.int32, (rows, n_keys), 0) // seq_rows
        ki = lax.broadcasted_iota(jnp.int32, (rows, n_keys), 1) // N_MEM
        mask = qi == ki
    for h in range(X_HEADS):
        hs = slice(h * X_HEAD_DIM, (h + 1) * X_HEAD_DIM)
        qh = q_scr[rsl, hs]
        s = _dot_nt(qh, k_scr[h])
        if seq_rows is not None:
            s = jnp.where(mask, s, -jnp.inf)
        e = jnp.exp(s - jnp.max(s, axis=-1, keepdims=True))
        p = e / jnp.sum(e, axis=-1, keepdims=True)
        a_scr[rsl, hs] = _dot(p.astype(BF16), v_scr[h]).astype(BF16)

    if hoist:
        pl.when(i == pl.num_programs(0) - 1)(project_out)
    else:
        project_out()


def _attn_prompt(x, k, v, layer, wq, wo, g, bsz, seq, tl):
    d = x.shape[1]
    nt = seq // tl
    kv_spec = pl.BlockSpec((None, None, N_MEM * KV_ROWS, LANES), lambda b, i: (layer, b, 0, 0))
    kv_scr = pltpu.VMEM((X_HEADS, N_MEM, X_HEAD_DIM), BF16)
    return pl.pallas_call(
        functools.partial(_attn_kernel, rows=tl, hoist=False, seq_rows=None),
        grid=(bsz, nt),
        in_specs=[pl.BlockSpec((tl, d), lambda b, i: (b * nt + i, 0)), kv_spec, kv_spec,
                  _const_spec(wq.shape), _const_spec(wo.shape), _const_spec(g.shape)],
        out_specs=pl.BlockSpec((tl, d), lambda b, i: (b * nt + i, 0)),
        out_shape=jax.ShapeDtypeStruct(x.shape, F32),
        scratch_shapes=[pltpu.VMEM((tl, d), BF16), pltpu.VMEM((tl, d), BF16), kv_scr, kv_scr],
        compiler_params=_params(2), name="attn_prompt",
    )(x, k, v, wq, wo, g)


def _attn_sample(x, k, v, layer, wq, wo, g, seq, nb):
    t, d = x.shape
    bsz = t // seq
    kv_spec = pl.BlockSpec((None, nb, N_MEM * KV_ROWS, LANES), lambda i: (layer, i, 0, 0))
    kv_scr = pltpu.VMEM((X_HEADS, nb * N_MEM, X_HEAD_DIM), BF16)
    return pl.pallas_call(
        functools.partial(_attn_kernel, rows=nb * seq, hoist=True, seq_rows=seq),
        grid=(bsz // nb,),
        in_specs=[_const_spec(x.shape), kv_spec, kv_spec,
                  _const_spec(wq.shape), _const_spec(wo.shape), _const_spec(g.shape)],
        out_specs=pl.BlockSpec(x.shape, lambda i: (0, 0)),
        out_shape=jax.ShapeDtypeStruct(x.shape, F32),
        scratch_shapes=[pltpu.VMEM((t, d), BF16), pltpu.VMEM((t, d), BF16), kv_scr, kv_scr],
        compiler_params=_params(1), name="attn_sample",
    )(x, k, v, wq, wo, g)


def _rg_gates(conv, blk, wax_ref, ba_ref, bx_ref, lam_ref):
    sl = slice(blk * LRU_BLOCK, (blk + 1) * LRU_BLOCK)
    pre = _dot(conv.astype(BF16), wax_ref[blk])
    rg = jax.nn.sigmoid(pre[:, :LRU_BLOCK] + ba_ref[:, sl])
    ig = jax.nn.sigmoid(pre[:, LRU_BLOCK:] + bx_ref[:, sl])
    a = jnp.exp(rg * (-LRU_C * jax.nn.softplus(-lam_ref[:, sl])))
    b = jnp.exp(0.5 * jnp.log(1.0 - a * a)) * (ig * conv)
    return a, b


def _rg_block_ab(u3, prev3, blk, cw_ref, cb_ref, wax_ref, ba_ref, bx_ref, lam_ref):
    g = u3.shape[0]
    sl = slice(blk * LRU_BLOCK, (blk + 1) * LRU_BLOCK)
    conv = _conv8(u3, prev3, cw_ref[:, sl], cb_ref[:, sl]).reshape(g * SUBLANES, LRU_BLOCK)
    a, b = _rg_gates(conv, blk, wax_ref, ba_ref, bx_ref, lam_ref)
    return a.reshape(g, SUBLANES, LRU_BLOCK), b.reshape(g, SUBLANES, LRU_BLOCK)


def _rg_ab(u3, prev3, cw_ref, cb_ref, wax_ref, ba_ref, bx_ref, lam_ref):
    parts = [_rg_block_ab(u3[:, :, blk * LRU_BLOCK:(blk + 1) * LRU_BLOCK],
                          prev3[:, :, blk * LRU_BLOCK:(blk + 1) * LRU_BLOCK],
                          blk, cw_ref, cb_ref, wax_ref, ba_ref, bx_ref, lam_ref)
             for blk in range(LRU_BLOCKS)]
    return (jnp.concatenate([p[0] for p in parts], axis=2), jnp.concatenate([p[1] for p in parts], axis=2))


def _rg_layer_prompt_kernel(x_ref, g_ref, win_ref, cw_ref, cb_ref, wax_ref, ba_ref, bx_ref, lam_ref,
                            wout_ref, o_ref, utail_ref, htail_ref, tail_scr, y_scr, hc):
    r, d = x_ref.shape
    c = BRANCH
    n = r // SUBLANES
    taps = CONV_W - 1

    @pl.when(pl.program_id(1) == 0)
    def _():
        tail_scr[...] = jnp.zeros_like(tail_scr)
        hc[...] = jnp.zeros_like(hc)

    xs = jnp.swapaxes(x_ref[...].reshape(SUBLANES, n, d), 0, 1).reshape(r, d)
    xn = _rms(xs, g_ref[0:1, :]).astype(BF16)
    first_seg = lax.broadcasted_iota(jnp.int32, (SUBLANES, LRU_BLOCK), 0) == 0
    for blk in range(LRU_BLOCKS):
        sl = slice(blk * LRU_BLOCK, (blk + 1) * LRU_BLOCK)
        u3 = _dot(xn, win_ref[:, sl]).reshape(n, SUBLANES, LRU_BLOCK)
        gate = _dot(xn, win_ref[:, c + blk * LRU_BLOCK:c + (blk + 1) * LRU_BLOCK])
        hist = []
        for j in range(taps):
            prev_seg = pltpu.roll(u3[n - taps + j], 1, 0)
            hist.append(jnp.where(first_seg, tail_scr[j, :, sl], prev_seg)[None])
            tail_scr[j, :, sl] = prev_seg
            utail_ref[j, :, sl] = prev_seg
        uext = jnp.concatenate(hist + [u3], axis=0)
        conv = cb_ref[:, sl] + cw_ref[taps:CONV_W, sl] * u3
        for k in range(1, CONV_W):
            conv = conv + cw_ref[taps - k:CONV_W - k, sl] * uext[taps - k:taps - k + n]
        a, b = _rg_gates(conv.reshape(r, LRU_BLOCK), blk, wax_ref, ba_ref, bx_ref, lam_ref)
        a3 = a.reshape(n, SUBLANES, LRU_BLOCK)
        b3 = b.reshape(n, SUBLANES, LRU_BLOCK)
        hs, prods = [b3[0]], [a3[0]]
        for s in range(1, n):
            hs.append(a3[s] * hs[-1] + b3[s])
            prods.append(a3[s] * prods[-1])
        h_in = hc[0:1, sl]
        carry_rows = []
        for q in range(SUBLANES):
            carry_rows.append(h_in)
            h_in = hs[-1][q:q + 1] + prods[-1][q:q + 1] * h_in
        carry = jnp.concatenate(carry_rows, axis=0)
        h_last = jnp.broadcast_to(h_in, (SUBLANES, LRU_BLOCK))
        hc[:, sl] = h_last
        htail_ref[:, sl] = h_last
        h = jnp.concatenate([(hs[s] + prods[s] * carry)[None] for s in range(n)], axis=0)
        y_scr[:, sl] = (h.reshape(r, LRU_BLOCK) * jax.nn.silu(gate)).astype(BF16)
    out = xs + _rms(_dot(y_scr[...], wout_ref[...]), g_ref[1:2, :])
    o_ref[...] = jnp.swapaxes(out.reshape(n, SUBLANES, d), 0, 1).reshape(r, d)


def _rg_sample_kernel(proj_ref, prev_ref, h0_ref, cw_ref, cb_ref, wax_ref, ba_ref, bx_ref, lam_ref,
                      y_ref, h_ref):
    r = proj_ref.shape[0]
    c = BRANCH
    g = r // SUBLANES
    u3 = proj_ref[:, :c].reshape(g, SUBLANES, c)
    prev3 = prev_ref[...].reshape(g, SUBLANES, c)
    a3, b3 = _rg_ab(u3, prev3, cw_ref, cb_ref, wax_ref, ba_ref, bx_ref, lam_ref)
    b3 = b3 + a3 * h0_ref[...].reshape(g, SUBLANES, c)
    _, h3 = _scan8(a3, b3)
    h = h3.reshape(r, c)
    h_ref[...] = h
    y_ref[...] = (h * jax.nn.silu(proj_ref[:, c:])).astype(BF16)


def _rg_weight_specs(p):
    return [_const_spec(p["conv_w"].shape), _const_spec(p["conv_b"].shape), _const_spec(p["w_ax"].shape),
            _const_spec(p["b_a"].shape), _const_spec(p["b_x"].shape), _const_spec(p["lam"].shape)]


def _rg_weights(p):
    return (p["conv_w"], p["conv_b"], p["w_ax"], p["b_a"], p["b_x"], p["lam"])


def _rg_layer_prompt(x, g, p, bsz, seq, tl):
    nt = seq // tl
    c = BRANCH
    d = x.shape[1]
    taps = CONV_W - 1
    x_spec = pl.BlockSpec((tl, d), lambda b, i: (b * nt + i, 0))
    return pl.pallas_call(
        _rg_layer_prompt_kernel,
        grid=(bsz, nt),
        in_specs=[x_spec, _const_spec(g.shape), _const_spec(p["w_in"].shape)] + _rg_weight_specs(p)
                 + [_const_spec(p["w_out"].shape)],
        out_specs=[x_spec, pl.BlockSpec((None, taps, SUBLANES, c), lambda b, i: (b, 0, 0, 0)),
                   pl.BlockSpec((None, SUBLANES, c), lambda b, i: (b, 0, 0))],
        out_shape=[jax.ShapeDtypeStruct(x.shape, F32), jax.ShapeDtypeStruct((bsz, taps, SUBLANES, c), F32),
                   jax.ShapeDtypeStruct((bsz, SUBLANES, c), F32)],
        scratch_shapes=[pltpu.VMEM((taps, SUBLANES, c), F32), pltpu.VMEM((tl, c), BF16),
                        pltpu.VMEM((SUBLANES, c), F32)],
        compiler_params=_params(2), name="rg_layer_prompt",
    )(x, g, p["w_in"], *_rg_weights(p), p["w_out"])


def _rg_core_sample(proj, prev8, h0pad, p, tm):
    t = proj.shape[0]
    c = BRANCH
    row = lambda w: pl.BlockSpec((tm, w), lambda i: (i, 0))
    return pl.pallas_call(
        _rg_sample_kernel,
        grid=(t // tm,),
        in_specs=[row(2 * c), row(c), row(c)] + _rg_weight_specs(p),
        out_specs=[row(c), row(c)],
        out_shape=[jax.ShapeDtypeStruct((t, c), BF16), jax.ShapeDtypeStruct((t, c), F32)],
        compiler_params=_params(1), name="rg_core_sample",
    )(proj, prev8, h0pad, *_rg_weights(p))


def _ssd_chunk(z, u3, prev3, dt_raw, s_read, s_write, cw_ref, cb_ref, dtb_ref, alog_ref, dexp_ref,
               ng_ref, acs_carry):
    q = z.shape[0]
    n = SSD_STATE
    xbc = jax.nn.silu(_conv8(u3, prev3, cw_ref[...], cb_ref[...]).reshape(q, SSD_CONV_DIM))
    xs = xbc[:, :BRANCH]
    bm = xbc[:, BRANCH:BRANCH + SSD_GROUPS * n]
    cm = xbc[:, BRANCH + SSD_GROUPS * n:]
    dt = jax.nn.softplus(dt_raw + dtb_ref[...])
    a = -jnp.exp(alog_ref[...])
    da3 = _cumsum8((dt * a).reshape(q // SUBLANES, SUBLANES, LANES))
    rows, carry = [], acs_carry
    for j in range(q // SUBLANES):
        blk = da3[j] + carry
        carry = jnp.broadcast_to(blk[SUBLANES - 1:, :], (SUBLANES, LANES))
        rows.append(blk)
    acs = jnp.concatenate(rows, axis=0) if len(rows) > 1 else rows[0]
    last = acs[q - 1:q, :]
    if q % LANES == 0:
        acs_t, dt_t = acs.T, dt.T
    else:
        pad = jnp.zeros((LANES - q, LANES), F32)
        acs_t = jnp.concatenate([acs, pad], axis=0).T[:, :q]
        dt_t = jnp.concatenate([dt, pad], axis=0).T[:, :q]
    causal = (lax.broadcasted_iota(jnp.int32, (q, q), 0) >= lax.broadcasted_iota(jnp.int32, (q, q), 1))
    lane = lax.broadcasted_iota(jnp.int32, (q, LANES), 1)
    srow = lax.broadcasted_iota(jnp.int32, (LANES, LANES), 0)
    half = SSD_HEAD_DIM
    y_pairs = []
    for g in range(SSD_GROUPS):
        bm_g = bm[:, g * n:(g + 1) * n]
        cm_g = cm[:, g * n:(g + 1) * n]
        cb_g = _dot_nt(cm_g.astype(BF16), bm_g.astype(BF16))
        for jp in range(2):
            pair = 2 * g + jp
            m_parts, cce_parts, bcw_parts, cds = [], [], [], []
            for h in (2 * pair, 2 * pair + 1):
                colb = jnp.broadcast_to(acs[:, h:h + 1], (q, LANES))
                dcol = jnp.broadcast_to(dt[:, h:h + 1], (q, LANES))
                seg = colb[:, :q] - acs_t[h:h + 1, :]
                decay = jnp.exp(jnp.where(causal, seg, -jnp.inf))
                m_parts.append(cb_g * decay * dt_t[h:h + 1, :])
                cce_parts.append(cm_g * jnp.exp(colb))
                lastb = last[:, h:h + 1]
                bcw_parts.append(bm_g * (dcol * jnp.exp(lastb - colb)))
                cds.append(jnp.exp(lastb))
            xs_pair = xs[:, pair * LANES:(pair + 1) * LANES]
            top = jnp.where(lane < half, xs_pair, 0.0)
            bot = jnp.where(lane >= half, xs_pair, 0.0)
            w = jnp.concatenate([top, bot], axis=0).astype(BF16)
            if q % LANES == 0:
                yd = _dot(jnp.concatenate(m_parts, axis=1).astype(BF16), w)
            else:
                yd = _dot(m_parts[0], top) + _dot(m_parts[1], bot)
            s_pair = s_read(pair)
            s_blk = jnp.concatenate([jnp.where(srow < half, s_pair, 0.0),
                                     jnp.where(srow >= half, s_pair, 0.0)], axis=1).astype(BF16)
            yo = _dot_nt(jnp.concatenate(cce_parts, axis=1).astype(BF16), s_blk)
            ds = _dot_tn(w, jnp.concatenate(bcw_parts, axis=0).astype(BF16))
            cd = jnp.where(srow < half, jnp.broadcast_to(cds[0], (LANES, LANES)),
                           jnp.broadcast_to(cds[1], (LANES, LANES)))
            s_write(pair, s_pair * cd + ds)
            y_pairs.append(yd + yo + dexp_ref[:, pair * LANES:(pair + 1) * LANES] * xs_pair)
    gw = BRANCH // SSD_GROUPS
    y_groups = []
    for g in range(SSD_GROUPS):
        yg = jnp.concatenate(y_pairs[2 * g:2 * g + 2], axis=1) * jax.nn.silu(z[:, g * gw:(g + 1) * gw])
        y_groups.append(yg * lax.rsqrt(jnp.mean(yg * yg, axis=-1, keepdims=True) + EPS))
    return jnp.concatenate(y_groups, axis=1) * ng_ref[...], carry


def _ssd_prompt_kernel(zx_ref, dt_ref, cw_ref, cb_ref, dtb_ref, alog_ref, dexp_ref, ng_ref,
                       y_ref, sout_ref, xbuf, s_scr):
    q = zx_ref.shape[0]
    g = q // SUBLANES
    c = SSD_CONV_DIM

    @pl.when(pl.program_id(1) == 0)
    def _():
        xbuf[0:SUBLANES, :] = jnp.zeros((SUBLANES, c), F32)
        s_scr[...] = jnp.zeros_like(s_scr)

    xbuf[SUBLANES:, :] = zx_ref[:, BRANCH:]
    u3 = xbuf[SUBLANES:, :].reshape(g, SUBLANES, c)
    prev3 = xbuf[0:q, :].reshape(g, SUBLANES, c)

    def s_read(pair):
        return s_scr[pair * LANES:(pair + 1) * LANES, :]

    def s_write(pair, val):
        s_scr[pair * LANES:(pair + 1) * LANES, :] = val

    y, _ = _ssd_chunk(zx_ref[:, :BRANCH], u3, prev3, dt_ref[...], s_read, s_write, cw_ref, cb_ref,
                      dtb_ref, alog_ref, dexp_ref, ng_ref, jnp.zeros((SUBLANES, LANES), F32))
    xbuf[0:SUBLANES, :] = xbuf[q:, :]
    y_ref[...] = y.astype(BF16)

    @pl.when(pl.program_id(1) == pl.num_programs(1) - 1)
    def _():
        sout_ref[...] = s_scr[...]


def _ssd_sample_kernel(zx_ref, dt_ref, prev_ref, s0_ref, cw_ref, cb_ref, dtb_ref, alog_ref, dexp_ref,
                       ng_ref, y_ref, sout_ref, *, seq):
    c = SSD_CONV_DIM
    ys = []
    for n in range(zx_ref.shape[0] // seq):
        rows = slice(n * seq, (n + 1) * seq)
        u3 = zx_ref[rows, BRANCH:].reshape(1, seq, c)
        prev3 = prev_ref[rows, :].reshape(1, seq, c)

        def s_read(pair, n=n):
            return s0_ref[n, pair * LANES:(pair + 1) * LANES, :]

        def s_write(pair, val, n=n):
            sout_ref[n, pair * LANES:(pair + 1) * LANES, :] = val

        y, _ = _ssd_chunk(zx_ref[rows, :BRANCH], u3, prev3, dt_ref[rows, :], s_read, s_write, cw_ref,
                          cb_ref, dtb_ref, alog_ref, dexp_ref, ng_ref, jnp.zeros((SUBLANES, LANES), F32))
        ys.append(y)
    y_ref[...] = jnp.concatenate(ys, axis=0).astype(BF16)


def _ssd_weight_specs(p):
    return [_const_spec(p[k].shape) for k in ("conv_w", "conv_b", "dt_bias", "a_log", "d_exp", "norm_g")]


def _ssd_weights(p):
    return tuple(p[k] for k in ("conv_w", "conv_b", "dt_bias", "a_log", "d_exp", "norm_g"))


def _ssd_core_prompt(zx, dt, p, bsz, seq):
    q = SSD_CHUNK
    nt = seq // q
    hp = SSD_HEADS * SSD_HEAD_DIM
    return pl.pallas_call(
        _ssd_prompt_kernel,
        grid=(bsz, nt),
        in_specs=[pl.BlockSpec((q, zx.shape[1]), lambda b, i: (b * nt + i, 0)),
                  pl.BlockSpec((q, LANES), lambda b, i: (b * nt + i, 0))] + _ssd_weight_specs(p),
        out_specs=[pl.BlockSpec((q, BRANCH), lambda b, i: (b * nt + i, 0)),
                   pl.BlockSpec((None, hp, SSD_STATE), lambda b, i: (b, 0, 0))],
        out_shape=[jax.ShapeDtypeStruct((bsz * seq, BRANCH), BF16),
                   jax.ShapeDtypeStruct((bsz, hp, SSD_STATE), F32)],
        scratch_shapes=[pltpu.VMEM((SUBLANES + q, SSD_CONV_DIM), F32), pltpu.VMEM((hp, SSD_STATE), F32)],
        compiler_params=_params(2), name="ssd_core_prompt",
    )(zx, dt, *_ssd_weights(p))


def _ssd_core_sample(zx, dt, prev8, s0, idx, p, seq, nb):
    t = zx.shape[0]
    bsz = t // seq
    hp = SSD_HEADS * SSD_HEAD_DIM
    row = lambda w: pl.BlockSpec((nb * seq, w), lambda i: (i, 0))
    st = pl.BlockSpec((nb, hp, SSD_STATE), lambda i: (i, 0, 0))
    st_in = pl.BlockSpec((None, nb, hp, SSD_STATE), lambda i: (idx, i, 0, 0))
    return pl.pallas_call(
        functools.partial(_ssd_sample_kernel, seq=seq),
        grid=(bsz // nb,),
        in_specs=[row(zx.shape[1]), row(LANES), row(SSD_CONV_DIM), st_in] + _ssd_weight_specs(p),
        out_specs=[row(BRANCH), st],
        out_shape=[jax.ShapeDtypeStruct((t, BRANCH), BF16), jax.ShapeDtypeStruct((bsz, hp, SSD_STATE), F32)],
        compiler_params=_params(1), name="ssd_core_sample",
    )(zx, dt, prev8, s0, *_ssd_weights(p))


def _hg_lower_bound(rows, layer):
    mx = functools.reduce(jnp.maximum, rows)
    es = [jnp.exp(x - mx) for x in rows]
    return sum(es[1:layer + 1]) / sum(es)


def _hg_prompt_kernel(proj_ref, hlb_ref, ng_ref, y_ref, sout_ref, st_scr, *, layer):
    r = proj_ref.shape[0]
    c = BRANCH
    dk = HGRN_KEY_DIM
    blk, sub = HGRN_BLOCK, HGRN_CHUNK
    nblk, nsub = r // blk, blk // sub

    @pl.when(pl.program_id(1) == 0)
    def _():
        st_scr[...] = jnp.zeros_like(st_scr)

    causal = (lax.broadcasted_iota(jnp.int32, (blk, blk), 0)
              >= lax.broadcasted_iota(jnp.int32, (blk, blk), 1))

    def head_body(h, carry):
        lanes = pl.ds(pl.multiple_of(h * dk, dk), dk)

        def proj_part(part):
            return proj_ref[:, pl.ds(pl.multiple_of(part * c + h * dk, dk), dk)]

        lb = _hg_lower_bound([hlb_ref[j:j + 1, lanes] for j in range(DEPTH)], layer)
        f = proj_part(1)
        forget = lb + (1.0 - lb) * jax.nn.sigmoid(f)
        k = ((1.0 - lb) * jax.nn.sigmoid(-f)).reshape(nblk, blk, dk)
        x = _cumsum8(jnp.log(forget).reshape(r // SUBLANES, SUBLANES, dk)).reshape(nblk, blk, dk)
        parts, carry_row = [], None
        for j in range(blk // SUBLANES):
            part = x[:, j * SUBLANES:(j + 1) * SUBLANES, :]
            if carry_row is not None:
                part = part + carry_row
            carry_row = jnp.broadcast_to(part[:, SUBLANES - 1:, :], part.shape)
            parts.append(part)
        gc = jnp.concatenate(parts, axis=1)
        ends = [gc[:, (i + 1) * sub - 1:(i + 1) * sub, :] for i in range(nsub)]
        starts = [jnp.zeros_like(ends[0])] + ends[:-1]
        spread = lambda rows_: jnp.concatenate(
            [jnp.broadcast_to(x_, (nblk, sub, dk)) for x_ in rows_], axis=1)
        b_prev, b_next = spread(starts), spread(ends)
        qi = jax.nn.silu(proj_part(0)).reshape(nblk, blk, dk) * jnp.exp(gc - b_prev)
        qc = qi * jnp.exp(b_prev)
        kd = k * jnp.exp(b_prev - gc)
        ke = kd * jnp.exp(b_next - b_prev)
        kend = ke * jnp.exp(ends[-1] - b_next)
        v = proj_part(2).reshape(nblk, blk, dk)

        st = st_scr[h]
        outs = []
        for b in range(nblk):
            qi_b = qi[b].astype(BF16)
            att_rows = []
            for i in range(nsub):
                keys = []
                for j in range(nsub):
                    rs = slice(j * sub, (j + 1) * sub)
                    if j == i:
                        keys.append(kd[b, rs])
                    elif j < i - 1:
                        keys.append(ke[b, rs] * jnp.exp(starts[i][b] - ends[j][b]))
                    else:
                        keys.append(ke[b, rs])
                keys = jnp.concatenate(keys, axis=0).astype(BF16)
                att_rows.append(_dot_nt(qi_b[i * sub:(i + 1) * sub], keys))
            att = jnp.where(causal, jnp.concatenate(att_rows, axis=0), 0.0).astype(BF16)
            vb = v[b].astype(BF16)
            outs.append(_dot(att, vb) + _dot_nt(qc[b].astype(BF16), st.astype(BF16)))
            st = st * jnp.exp(ends[-1][b]) + _dot_tn(vb, kend[b].astype(BF16))
        st_scr[h] = st
        o = jnp.concatenate(outs, axis=0)
        o = o * lax.rsqrt(jnp.mean(o * o, axis=-1, keepdims=True) + EPS)
        gate = jax.nn.silu(proj_part(3))
        y_ref[:, lanes] = (o * ng_ref[:, lanes] * gate).astype(BF16)
        return carry

    lax.fori_loop(0, HGRN_HEADS, head_body, 0, unroll=8)

    @pl.when(pl.program_id(1) == pl.num_programs(1) - 1)
    def _():
        for h in range(HGRN_HEADS):
            sout_ref[h] = st_scr[h].T


def _hg_seq_kernel(proj_ref, hlb_ref, ng_ref, s0_ref, y_ref, sout_ref,
                   qg_scr, kg_scr, ke_scr, v_scr, dec_scr, o_scr, *, layer, chunk):
    r = proj_ref.shape[0]
    c = BRANCH
    g = r // SUBLANES
    n_chunks = r // chunk
    dk, dv = HGRN_KEY_DIM, HGRN_VAL_DIM
    assert chunk == SUBLANES

    lb = _hg_lower_bound([hlb_ref[j:j + 1, :] for j in range(DEPTH)], layer)
    f = proj_ref[:, c:2 * c]
    forget = lb + (1.0 - lb) * jax.nn.sigmoid(f)
    k = (1.0 - lb) * jax.nn.sigmoid(-f)
    gcum3 = _cumsum8(jnp.log(forget).reshape(g, SUBLANES, c))
    gcum = gcum3.reshape(r, c)
    glast = jnp.broadcast_to(gcum3[:, SUBLANES - 1:, :], gcum3.shape).reshape(r, c)
    qg_scr[...] = jax.nn.silu(proj_ref[:, :c]) * jnp.exp(gcum)
    kg_scr[...] = k * jnp.exp(-gcum)
    ke_scr[...] = k * jnp.exp(glast - gcum)
    dec_scr[...] = jnp.exp(glast)
    v_scr[...] = proj_ref[:, 2 * c:3 * c]

    causal = (lax.broadcasted_iota(jnp.int32, (chunk, chunk), 0)
              >= lax.broadcasted_iota(jnp.int32, (chunk, chunk), 1))

    def chunk_body(ci, carry):
        rs = pl.ds(pl.multiple_of(ci * chunk, chunk), chunk)
        outs = []
        for h in range(HGRN_HEADS):
            ks = slice(h * dk, (h + 1) * dk)
            vs = slice(h * dv, (h + 1) * dv)
            qg = qg_scr[rs, ks].astype(BF16)
            vv = v_scr[rs, vs].astype(BF16)
            st = s0_ref[ci, h].T
            att = jnp.where(causal, _dot_nt(qg, kg_scr[rs, ks].astype(BF16)), 0.0)
            outs.append(_dot(att.astype(BF16), vv) + _dot_nt(qg, st.astype(BF16)))
            dec = dec_scr[rs, ks][chunk - 1:, :]
            sout_ref[ci, h] = (st * dec + _dot_tn(vv, ke_scr[rs, ks].astype(BF16))).T
        o_scr[rs, :] = jnp.concatenate(outs, axis=1)
        return carry

    lax.fori_loop(0, n_chunks, chunk_body, 0, unroll=2)

    gate = jax.nn.silu(proj_ref[:, 3 * c:])
    parts = []
    for h in range(HGRN_HEADS):
        o = o_scr[:, h * dv:(h + 1) * dv]
        parts.append(o * lax.rsqrt(jnp.mean(o * o, axis=-1, keepdims=True) + EPS))
    y_ref[...] = (jnp.concatenate(parts, axis=1) * ng_ref[...] * gate).astype(BF16)


def _hg_core_prompt(proj, hlb, ng, layer, bsz, seq, tl):
    nt = seq // tl
    c = BRANCH
    st_shape = (HGRN_HEADS, HGRN_KEY_DIM, HGRN_VAL_DIM)
    return pl.pallas_call(
        functools.partial(_hg_prompt_kernel, layer=layer),
        grid=(bsz, nt),
        in_specs=[pl.BlockSpec((tl, 4 * c), lambda b, i: (b * nt + i, 0)),
                  _const_spec(hlb.shape), _const_spec(ng.shape)],
        out_specs=[pl.BlockSpec((tl, c), lambda b, i: (b * nt + i, 0)),
                   pl.BlockSpec((None,) + st_shape, lambda b, i: (b, 0, 0, 0))],
        out_shape=[jax.ShapeDtypeStruct((bsz * seq, c), BF16),
                   jax.ShapeDtypeStruct((bsz,) + st_shape, F32)],
        scratch_shapes=[pltpu.VMEM((HGRN_HEADS, HGRN_VAL_DIM, HGRN_KEY_DIM), F32)],
        compiler_params=_params(2), name="hg_core_prompt",
    )(proj, hlb, ng)


def _hg_core_sample(proj, s0, idx, hlb, ng, layer, seq, nb):
    t = proj.shape[0]
    bsz = t // seq
    c = BRANCH
    rows = nb * seq
    st_shape = (nb, HGRN_HEADS, HGRN_KEY_DIM, HGRN_VAL_DIM)
    st_spec = pl.BlockSpec(st_shape, lambda i: (i, 0, 0, 0))
    st_in = pl.BlockSpec((None,) + st_shape, lambda i: (idx, i, 0, 0, 0))
    return pl.pallas_call(
        functools.partial(_hg_seq_kernel, layer=layer, chunk=seq),
        grid=(bsz // nb,),
        in_specs=[pl.BlockSpec((rows, 4 * c), lambda i: (i, 0)),
                  _const_spec(hlb.shape), _const_spec(ng.shape), st_in],
        out_specs=[pl.BlockSpec((rows, c), lambda i: (i, 0)), st_spec],
        out_shape=[jax.ShapeDtypeStruct((t, c), BF16), jax.ShapeDtypeStruct(s0.shape[1:], F32)],
        scratch_shapes=[pltpu.VMEM((rows, c), F32)] * 6,
        compiler_params=_params(1), name="hg_core_sample",
    )(proj, hlb, ng, s0)


def _tile(n, target):
    t = min(n, target)
    assert n % t == 0, (n, target)
    return t


def _pad_groups(state, first_row):
    n, k, c = state.shape
    return jnp.pad(state, ((0, 0), (first_row, SUBLANES - first_row - k), (0, 0))).reshape(n * SUBLANES, c)


def _trunk(x, mem_k, mem_v, states, w, bsz, seq, prompt):
    tm = _tile(x.shape[0], 256)
    rg_conv, rg_h, ssd_conv, ssd_s, hg_s = [], [], [], [], []
    tail = slice(seq - (CONV_W - 1), seq)
    for layer in range(DEPTH):
        kind, idx = layer % N_MIXERS, layer // N_MIXERS
        g = w["norm_g"][layer]
        if kind == 0:
            p = w["rg"][idx]
            if prompt:
                x, utail, htail = _rg_layer_prompt(x, g, p, bsz, seq, _tile(seq, 512))
                rg_h.append(htail[:, SUBLANES - 1])
                rg_conv.append(utail[:, :, 0])
                y = None
            else:
                (proj,) = _norm_matmul(x, g[0:1], [p["w_in"]], tm)
                prev8 = _pad_groups(states["rg_conv"][idx], SUBLANES - (CONV_W - 1))
                h0pad = _pad_groups(states["rg_h"][idx][:, None, :], 0)
                y, h = _rg_core_sample(proj, prev8, h0pad, p, tm)
                rg_h.append(h.reshape(bsz, seq, BRANCH)[:, seq - 1])
                rg_conv.append(proj.reshape(bsz, seq, 2 * BRANCH)[:, tail, :BRANCH])
        elif kind == 1:
            p = w["ssd"][idx]
            zx, dt = _norm_matmul(x, g[0:1], [p["w_zx"], p["w_dt"]], tm)
            if prompt:
                y, s_new = _ssd_core_prompt(zx, dt, p, bsz, seq)
            else:
                prev8 = _pad_groups(states["ssd_conv"][idx], SUBLANES - (CONV_W - 1))
                s0 = states["ssd_s"].reshape(-1, bsz, SSD_HEADS * SSD_HEAD_DIM, SSD_STATE)
                y, s_new = _ssd_core_sample(zx, dt, prev8, s0, idx, p, seq, 2)
            ssd_s.append(s_new.reshape(bsz, SSD_HEADS, SSD_HEAD_DIM, SSD_STATE))
            ssd_conv.append(zx.reshape(bsz, seq, BRANCH + SSD_CONV_DIM)[:, tail, BRANCH:])
        else:
            p = w["hg"][idx]
            (proj,) = _norm_matmul(x, g[0:1], [p["w_in"]], tm)
            if prompt:
                y, s_new = _hg_core_prompt(proj, w["hg_lower_bounds"], p["norm_g"], layer, bsz, seq,
                                           _tile(seq, 256))
            else:
                y, s_new = _hg_core_sample(proj, states["hg_s"], idx, w["hg_lower_bounds"], p["norm_g"],
                                           layer, seq, 8)
            hg_s.append(s_new)
        if y is not None:
            x = _proj_norm_res(y, p["w_out"], g[1:2], x, _tile(x.shape[0], 512))
        if prompt:
            x = _attn_prompt(x, mem_k, mem_v, layer, w["x_w_q"][layer], w["x_w_o"][layer], g,
                             bsz, seq, _tile(seq, 1024))
        else:
            x = _attn_sample(x, mem_k, mem_v, layer, w["x_w_q"][layer], w["x_w_o"][layer], g,
                             seq, 4)
    return x, jnp.stack(rg_conv), jnp.stack(rg_h), jnp.stack(ssd_conv), jnp.stack(ssd_s), jnp.stack(hg_s)


def kernel(x_prompt, x_sample, mem_prompt, state_rglru_conv, state_rglru_h, state_ssd_conv, state_ssd,
           state_hgrn, cache_mem_k, cache_mem_v, norm_g, mem_norm_g, rg_w_in, rg_conv_w, rg_conv_b, rg_w_a,
           rg_b_a, rg_w_x, rg_b_x, rg_lambda, rg_w_out, ssd_w_in, ssd_conv_w, ssd_conv_b, ssd_dt_bias,
           ssd_a_log, ssd_d, ssd_norm_g, ssd_w_out, hg_w_in, hg_lower_bounds, hg_norm_g, hg_w_out,
           x_w_q, x_w_k, x_w_v, x_w_o):
    bp, sp, d = x_prompt.shape
    bs, ss, _ = x_sample.shape
    n_a, n_b, n_c = rg_w_in.shape[0], ssd_w_in.shape[0], hg_w_in.shape[0]
    pad_heads = lambda v: jnp.pad(v, (0, LANES - SSD_HEADS))[None, :]
    w = {
        "norm_g": norm_g,
        "hg_lower_bounds": hg_lower_bounds,
        "x_w_q": x_w_q.astype(BF16),
        "x_w_o": x_w_o.astype(BF16),
        "rg": [{
            "w_in": rg_w_in[i].astype(BF16),
            "conv_w": rg_conv_w[i], "conv_b": rg_conv_b[i][None, :],
            "w_ax": jnp.concatenate([rg_w_a[i], rg_w_x[i]], axis=-1).astype(BF16),
            "b_a": rg_b_a[i][None, :], "b_x": rg_b_x[i][None, :], "lam": rg_lambda[i][None, :],
            "w_out": rg_w_out[i].astype(BF16),
        } for i in range(n_a)],
        "ssd": [{
            "w_zx": ssd_w_in[i][:, :BRANCH + SSD_CONV_DIM].astype(BF16),
            "w_dt": jnp.pad(ssd_w_in[i][:, BRANCH + SSD_CONV_DIM:], ((0, 0), (0, LANES - SSD_HEADS))).astype(BF16),
            "conv_w": ssd_conv_w[i], "conv_b": ssd_conv_b[i][None, :],
            "dt_bias": pad_heads(ssd_dt_bias[i]), "a_log": pad_heads(ssd_a_log[i]),
            "d_exp": jnp.repeat(ssd_d[i], SSD_HEAD_DIM)[None, :],
            "norm_g": ssd_norm_g[i][None, :],
            "w_out": ssd_w_out[i].astype(BF16),
        } for i in range(n_b)],
        "hg": [{
            "w_in": hg_w_in[i].astype(BF16),
            "norm_g": hg_norm_g[i][None, :],
            "w_out": hg_w_out[i].astype(BF16),
        } for i in range(n_c)],
    }

    mem_k_p, mem_v_p = _mem_kv(mem_prompt.reshape(bp * N_MEM, d), mem_norm_g[:, None, :],
                               x_w_k.astype(BF16), x_w_v.astype(BF16), _tile(bp, 2))
    y_p, rgc_p, rgh_p, sc_p, ss_p, hs_p = _trunk(
        x_prompt.reshape(bp * sp, d), mem_k_p, mem_v_p, None, w, bp, sp, True)
    states = {"rg_conv": state_rglru_conv, "rg_h": state_rglru_h, "ssd_conv": state_ssd_conv,
              "ssd_s": state_ssd, "hg_s": state_hgrn}
    y_s, rgc_s, rgh_s, sc_s, ss_s, hs_s = _trunk(
        x_sample.reshape(bs * ss, d), _kv_flat(cache_mem_k), _kv_flat(cache_mem_v), states, w, bs, ss, False)
    return (y_p.reshape(bp, sp, d), y_s.reshape(bs, ss, d), rgc_p, rgh_p, sc_p, ss_p, hs_p,
            _kv_unflat(mem_k_p), _kv_unflat(mem_v_p), rgc_s, rgh_s, sc_s, ss_s, hs_s)
```

```python
import functools

import jax
import jax.numpy as jnp
from jax import lax
from jax.experimental import pallas as pl
from jax.experimental.pallas import tpu as pltpu

F32 = jnp.float32
BF16 = jnp.bfloat16

D_MODEL = 1024
DEPTH = 4
N_MIXERS = 3
BRANCH = 2 * D_MODEL
CONV_W = 4
EPS = 1e-6
LRU_BLOCKS = 8
LRU_BLOCK = BRANCH // LRU_BLOCKS
LRU_C = 8.0
SSD_HEAD_DIM = 64
SSD_HEADS = BRANCH // SSD_HEAD_DIM
SSD_STATE = 128
SSD_GROUPS = 8
SSD_CONV_DIM = BRANCH + 2 * SSD_GROUPS * SSD_STATE
SSD_CHUNK = 128
HGRN_KEY_DIM = 128
HGRN_HEADS = BRANCH // HGRN_KEY_DIM
HGRN_VAL_DIM = BRANCH // HGRN_HEADS
HGRN_CHUNK = 16
HGRN_BLOCK = 4 * HGRN_CHUNK
N_MEM = 256
X_HEADS = 4
X_HEAD_DIM = D_MODEL // X_HEADS

SUBLANES = 8
LANES = 128
VMEM_BYTES_V7X = 64 * 1024 * 1024
VMEM_LIMIT = VMEM_BYTES_V7X * 7 // 8

NT_DIMS = (((1,), (1,)), ((), ()))
TN_DIMS = (((0,), (0,)), ((), ()))


def _params(n_grid_dims):
    return pltpu.CompilerParams(
        dimension_semantics=("arbitrary",) * n_grid_dims, vmem_limit_bytes=VMEM_LIMIT)


def _const_spec(shape):
    nd = len(shape)
    return pl.BlockSpec(shape, lambda *_: (0,) * nd, pipeline_mode=pl.Buffered(1))


def _rms(x, g):
    return x * lax.rsqrt(jnp.mean(x * x, axis=-1, keepdims=True) + EPS) * g


def _dot(a, b):
    return jnp.dot(a, b, preferred_element_type=F32)


def _dot_nt(a, b):
    return lax.dot_general(a, b, NT_DIMS, preferred_element_type=F32)


def _dot_tn(a, b):
    return lax.dot_general(a, b, TN_DIMS, preferred_element_type=F32)


def _group_iota(width):
    return lax.broadcasted_iota(jnp.int32, (1, SUBLANES, width), 1)


def _conv8(u3, prev3, cw, cb):
    t = _group_iota(u3.shape[-1])
    acc = cb + cw[CONV_W - 1:CONV_W, :] * u3
    for k in range(1, CONV_W):
        shifted = pltpu.roll(jnp.where(t >= SUBLANES - k, prev3, u3), k, 1)
        acc = acc + cw[CONV_W - 1 - k:CONV_W - k, :] * shifted
    return acc


def _scan8(a3, b3):
    t = _group_iota(a3.shape[-1])
    for s in (1, 2, 4):
        m = t >= s
        a_sh = pltpu.roll(a3, s, 1)
        b_sh = pltpu.roll(b3, s, 1)
        b3 = jnp.where(m, a3 * b_sh + b3, b3)
        a3 = jnp.where(m, a3 * a_sh, a3)
    return a3, b3


def _cumsum8(x3):
    t = _group_iota(x3.shape[-1])
    for s in (1, 2, 4):
        x3 = x3 + jnp.where(t >= s, pltpu.roll(x3, s, 1), 0.0)
    return x3


def _norm_matmul_kernel(x_ref, g_ref, *refs, n_chunk):
    n_w = len(refs) // 2
    h = _rms(x_ref[...], g_ref[...]).astype(BF16)
    for w_ref, o_ref in zip(refs[:n_w], refs[n_w:]):
        n = o_ref.shape[-1]
        step = min(n_chunk, n)
        for c in range(0, n, step):
            o_ref[:, c:c + step] = _dot(h, w_ref[:, c:c + step])


def _norm_matmul(x, g, ws, tm):
    t, d = x.shape
    grid = (t // tm,)
    in_specs = [pl.BlockSpec((tm, d), lambda i: (i, 0)), _const_spec((1, d))]
    in_specs += [_const_spec(w.shape) for w in ws]
    out_specs = [pl.BlockSpec((tm, w.shape[1]), lambda i: (i, 0)) for w in ws]
    out_shape = [jax.ShapeDtypeStruct((t, w.shape[1]), F32) for w in ws]
    return pl.pallas_call(
        functools.partial(_norm_matmul_kernel, n_chunk=512),
        grid=grid, in_specs=in_specs, out_specs=out_specs, out_shape=out_shape,
        compiler_params=_params(1), name="norm_matmul",
    )(x, g, *ws)


def _proj_norm_res_kernel(a_ref, w_ref, g_ref, x_ref, o_ref):
    y = _dot(a_ref[...], w_ref[...])
    o_ref[...] = x_ref[...] + _rms(y, g_ref[...])


def _proj_norm_res(a, w, g, x, tm):
    t, k = a.shape
    d = x.shape[1]
    return pl.pallas_call(
        _proj_norm_res_kernel,
        grid=(t // tm,),
        in_specs=[pl.BlockSpec((tm, k), lambda i: (i, 0)), _const_spec(w.shape), _const_spec((1, d)),
                  pl.BlockSpec((tm, d), lambda i: (i, 0))],
        out_specs=pl.BlockSpec((tm, d), lambda i: (i, 0)),
        out_shape=jax.ShapeDtypeStruct((t, d), F32),
        compiler_params=_params(1), name="proj_norm_res",
    )(a, w, g, x)


KV_LANE_TILES = X_HEAD_DIM // LANES
KV_ROWS = KV_LANE_TILES * X_HEADS


def _kv_flat(kv):
    lead = kv.shape[:-3]
    x = kv.reshape(lead + (N_MEM, X_HEADS, KV_LANE_TILES, LANES))
    return jnp.swapaxes(x, -3, -2).reshape(lead + (N_MEM * KV_ROWS, LANES))


def _kv_unflat(flat):
    lead = flat.shape[:-2]
    x = flat.reshape(lead + (N_MEM, KV_LANE_TILES, X_HEADS, LANES))
    return jnp.swapaxes(x, -3, -2).reshape(lead + (N_MEM, X_HEADS, X_HEAD_DIM))


def _mem_kv_kernel(m_ref, g_ref, wk_ref, wv_ref, k_ref, v_ref):
    h = _rms(m_ref[...], g_ref[...]).astype(BF16)
    tm = m_ref.shape[0]
    for w_ref, o_ref in ((wk_ref, k_ref), (wv_ref, v_ref)):
        y = _dot(h, w_ref[...])
        pieces = [y[:, hd * X_HEAD_DIM + t * LANES:hd * X_HEAD_DIM + (t + 1) * LANES][None]
                  for t in range(KV_LANE_TILES) for hd in range(X_HEADS)]
        rows = jnp.swapaxes(jnp.concatenate(pieces, axis=0), 0, 1)
        o_ref[...] = rows.reshape(o_ref.shape)


def _mem_kv(mem, g, wk, wv, nb):
    t, d = mem.shape
    bsz = t // N_MEM
    tm = nb * N_MEM
    w_spec = pl.BlockSpec((None, d, d), lambda l, i: (l, 0, 0))
    o_spec = pl.BlockSpec((None, nb, N_MEM * KV_ROWS, LANES), lambda l, i: (l, i, 0, 0))
    return pl.pallas_call(
        _mem_kv_kernel,
        grid=(DEPTH, bsz // nb),
        in_specs=[pl.BlockSpec((tm, d), lambda l, i: (i, 0)),
                  pl.BlockSpec((None, 1, d), lambda l, i: (l, 0, 0)), w_spec, w_spec],
        out_specs=[o_spec, o_spec],
        out_shape=[jax.ShapeDtypeStruct((DEPTH, bsz, N_MEM * KV_ROWS, LANES), F32)] * 2,
        compiler_params=_params(2), name="mem_kv",
    )(mem, g, wk, wv)


def _attn_kernel(x_ref, k_ref, v_ref, wq_ref, wo_ref, g_ref, o_ref, q_scr, a_scr, k_scr, v_scr, *,
                 rows, hoist, seq_rows):
    i = pl.program_id(0) if hoist else None
    scale = X_HEAD_DIM ** -0.5
    n_keys = k_scr.shape[1]

    def split_heads():
        def every_kv_row(ref, first):
            rows = pl.ds(first, N_MEM, stride=KV_ROWS)
            if len(ref.shape) == 2:
                return ref[rows, :]
            return ref[:, rows, :].reshape(n_keys, LANES)

        for ref, scr in ((k_ref, k_scr), (v_ref, v_scr)):
            for h in range(X_HEADS):
                scr[h] = jnp.concatenate([every_kv_row(ref, t * X_HEADS + h) for t in range(KV_LANE_TILES)],
                                         axis=1).astype(BF16)

    if hoist:
        split_heads()
    else:
        pl.when(pl.program_id(1) == 0)(split_heads)

    def project_q():
        h = _rms(x_ref[...], g_ref[2:3, :]).astype(BF16)
        q_scr[...] = (_dot(h, wq_ref[...]) * scale).astype(BF16)

    def project_out():
        y = _dot(a_scr[...], wo_ref[...])
        o_ref[...] = x_ref[...] + _rms(y, g_ref[3:4, :])

    if hoist:
        pl.when(i == 0)(project_q)
        r0 = pl.multiple_of(i * rows, rows)
        rsl = pl.ds(r0, rows)
    else:
        project_q()
        rsl = slice(None)

    if seq_rows is not None:
        qi = lax.broadcasted_iota(jnp.int32, (rows, n_keys), 0) // seq_rows
        ki = lax.broadcasted_iota(jnp.int32, (rows, n_keys), 1) // N_MEM
        mask = qi == ki
    for h in range(X_HEADS):
        hs = slice(h * X_HEAD_DIM, (h + 1) * X_HEAD_DIM)
        qh = q_scr[rsl, hs]
        s = _dot_nt(qh, k_scr[h])
        if seq_rows is not None:
            s = jnp.where(mask, s, -jnp.inf)
        e = jnp.exp(s - jnp.max(s, axis=-1, keepdims=True))
        p = e / jnp.sum(e, axis=-1, keepdims=True)
        a_scr[rsl, hs] = _dot(p.astype(BF16), v_scr[h]).astype(BF16)

    if hoist:
        pl.when(i == pl.num_programs(0) - 1)(project_out)
    else:
        project_out()


def _attn_prompt(x, k, v, layer, wq, wo, g, bsz, seq, tl):
    d = x.shape[1]
    nt = seq // tl
    kv_spec = pl.BlockSpec((None, None, N_MEM * KV_ROWS, LANES), lambda b, i: (layer, b, 0, 0))
    kv_scr = pltpu.VMEM((X_HEADS, N_MEM, X_HEAD_DIM), BF16)
    return pl.pallas_call(
        functools.partial(_attn_kernel, rows=tl, hoist=False, seq_rows=None),
        grid=(bsz, nt),
        in_specs=[pl.BlockSpec((tl, d), lambda b, i: (b * nt + i, 0)), kv_spec, kv_spec,
                  _const_spec(wq.shape), _const_spec(wo.shape), _const_spec(g.shape)],
        out_specs=pl.BlockSpec((tl, d), lambda b, i: (b * nt + i, 0)),
        out_shape=jax.ShapeDtypeStruct(x.shape, F32),
        scratch_shapes=[pltpu.VMEM((tl, d), BF16), pltpu.VMEM((tl, d), BF16), kv_scr, kv_scr],
        compiler_params=_params(2), name="attn_prompt",
    )(x, k, v, wq, wo, g)


def _attn_sample(x, k, v, layer, wq, wo, g, seq, nb):
    t, d = x.shape
    bsz = t // seq
    kv_spec = pl.BlockSpec((None, nb, N_MEM * KV_ROWS, LANES), lambda i: (layer, i, 0, 0))
    kv_scr = pltpu.VMEM((X_HEADS, nb * N_MEM, X_HEAD_DIM), BF16)
    return pl.pallas_call(
        functools.partial(_attn_kernel, rows=nb * seq, hoist=True, seq_rows=seq),
        grid=(bsz // nb,),
        in_specs=[_const_spec(x.shape), kv_spec, kv_spec,
                  _const_spec(wq.shape), _const_spec(wo.shape), _const_spec(g.shape)],
        out_specs=pl.BlockSpec(x.shape, lambda i: (0, 0)),
        out_shape=jax.ShapeDtypeStruct(x.shape, F32),
        scratch_shapes=[pltpu.VMEM((t, d), BF16), pltpu.VMEM((t, d), BF16), kv_scr, kv_scr],
        compiler_params=_params(1), name="attn_sample",
    )(x, k, v, wq, wo, g)


def _rg_gates(conv, blk, wax_ref, ba_ref, bx_ref, lam_ref):
    sl = slice(blk * LRU_BLOCK, (blk + 1) * LRU_BLOCK)
    pre = _dot(conv.astype(BF16), wax_ref[blk])
    rg = jax.nn.sigmoid(pre[:, :LRU_BLOCK] + ba_ref[:, sl])
    ig = jax.nn.sigmoid(pre[:, LRU_BLOCK:] + bx_ref[:, sl])
    a = jnp.exp(rg * (-LRU_C * jax.nn.softplus(-lam_ref[:, sl])))
    b = jnp.exp(0.5 * jnp.log(1.0 - a * a)) * (ig * conv)
    return a, b


def _rg_block_ab(u3, prev3, blk, cw_ref, cb_ref, wax_ref, ba_ref, bx_ref, lam_ref):
    g = u3.shape[0]
    sl = slice(blk * LRU_BLOCK, (blk + 1) * LRU_BLOCK)
    conv = _conv8(u3, prev3, cw_ref[:, sl], cb_ref[:, sl]).reshape(g * SUBLANES, LRU_BLOCK)
    a, b = _rg_gates(conv, blk, wax_ref, ba_ref, bx_ref, lam_ref)
    return a.reshape(g, SUBLANES, LRU_BLOCK), b.reshape(g, SUBLANES, LRU_BLOCK)


def _rg_ab(u3, prev3, cw_ref, cb_ref, wax_ref, ba_ref, bx_ref, lam_ref):
    parts = [_rg_block_ab(u3[:, :, blk * LRU_BLOCK:(blk + 1) * LRU_BLOCK],
                          prev3[:, :, blk * LRU_BLOCK:(blk + 1) * LRU_BLOCK],
                          blk, cw_ref, cb_ref, wax_ref, ba_ref, bx_ref, lam_ref)
             for blk in range(LRU_BLOCKS)]
    return (jnp.concatenate([p[0] for p in parts], axis=2), jnp.concatenate([p[1] for p in parts], axis=2))


def _rg_layer_prompt_kernel(x_ref, g_ref, win_ref, cw_ref, cb_ref, wax_ref, ba_ref, bx_ref, lam_ref,
                            wout_ref, o_ref, utail_ref, htail_ref, tail_scr, y_scr, hc):
    r, d = x_ref.shape
    c = BRANCH
    n = r // SUBLANES
    taps = CONV_W - 1

    @pl.when(pl.program_id(1) == 0)
    def _():
        tail_scr[...] = jnp.zeros_like(tail_scr)
        hc[...] = jnp.zeros_like(hc)

    xs = jnp.swapaxes(x_ref[...].reshape(SUBLANES, n, d), 0, 1).reshape(r, d)
    xn = _rms(xs, g_ref[0:1, :]).astype(BF16)
    first_seg = lax.broadcasted_iota(jnp.int32, (SUBLANES, LRU_BLOCK), 0) == 0
    for blk in range(LRU_BLOCKS):
        sl = slice(blk * LRU_BLOCK, (blk + 1) * LRU_BLOCK)
        u3 = _dot(xn, win_ref[:, sl]).reshape(n, SUBLANES, LRU_BLOCK)
        gate = _dot(xn, win_ref[:, c + blk * LRU_BLOCK:c + (blk + 1) * LRU_BLOCK])
        hist = []
        for j in range(taps):
            prev_seg = pltpu.roll(u3[n - taps + j], 1, 0)
            hist.append(jnp.where(first_seg, tail_scr[j, :, sl], prev_seg)[None])
            tail_scr[j, :, sl] = prev_seg
            utail_ref[j, :, sl] = prev_seg
        uext = jnp.concatenate(hist + [u3], axis=0)
        conv = cb_ref[:, sl] + cw_ref[taps:CONV_W, sl] * u3
        for k in range(1, CONV_W):
            conv = conv + cw_ref[taps - k:CONV_W - k, sl] * uext[taps - k:taps - k + n]
        a, b = _rg_gates(conv.reshape(r, LRU_BLOCK), blk, wax_ref, ba_ref, bx_ref, lam_ref)
        a3 = a.reshape(n, SUBLANES, LRU_BLOCK)
        b3 = b.reshape(n, SUBLANES, LRU_BLOCK)
        hs, prods = [b3[0]], [a3[0]]
        for s in range(1, n):
            hs.append(a3[s] * hs[-1] + b3[s])
            prods.append(a3[s] * prods[-1])
        h_in = hc[0:1, sl]
        carry_rows = []
        for q in range(SUBLANES):
            carry_rows.append(h_in)
            h_in = hs[-1][q:q + 1] + prods[-1][q:q + 1] * h_in
        carry = jnp.concatenate(carry_rows, axis=0)
        h_last = jnp.broadcast_to(h_in, (SUBLANES, LRU_BLOCK))
        hc[:, sl] = h_last
        htail_ref[:, sl] = h_last
        h = jnp.concatenate([(hs[s] + prods[s] * carry)[None] for s in range(n)], axis=0)
        y_scr[:, sl] = (h.reshape(r, LRU_BLOCK) * jax.nn.silu(gate)).astype(BF16)
    out = xs + _rms(_dot(y_scr[...], wout_ref[...]), g_ref[1:2, :])
    o_ref[...] = jnp.swapaxes(out.reshape(n, SUBLANES, d), 0, 1).reshape(r, d)


def _rg_sample_kernel(proj_ref, prev_ref, h0_ref, cw_ref, cb_ref, wax_ref, ba_ref, bx_ref, lam_ref,
                      y_ref, h_ref):
    r = proj_ref.shape[0]
    c = BRANCH
    g = r // SUBLANES
    u3 = proj_ref[:, :c].reshape(g, SUBLANES, c)
    prev3 = prev_ref[...].reshape(g, SUBLANES, c)
    a3, b3 = _rg_ab(u3, prev3, cw_ref, cb_ref, wax_ref, ba_ref, bx_ref, lam_ref)
    b3 = b3 + a3 * h0_ref[...].reshape(g, SUBLANES, c)
    _, h3 = _scan8(a3, b3)
    h = h3.reshape(r, c)
    h_ref[...] = h
    y_ref[...] = (h * jax.nn.silu(proj_ref[:, c:])).astype(BF16)


def _rg_weight_specs(p):
    return [_const_spec(p["conv_w"].shape), _const_spec(p["conv_b"].shape), _const_spec(p["w_ax"].shape),
            _const_spec(p["b_a"].shape), _const_spec(p["b_x"].shape), _const_spec(p["lam"].shape)]


def _rg_weights(p):
    return (p["conv_w"], p["conv_b"], p["w_ax"], p["b_a"], p["b_x"], p["lam"])


def _rg_layer_prompt(x, g, p, bsz, seq, tl):
    nt = seq // tl
    c = BRANCH
    d = x.shape[1]
    taps = CONV_W - 1
    x_spec = pl.BlockSpec((tl, d), lambda b, i: (b * nt + i, 0))
    return pl.pallas_call(
        _rg_layer_prompt_kernel,
        grid=(bsz, nt),
        in_specs=[x_spec, _const_spec(g.shape), _const_spec(p["w_in"].shape)] + _rg_weight_specs(p)
                 + [_const_spec(p["w_out"].shape)],
        out_specs=[x_spec, pl.BlockSpec((None, taps, SUBLANES, c), lambda b, i: (b, 0, 0, 0)),
                   pl.BlockSpec((None, SUBLANES, c), lambda b, i: (b, 0, 0))],
        out_shape=[jax.ShapeDtypeStruct(x.shape, F32), jax.ShapeDtypeStruct((bsz, taps, SUBLANES, c), F32),
                   jax.ShapeDtypeStruct((bsz, SUBLANES, c), F32)],
        scratch_shapes=[pltpu.VMEM((taps, SUBLANES, c), F32), pltpu.VMEM((tl, c), BF16),
                        pltpu.VMEM((SUBLANES, c), F32)],
        compiler_params=_params(2), name="rg_layer_prompt",
    )(x, g, p["w_in"], *_rg_weights(p), p["w_out"])


def _rg_core_sample(proj, prev8, h0pad, p, tm):
    t = proj.shape[0]
    c = BRANCH
    row = lambda w: pl.BlockSpec((tm, w), lambda i: (i, 0))
    return pl.pallas_call(
        _rg_sample_kernel,
        grid=(t // tm,),
        in_specs=[row(2 * c), row(c), row(c)] + _rg_weight_specs(p),
        out_specs=[row(c), row(c)],
        out_shape=[jax.ShapeDtypeStruct((t, c), BF16), jax.ShapeDtypeStruct((t, c), F32)],
        compiler_params=_params(1), name="rg_core_sample",
    )(proj, prev8, h0pad, *_rg_weights(p))


def _ssd_chunk(z, u3, prev3, dt_raw, s_read, s_write, cw_ref, cb_ref, dtb_ref, alog_ref, dexp_ref,
               ng_ref, acs_carry):
    q = z.shape[0]
    n = SSD_STATE
    xbc = jax.nn.silu(_conv8(u3, prev3, cw_ref[...], cb_ref[...]).reshape(q, SSD_CONV_DIM))
    xs = xbc[:, :BRANCH]
    bm = xbc[:, BRANCH:BRANCH + SSD_GROUPS * n]
    cm = xbc[:, BRANCH + SSD_GROUPS * n:]
    dt = jax.nn.softplus(dt_raw + dtb_ref[...])
    a = -jnp.exp(alog_ref[...])
    da3 = _cumsum8((dt * a).reshape(q // SUBLANES, SUBLANES, LANES))
    rows, carry = [], acs_carry
    for j in range(q // SUBLANES):
        blk = da3[j] + carry
        carry = jnp.broadcast_to(blk[SUBLANES - 1:, :], (SUBLANES, LANES))
        rows.append(blk)
    acs = jnp.concatenate(rows, axis=0) if len(rows) > 1 else rows[0]
    last = acs[q - 1:q, :]
    if q % LANES == 0:
        acs_t, dt_t = acs.T, dt.T
    else:
        pad = jnp.zeros((LANES - q, LANES), F32)
        acs_t = jnp.concatenate([acs, pad], axis=0).T[:, :q]
        dt_t = jnp.concatenate([dt, pad], axis=0).T[:, :q]
    causal = (lax.broadcasted_iota(jnp.int32, (q, q), 0) >= lax.broadcasted_iota(jnp.int32, (q, q), 1))
    lane = lax.broadcasted_iota(jnp.int32, (q, LANES), 1)
    srow = lax.broadcasted_iota(jnp.int32, (LANES, LANES), 0)
    half = SSD_HEAD_DIM
    y_pairs = []
    for g in range(SSD_GROUPS):
        bm_g = bm[:, g * n:(g + 1) * n]
        cm_g = cm[:, g * n:(g + 1) * n]
        cb_g = _dot_nt(cm_g.astype(BF16), bm_g.astype(BF16))
        for jp in range(2):
            pair = 2 * g + jp
            m_parts, cce_parts, bcw_parts, cds = [], [], [], []
            for h in (2 * pair, 2 * pair + 1):
                colb = jnp.broadcast_to(acs[:, h:h + 1], (q, LANES))
                dcol = jnp.broadcast_to(dt[:, h:h + 1], (q, LANES))
                seg = colb[:, :q] - acs_t[h:h + 1, :]
                decay = jnp.exp(jnp.where(causal, seg, -jnp.inf))
                m_parts.append(cb_g * decay * dt_t[h:h + 1, :])
                cce_parts.append(cm_g * jnp.exp(colb))
                lastb = last[:, h:h + 1]
                bcw_parts.append(bm_g * (dcol * jnp.exp(lastb - colb)))
                cds.append(jnp.exp(lastb))
            xs_pair = xs[:, pair * LANES:(pair + 1) * LANES]
            top = jnp.where(lane < half, xs_pair, 0.0)
            bot = jnp.where(lane >= half, xs_pair, 0.0)
            w = jnp.concatenate([top, bot], axis=0).astype(BF16)
            if q % LANES == 0:
                yd = _dot(jnp.concatenate(m_parts, axis=1).astype(BF16), w)
            else:
                yd = _dot(m_parts[0], top) + _dot(m_parts[1], bot)
            s_pair = s_read(pair)
            s_blk = jnp.concatenate([jnp.where(srow < half, s_pair, 0.0),
                                     jnp.where(srow >= half, s_pair, 0.0)], axis=1).astype(BF16)
            yo = _dot_nt(jnp.concatenate(cce_parts, axis=1).astype(BF16), s_blk)
            ds = _dot_tn(w, jnp.concatenate(bcw_parts, axis=0).astype(BF16))
            cd = jnp.where(srow < half, jnp.broadcast_to(cds[0], (LANES, LANES)),
                           jnp.broadcast_to(cds[1], (LANES, LANES)))
            s_write(pair, s_pair * cd + ds)
            y_pairs.append(yd + yo + dexp_ref[:, pair * LANES:(pair + 1) * LANES] * xs_pair)
    gw = BRANCH // SSD_GROUPS
    y_groups = []
    for g in range(SSD_GROUPS):
        yg = jnp.concatenate(y_pairs[2 * g:2 * g + 2], axis=1) * jax.nn.silu(z[:, g * gw:(g + 1) * gw])
        y_groups.append(yg * lax.rsqrt(jnp.mean(yg * yg, axis=-1, keepdims=True) + EPS))
    return jnp.concatenate(y_groups, axis=1) * ng_ref[...], carry


def _ssd_prompt_kernel(zx_ref, dt_ref, cw_ref, cb_ref, dtb_ref, alog_ref, dexp_ref, ng_ref,
                       y_ref, sout_ref, xbuf, s_scr):
    r = zx_ref.shape[0]
    q = SSD_CHUNK
    g = q // SUBLANES
    c = SSD_CONV_DIM

    @pl.when(pl.program_id(1) == 0)
    def _():
        xbuf[0:SUBLANES, :] = jnp.zeros((SUBLANES, c), F32)
        s_scr[...] = jnp.zeros_like(s_scr)

    xbuf[SUBLANES:, :] = zx_ref[:, BRANCH:]

    def s_read(pair):
        return s_scr[pair * LANES:(pair + 1) * LANES, :]

    def s_write(pair, val):
        s_scr[pair * LANES:(pair + 1) * LANES, :] = val

    for ch in range(r // q):
        rows = slice(ch * q, (ch + 1) * q)
        u3 = xbuf[SUBLANES + ch * q:SUBLANES + (ch + 1) * q, :].reshape(g, SUBLANES, c)
        prev3 = xbuf[ch * q:(ch + 1) * q, :].reshape(g, SUBLANES, c)
        y, _ = _ssd_chunk(zx_ref[rows, :BRANCH], u3, prev3, dt_ref[rows, :], s_read, s_write, cw_ref,
                          cb_ref, dtb_ref, alog_ref, dexp_ref, ng_ref, jnp.zeros((SUBLANES, LANES), F32))
        y_ref[rows, :] = y.astype(BF16)
    xbuf[0:SUBLANES, :] = xbuf[r:, :]

    @pl.when(pl.program_id(1) == pl.num_programs(1) - 1)
    def _():
        sout_ref[...] = s_scr[...]


def _ssd_sample_kernel(zx_ref, dt_ref, prev_ref, s0_ref, cw_ref, cb_ref, dtb_ref, alog_ref, dexp_ref,
                       ng_ref, y_ref, sout_ref, *, seq):
    c = SSD_CONV_DIM
    ys = []
    for n in range(zx_ref.shape[0] // seq):
        rows = slice(n * seq, (n + 1) * seq)
        u3 = zx_ref[rows, BRANCH:].reshape(1, seq, c)
        prev3 = prev_ref[rows, :].reshape(1, seq, c)

        def s_read(pair, n=n):
            return s0_ref[n, pair * LANES:(pair + 1) * LANES, :]

        def s_write(pair, val, n=n):
            sout_ref[n, pair * LANES:(pair + 1) * LANES, :] = val

        y, _ = _ssd_chunk(zx_ref[rows, :BRANCH], u3, prev3, dt_ref[rows, :], s_read, s_write, cw_ref,
                          cb_ref, dtb_ref, alog_ref, dexp_ref, ng_ref, jnp.zeros((SUBLANES, LANES), F32))
        ys.append(y)
    y_ref[...] = jnp.concatenate(ys, axis=0).astype(BF16)


def _ssd_weight_specs(p):
    return [_const_spec(p[k].shape) for k in ("conv_w", "conv_b", "dt_bias", "a_log", "d_exp", "norm_g")]


def _ssd_weights(p):
    return tuple(p[k] for k in ("conv_w", "conv_b", "dt_bias", "a_log", "d_exp", "norm_g"))


def _ssd_core_prompt(zx, dt, p, bsz, seq, q):
    nt = seq // q
    hp = SSD_HEADS * SSD_HEAD_DIM
    return pl.pallas_call(
        _ssd_prompt_kernel,
        grid=(bsz, nt),
        in_specs=[pl.BlockSpec((q, zx.shape[1]), lambda b, i: (b * nt + i, 0)),
                  pl.BlockSpec((q, LANES), lambda b, i: (b * nt + i, 0))] + _ssd_weight_specs(p),
        out_specs=[pl.BlockSpec((q, BRANCH), lambda b, i: (b * nt + i, 0)),
                   pl.BlockSpec((None, hp, SSD_STATE), lambda b, i: (b, 0, 0))],
        out_shape=[jax.ShapeDtypeStruct((bsz * seq, BRANCH), BF16),
                   jax.ShapeDtypeStruct((bsz, hp, SSD_STATE), F32)],
        scratch_shapes=[pltpu.VMEM((SUBLANES + q, SSD_CONV_DIM), F32), pltpu.VMEM((hp, SSD_STATE), F32)],
        compiler_params=_params(2), name="ssd_core_prompt",
    )(zx, dt, *_ssd_weights(p))


def _ssd_core_sample(zx, dt, prev8, s0, idx, p, seq, nb):
    t = zx.shape[0]
    bsz = t // seq
    hp = SSD_HEADS * SSD_HEAD_DIM
    row = lambda w: pl.BlockSpec((nb * seq, w), lambda i: (i, 0))
    st = pl.BlockSpec((nb, hp, SSD_STATE), lambda i: (i, 0, 0))
    st_in = pl.BlockSpec((None, nb, hp, SSD_STATE), lambda i: (idx, i, 0, 0))
    return pl.pallas_call(
        functools.partial(_ssd_sample_kernel, seq=seq),
        grid=(bsz // nb,),
        in_specs=[row(zx.shape[1]), row(LANES), row(SSD_CONV_DIM), st_in] + _ssd_weight_specs(p),
        out_specs=[row(BRANCH), st],
        out_shape=[jax.ShapeDtypeStruct((t, BRANCH), BF16), jax.ShapeDtypeStruct((bsz, hp, SSD_STATE), F32)],
        compiler_params=_params(1), name="ssd_core_sample",
    )(zx, dt, prev8, s0, *_ssd_weights(p))


def _hg_lower_bound(rows, layer):
    mx = functools.reduce(jnp.maximum, rows)
    es = [jnp.exp(x - mx) for x in rows]
    return sum(es[1:layer + 1]) / sum(es)


def _hg_prompt_kernel(proj_ref, hlb_ref, ng_ref, y_ref, sout_ref, st_scr, *, layer):
    r = proj_ref.shape[0]
    c = BRANCH
    dk = HGRN_KEY_DIM
    blk, sub = HGRN_BLOCK, HGRN_CHUNK
    nblk, nsub = r // blk, blk // sub

    @pl.when(pl.program_id(1) == 0)
    def _():
        st_scr[...] = jnp.zeros_like(st_scr)

    causal = (lax.broadcasted_iota(jnp.int32, (blk, blk), 0)
              >= lax.broadcasted_iota(jnp.int32, (blk, blk), 1))

    def head_body(h, carry):
        lanes = pl.ds(pl.multiple_of(h * dk, dk), dk)

        def proj_part(part):
            return proj_ref[:, pl.ds(pl.multiple_of(part * c + h * dk, dk), dk)]

        lb = _hg_lower_bound([hlb_ref[j:j + 1, lanes] for j in range(DEPTH)], layer)
        f = proj_part(1)
        forget = lb + (1.0 - lb) * jax.nn.sigmoid(f)
        k = ((1.0 - lb) * jax.nn.sigmoid(-f)).reshape(nblk, blk, dk)
        x = _cumsum8(jnp.log(forget).reshape(r // SUBLANES, SUBLANES, dk)).reshape(nblk, blk, dk)
        parts, carry_row = [], None
        for j in range(blk // SUBLANES):
            part = x[:, j * SUBLANES:(j + 1) * SUBLANES, :]
            if carry_row is not None:
                part = part + carry_row
            carry_row = jnp.broadcast_to(part[:, SUBLANES - 1:, :], part.shape)
            parts.append(part)
        gc = jnp.concatenate(parts, axis=1)
        ends = [gc[:, (i + 1) * sub - 1:(i + 1) * sub, :] for i in range(nsub)]
        starts = [jnp.zeros_like(ends[0])] + ends[:-1]
        spread = lambda rows_: jnp.concatenate(
            [jnp.broadcast_to(x_, (nblk, sub, dk)) for x_ in rows_], axis=1)
        b_prev, b_next = spread(starts), spread(ends)
        qi = jax.nn.silu(proj_part(0)).reshape(nblk, blk, dk) * jnp.exp(gc - b_prev)
        qc = qi * jnp.exp(b_prev)
        kd = k * jnp.exp(b_prev - gc)
        ke = kd * jnp.exp(b_next - b_prev)
        kend = ke * jnp.exp(ends[-1] - b_next)
        v = proj_part(2).reshape(nblk, blk, dk)

        st = st_scr[h]
        outs = []
        for b in range(nblk):
            qi_b = qi[b].astype(BF16)
            att_rows = []
            for i in range(nsub):
                keys = []
                for j in range(nsub):
                    rs = slice(j * sub, (j + 1) * sub)
                    if j == i:
                        keys.append(kd[b, rs])
                    elif j < i - 1:
                        keys.append(ke[b, rs] * jnp.exp(starts[i][b] - ends[j][b]))
                    else:
                        keys.append(ke[b, rs])
                keys = jnp.concatenate(keys, axis=0).astype(BF16)
                att_rows.append(_dot_nt(qi_b[i * sub:(i + 1) * sub], keys))
            att = jnp.where(causal, jnp.concatenate(att_rows, axis=0), 0.0).astype(BF16)
            vb = v[b].astype(BF16)
            outs.append(_dot(att, vb) + _dot_nt(qc[b].astype(BF16), st.astype(BF16)))
            st = st * jnp.exp(ends[-1][b]) + _dot_tn(vb, kend[b].astype(BF16))
        st_scr[h] = st
        o = jnp.concatenate(outs, axis=0)
        o = o * lax.rsqrt(jnp.mean(o * o, axis=-1, keepdims=True) + EPS)
        gate = jax.nn.silu(proj_part(3))
        y_ref[:, lanes] = (o * ng_ref[:, lanes] * gate).astype(BF16)
        return carry

    lax.fori_loop(0, HGRN_HEADS, head_body, 0, unroll=True)

    @pl.when(pl.program_id(1) == pl.num_programs(1) - 1)
    def _():
        for h in range(HGRN_HEADS):
            sout_ref[h] = st_scr[h].T


def _hg_seq_kernel(proj_ref, hlb_ref, ng_ref, s0_ref, y_ref, sout_ref,
                   qg_scr, kg_scr, ke_scr, v_scr, dec_scr, o_scr, *, layer, chunk):
    r = proj_ref.shape[0]
    c = BRANCH
    g = r // SUBLANES
    n_chunks = r // chunk
    dk, dv = HGRN_KEY_DIM, HGRN_VAL_DIM
    assert chunk == SUBLANES

    lb = _hg_lower_bound([hlb_ref[j:j + 1, :] for j in range(DEPTH)], layer)
    f = proj_ref[:, c:2 * c]
    forget = lb + (1.0 - lb) * jax.nn.sigmoid(f)
    k = (1.0 - lb) * jax.nn.sigmoid(-f)
    gcum3 = _cumsum8(jnp.log(forget).reshape(g, SUBLANES, c))
    gcum = gcum3.reshape(r, c)
    glast = jnp.broadcast_to(gcum3[:, SUBLANES - 1:, :], gcum3.shape).reshape(r, c)
    qg_scr[...] = jax.nn.silu(proj_ref[:, :c]) * jnp.exp(gcum)
    kg_scr[...] = k * jnp.exp(-gcum)
    ke_scr[...] = k * jnp.exp(glast - gcum)
    dec_scr[...] = jnp.exp(glast)
    v_scr[...] = proj_ref[:, 2 * c:3 * c]

    causal = (lax.broadcasted_iota(jnp.int32, (chunk, chunk), 0)
              >= lax.broadcasted_iota(jnp.int32, (chunk, chunk), 1))

    def chunk_body(ci, carry):
        rs = pl.ds(pl.multiple_of(ci * chunk, chunk), chunk)
        outs = []
        for h in range(HGRN_HEADS):
            ks = slice(h * dk, (h + 1) * dk)
            vs = slice(h * dv, (h + 1) * dv)
            qg = qg_scr[rs, ks].astype(BF16)
            vv = v_scr[rs, vs].astype(BF16)
            st = s0_ref[ci, h].T
            att = jnp.where(causal, _dot_nt(qg, kg_scr[rs, ks].astype(BF16)), 0.0)
            outs.append(_dot(att.astype(BF16), vv) + _dot_nt(qg, st.astype(BF16)))
            dec = dec_scr[rs, ks][chunk - 1:, :]
            sout_ref[ci, h] = (st * dec + _dot_tn(vv, ke_scr[rs, ks].astype(BF16))).T
        o_scr[rs, :] = jnp.concatenate(outs, axis=1)
        return carry

    lax.fori_loop(0, n_chunks, chunk_body, 0, unroll=2)

    gate = jax.nn.silu(proj_ref[:, 3 * c:])
    parts = []
    for h in range(HGRN_HEADS):
        o = o_scr[:, h * dv:(h + 1) * dv]
        parts.append(o * lax.rsqrt(jnp.mean(o * o, axis=-1, keepdims=True) + EPS))
    y_ref[...] = (jnp.concatenate(parts, axis=1) * ng_ref[...] * gate).astype(BF16)


def _hg_core_prompt(proj, hlb, ng, layer, bsz, seq, tl):
    nt = seq // tl
    c = BRANCH
    st_shape = (HGRN_HEADS, HGRN_KEY_DIM, HGRN_VAL_DIM)
    return pl.pallas_call(
        functools.partial(_hg_prompt_kernel, layer=layer),
        grid=(bsz, nt),
        in_specs=[pl.BlockSpec((tl, 4 * c), lambda b, i: (b * nt + i, 0)),
                  _const_spec(hlb.shape), _const_spec(ng.shape)],
        out_specs=[pl.BlockSpec((tl, c), lambda b, i: (b * nt + i, 0)),
                   pl.BlockSpec((None,) + st_shape, lambda b, i: (b, 0, 0, 0))],
        out_shape=[jax.ShapeDtypeStruct((bsz * seq, c), BF16),
                   jax.ShapeDtypeStruct((bsz,) + st_shape, F32)],
        scratch_shapes=[pltpu.VMEM((HGRN_HEADS, HGRN_VAL_DIM, HGRN_KEY_DIM), F32)],
        compiler_params=_params(2), name="hg_core_prompt",
    )(proj, hlb, ng)


def _hg_core_sample(proj, s0, idx, hlb, ng, layer, seq, nb):
    t = proj.shape[0]
    bsz = t // seq
    c = BRANCH
    rows = nb * seq
    st_shape = (nb, HGRN_HEADS, HGRN_KEY_DIM, HGRN_VAL_DIM)
    st_spec = pl.BlockSpec(st_shape, lambda i: (i, 0, 0, 0))
    st_in = pl.BlockSpec((None,) + st_shape, lambda i: (idx, i, 0, 0, 0))
    return pl.pallas_call(
        functools.partial(_hg_seq_kernel, layer=layer, chunk=seq),
        grid=(bsz // nb,),
        in_specs=[pl.BlockSpec((rows, 4 * c), lambda i: (i, 0)),
                  _const_spec(hlb.shape), _const_spec(ng.shape), st_in],
        out_specs=[pl.BlockSpec((rows, c), lambda i: (i, 0)), st_spec],
        out_shape=[jax.ShapeDtypeStruct((t, c), BF16), jax.ShapeDtypeStruct(s0.shape[1:], F32)],
        scratch_shapes=[pltpu.VMEM((rows, c), F32)] * 6,
        compiler_params=_params(1), name="hg_core_sample",
    )(proj, hlb, ng, s0)


def _tile(n, target):
    t = min(n, target)
    assert n % t == 0, (n, target)
    return t


def _pad_groups(state, first_row):
    n, k, c = state.shape
    return jnp.pad(state, ((0, 0), (first_row, SUBLANES - first_row - k), (0, 0))).reshape(n * SUBLANES, c)


def _trunk(x, mem_k, mem_v, states, w, bsz, seq, prompt):
    tm = _tile(x.shape[0], 256)
    rg_conv, rg_h, ssd_conv, ssd_s, hg_s = [], [], [], [], []
    tail = slice(seq - (CONV_W - 1), seq)
    for layer in range(DEPTH):
        kind, idx = layer % N_MIXERS, layer // N_MIXERS
        g = w["norm_g"][layer]
        if kind == 0:
            p = w["rg"][idx]
            if prompt:
                x, utail, htail = _rg_layer_prompt(x, g, p, bsz, seq, _tile(seq, 512))
                rg_h.append(htail[:, SUBLANES - 1])
                rg_conv.append(utail[:, :, 0])
                y = None
            else:
                (proj,) = _norm_matmul(x, g[0:1], [p["w_in"]], tm)
                prev8 = _pad_groups(states["rg_conv"][idx], SUBLANES - (CONV_W - 1))
                h0pad = _pad_groups(states["rg_h"][idx][:, None, :], 0)
                y, h = _rg_core_sample(proj, prev8, h0pad, p, tm)
                rg_h.append(h.reshape(bsz, seq, BRANCH)[:, seq - 1])
                rg_conv.append(proj.reshape(bsz, seq, 2 * BRANCH)[:, tail, :BRANCH])
        elif kind == 1:
            p = w["ssd"][idx]
            zx, dt = _norm_matmul(x, g[0:1], [p["w_zx"], p["w_dt"]], tm)
            if prompt:
                y, s_new = _ssd_core_prompt(zx, dt, p, bsz, seq, _tile(seq, 2 * SSD_CHUNK))
            else:
                prev8 = _pad_groups(states["ssd_conv"][idx], SUBLANES - (CONV_W - 1))
                s0 = states["ssd_s"].reshape(-1, bsz, SSD_HEADS * SSD_HEAD_DIM, SSD_STATE)
                y, s_new = _ssd_core_sample(zx, dt, prev8, s0, idx, p, seq, 4)
            ssd_s.append(s_new.reshape(bsz, SSD_HEADS, SSD_HEAD_DIM, SSD_STATE))
            ssd_conv.append(zx.reshape(bsz, seq, BRANCH + SSD_CONV_DIM)[:, tail, BRANCH:])
        else:
            p = w["hg"][idx]
            (proj,) = _norm_matmul(x, g[0:1], [p["w_in"]], tm)
            if prompt:
                y, s_new = _hg_core_prompt(proj, w["hg_lower_bounds"], p["norm_g"], layer, bsz, seq,
                                           _tile(seq, 256))
            else:
                y, s_new = _hg_core_sample(proj, states["hg_s"], idx, w["hg_lower_bounds"], p["norm_g"],
                                           layer, seq, 8)
            hg_s.append(s_new)
        if y is not None:
            x = _proj_norm_res(y, p["w_out"], g[1:2], x, _tile(x.shape[0], 512))
        if prompt:
            x = _attn_prompt(x, mem_k, mem_v, layer, w["x_w_q"][layer], w["x_w_o"][layer], g,
                             bsz, seq, _tile(seq, 1024))
        else:
            x = _attn_sample(x, mem_k, mem_v, layer, w["x_w_q"][layer], w["x_w_o"][layer], g,
                             seq, 4)
    return x, jnp.stack(rg_conv), jnp.stack(rg_h), jnp.stack(ssd_conv), jnp.stack(ssd_s), jnp.stack(hg_s)


def kernel(x_prompt, x_sample, mem_prompt, state_rglru_conv, state_rglru_h, state_ssd_conv, state_ssd,
           state_hgrn, cache_mem_k, cache_mem_v, norm_g, mem_norm_g, rg_w_in, rg_conv_w, rg_conv_b, rg_w_a,
           rg_b_a, rg_w_x, rg_b_x, rg_lambda, rg_w_out, ssd_w_in, ssd_conv_w, ssd_conv_b, ssd_dt_bias,
           ssd_a_log, ssd_d, ssd_norm_g, ssd_w_out, hg_w_in, hg_lower_bounds, hg_norm_g, hg_w_out,
           x_w_q, x_w_k, x_w_v, x_w_o):
    bp, sp, d = x_prompt.shape
    bs, ss, _ = x_sample.shape
    n_a, n_b, n_c = rg_w_in.shape[0], ssd_w_in.shape[0], hg_w_in.shape[0]
    pad_heads = lambda v: jnp.pad(v, (0, LANES - SSD_HEADS))[None, :]
    w = {
        "norm_g": norm_g,
        "hg_lower_bounds": hg_lower_bounds,
        "x_w_q": x_w_q.astype(BF16),
        "x_w_o": x_w_o.astype(BF16),
        "rg": [{
            "w_in": rg_w_in[i].astype(BF16),
            "conv_w": rg_conv_w[i], "conv_b": rg_conv_b[i][None, :],
            "w_ax": jnp.concatenate([rg_w_a[i], rg_w_x[i]], axis=-1).astype(BF16),
            "b_a": rg_b_a[i][None, :], "b_x": rg_b_x[i][None, :], "lam": rg_lambda[i][None, :],
            "w_out": rg_w_out[i].astype(BF16),
        } for i in range(n_a)],
        "ssd": [{
            "w_zx": ssd_w_in[i][:, :BRANCH + SSD_CONV_DIM].astype(BF16),
            "w_dt": jnp.pad(ssd_w_in[i][:, BRANCH + SSD_CONV_DIM:], ((0, 0), (0, LANES - SSD_HEADS))).astype(BF16),
            "conv_w": ssd_conv_w[i], "conv_b": ssd_conv_b[i][None, :],
            "dt_bias": pad_heads(ssd_dt_bias[i]), "a_log": pad_heads(ssd_a_log[i]),
            "d_exp": jnp.repeat(ssd_d[i], SSD_HEAD_DIM)[None, :],
            "norm_g": ssd_norm_g[i][None, :],
            "w_out": ssd_w_out[i].astype(BF16),
        } for i in range(n_b)],
        "hg": [{
            "w_in": hg_w_in[i].astype(BF16),
            "norm_g": hg_norm_g[i][None, :],
            "w_out": hg_w_out[i].astype(BF16),
        } for i in range(n_c)],
    }

    mem_k_p, mem_v_p = _mem_kv(mem_prompt.reshape(bp * N_MEM, d), mem_norm_g[:, None, :],
                               x_w_k.astype(BF16), x_w_v.astype(BF16), _tile(bp, 2))
    y_p, rgc_p, rgh_p, sc_p, ss_p, hs_p = _trunk(
        x_prompt.reshape(bp * sp, d), mem_k_p, mem_v_p, None, w, bp, sp, True)
    states = {"rg_conv": state_rglru_conv, "rg_h": state_rglru_h, "ssd_conv": state_ssd_conv,
              "ssd_s": state_ssd, "hg_s": state_hgrn}
    y_s, rgc_s, rgh_s, sc_s, ss_s, hs_s = _trunk(
        x_sample.reshape(bs * ss, d), _kv_flat(cache_mem_k), _kv_flat(cache_mem_v), states, w, bs, ss, False)
    return (y_p.reshape(bp, sp, d), y_s.reshape(bs, ss, d), rgc_p, rgh_p, sc_p, ss_p, hs_p,
            _kv_unflat(mem_k_p), _kv_unflat(mem_v_p), rgc_s, rgh_s, sc_s, ss_s, hs_s)
```

```python
import functools

import jax
import jax.numpy as jnp
from jax import lax
from jax.experimental import pallas as pl
from jax.experimental.pallas import tpu as pltpu

F32 = jnp.float32
BF16 = jnp.bfloat16

D_MODEL = 1024
DEPTH = 4
N_MIXERS = 3
BRANCH = 2 * D_MODEL
CONV_W = 4
EPS = 1e-6
LRU_BLOCKS = 8
LRU_BLOCK = BRANCH // LRU_BLOCKS
LRU_C = 8.0
SSD_HEAD_DIM = 64
SSD_HEADS = BRANCH // SSD_HEAD_DIM
SSD_STATE = 128
SSD_GROUPS = 8
SSD_CONV_DIM = BRANCH + 2 * SSD_GROUPS * SSD_STATE
SSD_CHUNK = 128
HGRN_KEY_DIM = 128
HGRN_HEADS = BRANCH // HGRN_KEY_DIM
HGRN_VAL_DIM = BRANCH // HGRN_HEADS
HGRN_CHUNK = 16
HGRN_BLOCK = 4 * HGRN_CHUNK
N_MEM = 256
X_HEADS = 4
X_HEAD_DIM = D_MODEL // X_HEADS

SUBLANES = 8
LANES = 128
VMEM_BYTES_V7X = 64 * 1024 * 1024
VMEM_LIMIT = VMEM_BYTES_V7X * 7 // 8

NT_DIMS = (((1,), (1,)), ((), ()))
TN_DIMS = (((0,), (0,)), ((), ()))


def _params(n_grid_dims):
    return pltpu.CompilerParams(
        dimension_semantics=("arbitrary",) * n_grid_dims, vmem_limit_bytes=VMEM_LIMIT)


def _const_spec(shape):
    nd = len(shape)
    return pl.BlockSpec(shape, lambda *_: (0,) * nd, pipeline_mode=pl.Buffered(1))


def _rms(x, g):
    return x * lax.rsqrt(jnp.mean(x * x, axis=-1, keepdims=True) + EPS) * g


def _dot(a, b):
    return jnp.dot(a, b, preferred_element_type=F32)


def _dot_nt(a, b):
    return lax.dot_general(a, b, NT_DIMS, preferred_element_type=F32)


def _dot_tn(a, b):
    return lax.dot_general(a, b, TN_DIMS, preferred_element_type=F32)


def _group_iota(width):
    return lax.broadcasted_iota(jnp.int32, (1, SUBLANES, width), 1)


def _conv8(u3, prev3, cw, cb):
    t = _group_iota(u3.shape[-1])
    acc = cb + cw[CONV_W - 1:CONV_W, :] * u3
    for k in range(1, CONV_W):
        shifted = pltpu.roll(jnp.where(t >= SUBLANES - k, prev3, u3), k, 1)
        acc = acc + cw[CONV_W - 1 - k:CONV_W - k, :] * shifted
    return acc


def _scan8(a3, b3):
    t = _group_iota(a3.shape[-1])
    for s in (1, 2, 4):
        m = t >= s
        a_sh = pltpu.roll(a3, s, 1)
        b_sh = pltpu.roll(b3, s, 1)
        b3 = jnp.where(m, a3 * b_sh + b3, b3)
        a3 = jnp.where(m, a3 * a_sh, a3)
    return a3, b3


def _cumsum8(x3):
    t = _group_iota(x3.shape[-1])
    for s in (1, 2, 4):
        x3 = x3 + jnp.where(t >= s, pltpu.roll(x3, s, 1), 0.0)
    return x3


def _norm_matmul_kernel(x_ref, g_ref, *refs, n_chunk):
    n_w = len(refs) // 2
    h = _rms(x_ref[...], g_ref[...]).astype(BF16)
    for w_ref, o_ref in zip(refs[:n_w], refs[n_w:]):
        n = o_ref.shape[-1]
        step = min(n_chunk, n)
        for c in range(0, n, step):
            o_ref[:, c:c + step] = _dot(h, w_ref[:, c:c + step])


def _norm_matmul(x, g, ws, tm):
    t, d = x.shape
    grid = (t // tm,)
    in_specs = [pl.BlockSpec((tm, d), lambda i: (i, 0)), _const_spec((1, d))]
    in_specs += [_const_spec(w.shape) for w in ws]
    out_specs = [pl.BlockSpec((tm, w.shape[1]), lambda i: (i, 0)) for w in ws]
    out_shape = [jax.ShapeDtypeStruct((t, w.shape[1]), F32) for w in ws]
    return pl.pallas_call(
        functools.partial(_norm_matmul_kernel, n_chunk=512),
        grid=grid, in_specs=in_specs, out_specs=out_specs, out_shape=out_shape,
        compiler_params=_params(1), name="norm_matmul",
    )(x, g, *ws)


def _proj_norm_res_kernel(a_ref, w_ref, g_ref, x_ref, o_ref):
    y = _dot(a_ref[...], w_ref[...])
    o_ref[...] = x_ref[...] + _rms(y, g_ref[...])


def _proj_norm_res(a, w, g, x, tm):
    t, k = a.shape
    d = x.shape[1]
    return pl.pallas_call(
        _proj_norm_res_kernel,
        grid=(t // tm,),
        in_specs=[pl.BlockSpec((tm, k), lambda i: (i, 0)), _const_spec(w.shape), _const_spec((1, d)),
                  pl.BlockSpec((tm, d), lambda i: (i, 0))],
        out_specs=pl.BlockSpec((tm, d), lambda i: (i, 0)),
        out_shape=jax.ShapeDtypeStruct((t, d), F32),
        compiler_params=_params(1), name="proj_norm_res",
    )(a, w, g, x)


KV_LANE_TILES = X_HEAD_DIM // LANES
KV_ROWS = KV_LANE_TILES * X_HEADS


def _kv_flat(kv):
    lead = kv.shape[:-3]
    x = kv.reshape(lead + (N_MEM, X_HEADS, KV_LANE_TILES, LANES))
    return jnp.swapaxes(x, -3, -2).reshape(lead + (N_MEM * KV_ROWS, LANES))


def _kv_unflat(flat):
    lead = flat.shape[:-2]
    x = flat.reshape(lead + (N_MEM, KV_LANE_TILES, X_HEADS, LANES))
    return jnp.swapaxes(x, -3, -2).reshape(lead + (N_MEM, X_HEADS, X_HEAD_DIM))


def _mem_kv_kernel(m_ref, g_ref, wk_ref, wv_ref, k_ref, v_ref):
    h = _rms(m_ref[...], g_ref[...]).astype(BF16)
    tm = m_ref.shape[0]
    for w_ref, o_ref in ((wk_ref, k_ref), (wv_ref, v_ref)):
        y = _dot(h, w_ref[...])
        pieces = [y[:, hd * X_HEAD_DIM + t * LANES:hd * X_HEAD_DIM + (t + 1) * LANES][None]
                  for t in range(KV_LANE_TILES) for hd in range(X_HEADS)]
        rows = jnp.swapaxes(jnp.concatenate(pieces, axis=0), 0, 1)
        o_ref[...] = rows.reshape(o_ref.shape)


def _mem_kv(mem, g, wk, wv, nb):
    t, d = mem.shape
    bsz = t // N_MEM
    tm = nb * N_MEM
    w_spec = pl.BlockSpec((None, d, d), lambda l, i: (l, 0, 0))
    o_spec = pl.BlockSpec((None, nb, N_MEM * KV_ROWS, LANES), lambda l, i: (l, i, 0, 0))
    return pl.pallas_call(
        _mem_kv_kernel,
        grid=(DEPTH, bsz // nb),
        in_specs=[pl.BlockSpec((tm, d), lambda l, i: (i, 0)),
                  pl.BlockSpec((None, 1, d), lambda l, i: (l, 0, 0)), w_spec, w_spec],
        out_specs=[o_spec, o_spec],
        out_shape=[jax.ShapeDtypeStruct((DEPTH, bsz, N_MEM * KV_ROWS, LANES), F32)] * 2,
        compiler_params=_params(2), name="mem_kv",
    )(mem, g, wk, wv)


def _attn_kernel(x_ref, k_ref, v_ref, wq_ref, wo_ref, g_ref, o_ref, q_scr, a_scr, k_scr, v_scr, *,
                 rows, hoist, seq_rows):
    i = pl.program_id(0) if hoist else None
    scale = X_HEAD_DIM ** -0.5
    n_keys = k_scr.shape[1]

    def split_heads():
        def every_kv_row(ref, first):
            rows = pl.ds(first, N_MEM, stride=KV_ROWS)
            if len(ref.shape) == 2:
                return ref[rows, :]
            return ref[:, rows, :].reshape(n_keys, LANES)

        for ref, scr in ((k_ref, k_scr), (v_ref, v_scr)):
            for h in range(X_HEADS):
                scr[h] = jnp.concatenate([every_kv_row(ref, t * X_HEADS + h) for t in range(KV_LANE_TILES)],
                                         axis=1).astype(BF16)

    if hoist:
        split_heads()
    else:
        pl.when(pl.program_id(1) == 0)(split_heads)

    def project_q():
        h = _rms(x_ref[...], g_ref[2:3, :]).astype(BF16)
        q_scr[...] = (_dot(h, wq_ref[...]) * scale).astype(BF16)

    def project_out():
        y = _dot(a_scr[...], wo_ref[...])
        o_ref[...] = x_ref[...] + _rms(y, g_ref[3:4, :])

    if hoist:
        pl.when(i == 0)(project_q)
        r0 = pl.multiple_of(i * rows, rows)
        rsl = pl.ds(r0, rows)
    else:
        project_q()
        rsl = slice(None)

    if seq_rows is not None:
        qi = lax.broadcasted_iota(jnp.int32, (rows, n_keys), 0) // seq_rows
        ki = lax.broadcasted_iota(jnp.int32, (rows, n_keys), 1) // N_MEM
        mask = qi == ki
    for h in range(X_HEADS):
        hs = slice(h * X_HEAD_DIM, (h + 1) * X_HEAD_DIM)
        qh = q_scr[rsl, hs]
        s = _dot_nt(qh, k_scr[h])
        if seq_rows is not None:
            s = jnp.where(mask, s, -jnp.inf)
        e = jnp.exp(s - jnp.max(s, axis=-1, keepdims=True))
        p = e / jnp.sum(e, axis=-1, keepdims=True)
        a_scr[rsl, hs] = _dot(p.astype(BF16), v_scr[h]).astype(BF16)

    if hoist:
        pl.when(i == pl.num_programs(0) - 1)(project_out)
    else:
        project_out()


def _attn_prompt(x, k, v, layer, wq, wo, g, bsz, seq, tl):
    d = x.shape[1]
    nt = seq // tl
    kv_spec = pl.BlockSpec((None, None, N_MEM * KV_ROWS, LANES), lambda b, i: (layer, b, 0, 0))
    kv_scr = pltpu.VMEM((X_HEADS, N_MEM, X_HEAD_DIM), BF16)
    return pl.pallas_call(
        functools.partial(_attn_kernel, rows=tl, hoist=False, seq_rows=None),
        grid=(bsz, nt),
        in_specs=[pl.BlockSpec((tl, d), lambda b, i: (b * nt + i, 0)), kv_spec, kv_spec,
                  _const_spec(wq.shape), _const_spec(wo.shape), _const_spec(g.shape)],
        out_specs=pl.BlockSpec((tl, d), lambda b, i: (b * nt + i, 0)),
        out_shape=jax.ShapeDtypeStruct(x.shape, F32),
        scratch_shapes=[pltpu.VMEM((tl, d), BF16), pltpu.VMEM((tl, d), BF16), kv_scr, kv_scr],
        compiler_params=_params(2), name="attn_prompt",
    )(x, k, v, wq, wo, g)


def _attn_sample(x, k, v, layer, wq, wo, g, seq, nb):
    t, d = x.shape
    bsz = t // seq
    kv_spec = pl.BlockSpec((None, nb, N_MEM * KV_ROWS, LANES), lambda i: (layer, i, 0, 0))
    kv_scr = pltpu.VMEM((X_HEADS, nb * N_MEM, X_HEAD_DIM), BF16)
    return pl.pallas_call(
        functools.partial(_attn_kernel, rows=nb * seq, hoist=True, seq_rows=seq),
        grid=(bsz // nb,),
        in_specs=[_const_spec(x.shape), kv_spec, kv_spec,
                  _const_spec(wq.shape), _const_spec(wo.shape), _const_spec(g.shape)],
        out_specs=pl.BlockSpec(x.shape, lambda i: (0, 0)),
        out_shape=jax.ShapeDtypeStruct(x.shape, F32),
        scratch_shapes=[pltpu.VMEM((t, d), BF16), pltpu.VMEM((t, d), BF16), kv_scr, kv_scr],
        compiler_params=_params(1), name="attn_sample",
    )(x, k, v, wq, wo, g)


def _rg_gates(conv, blk, wax_ref, ba_ref, bx_ref, lam_ref):
    sl = slice(blk * LRU_BLOCK, (blk + 1) * LRU_BLOCK)
    pre = _dot(conv.astype(BF16), wax_ref[blk])
    rg = jax.nn.sigmoid(pre[:, :LRU_BLOCK] + ba_ref[:, sl])
    ig = jax.nn.sigmoid(pre[:, LRU_BLOCK:] + bx_ref[:, sl])
    a = jnp.exp(rg * (-LRU_C * jax.nn.softplus(-lam_ref[:, sl])))
    b = jnp.exp(0.5 * jnp.log(1.0 - a * a)) * (ig * conv)
    return a, b


def _rg_block_ab(u3, prev3, blk, cw_ref, cb_ref, wax_ref, ba_ref, bx_ref, lam_ref):
    g = u3.shape[0]
    sl = slice(blk * LRU_BLOCK, (blk + 1) * LRU_BLOCK)
    conv = _conv8(u3, prev3, cw_ref[:, sl], cb_ref[:, sl]).reshape(g * SUBLANES, LRU_BLOCK)
    a, b = _rg_gates(conv, blk, wax_ref, ba_ref, bx_ref, lam_ref)
    return a.reshape(g, SUBLANES, LRU_BLOCK), b.reshape(g, SUBLANES, LRU_BLOCK)


def _rg_ab(u3, prev3, cw_ref, cb_ref, wax_ref, ba_ref, bx_ref, lam_ref):
    parts = [_rg_block_ab(u3[:, :, blk * LRU_BLOCK:(blk + 1) * LRU_BLOCK],
                          prev3[:, :, blk * LRU_BLOCK:(blk + 1) * LRU_BLOCK],
                          blk, cw_ref, cb_ref, wax_ref, ba_ref, bx_ref, lam_ref)
             for blk in range(LRU_BLOCKS)]
    return (jnp.concatenate([p[0] for p in parts], axis=2), jnp.concatenate([p[1] for p in parts], axis=2))


def _rg_layer_prompt_kernel(x_ref, g_ref, win_ref, cw_ref, cb_ref, wax_ref, ba_ref, bx_ref, lam_ref,
                            wout_ref, o_ref, utail_ref, htail_ref, tail_scr, y_scr, hc):
    r, d = x_ref.shape
    c = BRANCH
    n = r // SUBLANES
    taps = CONV_W - 1

    @pl.when(pl.program_id(1) == 0)
    def _():
        tail_scr[...] = jnp.zeros_like(tail_scr)
        hc[...] = jnp.zeros_like(hc)

    xs = jnp.swapaxes(x_ref[...].reshape(SUBLANES, n, d), 0, 1).reshape(r, d)
    xn = _rms(xs, g_ref[0:1, :]).astype(BF16)
    first_seg = lax.broadcasted_iota(jnp.int32, (SUBLANES, LRU_BLOCK), 0) == 0
    for blk in range(LRU_BLOCKS):
        sl = slice(blk * LRU_BLOCK, (blk + 1) * LRU_BLOCK)
        u3 = _dot(xn, win_ref[:, sl]).reshape(n, SUBLANES, LRU_BLOCK)
        gate = _dot(xn, win_ref[:, c + blk * LRU_BLOCK:c + (blk + 1) * LRU_BLOCK])
        hist = []
        for j in range(taps):
            prev_seg = pltpu.roll(u3[n - taps + j], 1, 0)
            hist.append(jnp.where(first_seg, tail_scr[j, :, sl], prev_seg)[None])
            tail_scr[j, :, sl] = prev_seg
            utail_ref[j, :, sl] = prev_seg
        uext = jnp.concatenate(hist + [u3], axis=0)
        conv = cb_ref[:, sl] + cw_ref[taps:CONV_W, sl] * u3
        for k in range(1, CONV_W):
            conv = conv + cw_ref[taps - k:CONV_W - k, sl] * uext[taps - k:taps - k + n]
        a, b = _rg_gates(conv.reshape(r, LRU_BLOCK), blk, wax_ref, ba_ref, bx_ref, lam_ref)
        a3 = a.reshape(n, SUBLANES, LRU_BLOCK)
        b3 = b.reshape(n, SUBLANES, LRU_BLOCK)
        hs, prods = [b3[0]], [a3[0]]
        for s in range(1, n):
            hs.append(a3[s] * hs[-1] + b3[s])
            prods.append(a3[s] * prods[-1])
        h_in = hc[0:1, sl]
        carry_rows = []
        for q in range(SUBLANES):
            carry_rows.append(h_in)
            h_in = hs[-1][q:q + 1] + prods[-1][q:q + 1] * h_in
        carry = jnp.concatenate(carry_rows, axis=0)
        h_last = jnp.broadcast_to(h_in, (SUBLANES, LRU_BLOCK))
        hc[:, sl] = h_last
        htail_ref[:, sl] = h_last
        h = jnp.concatenate([(hs[s] + prods[s] * carry)[None] for s in range(n)], axis=0)
        y_scr[:, sl] = (h.reshape(r, LRU_BLOCK) * jax.nn.silu(gate)).astype(BF16)
    out = xs + _rms(_dot(y_scr[...], wout_ref[...]), g_ref[1:2, :])
    o_ref[...] = jnp.swapaxes(out.reshape(n, SUBLANES, d), 0, 1).reshape(r, d)


def _rg_sample_kernel(proj_ref, prev_ref, h0_ref, cw_ref, cb_ref, wax_ref, ba_ref, bx_ref, lam_ref,
                      y_ref, h_ref):
    r = proj_ref.shape[0]
    c = BRANCH
    g = r // SUBLANES
    u3 = proj_ref[:, :c].reshape(g, SUBLANES, c)
    prev3 = prev_ref[...].reshape(g, SUBLANES, c)
    a3, b3 = _rg_ab(u3, prev3, cw_ref, cb_ref, wax_ref, ba_ref, bx_ref, lam_ref)
    b3 = b3 + a3 * h0_ref[...].reshape(g, SUBLANES, c)
    _, h3 = _scan8(a3, b3)
    h = h3.reshape(r, c)
    h_ref[...] = h
    y_ref[...] = (h * jax.nn.silu(proj_ref[:, c:])).astype(BF16)


def _rg_weight_specs(p):
    return [_const_spec(p["conv_w"].shape), _const_spec(p["conv_b"].shape), _const_spec(p["w_ax"].shape),
            _const_spec(p["b_a"].shape), _const_spec(p["b_x"].shape), _const_spec(p["lam"].shape)]


def _rg_weights(p):
    return (p["conv_w"], p["conv_b"], p["w_ax"], p["b_a"], p["b_x"], p["lam"])


def _rg_layer_prompt(x, g, p, bsz, seq, tl):
    nt = seq // tl
    c = BRANCH
    d = x.shape[1]
    taps = CONV_W - 1
    x_spec = pl.BlockSpec((tl, d), lambda b, i: (b * nt + i, 0))
    return pl.pallas_call(
        _rg_layer_prompt_kernel,
        grid=(bsz, nt),
        in_specs=[x_spec, _const_spec(g.shape), _const_spec(p["w_in"].shape)] + _rg_weight_specs(p)
                 + [_const_spec(p["w_out"].shape)],
        out_specs=[x_spec, pl.BlockSpec((None, taps, SUBLANES, c), lambda b, i: (b, 0, 0, 0)),
                   pl.BlockSpec((None, SUBLANES, c), lambda b, i: (b, 0, 0))],
        out_shape=[jax.ShapeDtypeStruct(x.shape, F32), jax.ShapeDtypeStruct((bsz, taps, SUBLANES, c), F32),
                   jax.ShapeDtypeStruct((bsz, SUBLANES, c), F32)],
        scratch_shapes=[pltpu.VMEM((taps, SUBLANES, c), F32), pltpu.VMEM((tl, c), BF16),
                        pltpu.VMEM((SUBLANES, c), F32)],
        compiler_params=_params(2), name="rg_layer_prompt",
    )(x, g, p["w_in"], *_rg_weights(p), p["w_out"])


def _rg_core_sample(proj, prev8, h0pad, p, tm):
    t = proj.shape[0]
    c = BRANCH
    row = lambda w: pl.BlockSpec((tm, w), lambda i: (i, 0))
    return pl.pallas_call(
        _rg_sample_kernel,
        grid=(t // tm,),
        in_specs=[row(2 * c), row(c), row(c)] + _rg_weight_specs(p),
        out_specs=[row(c), row(c)],
        out_shape=[jax.ShapeDtypeStruct((t, c), BF16), jax.ShapeDtypeStruct((t, c), F32)],
        compiler_params=_params(1), name="rg_core_sample",
    )(proj, prev8, h0pad, *_rg_weights(p))


def _ssd_chunk(z, u3, prev3, dt_raw, s_read, s_write, cw_ref, cb_ref, dtb_ref, alog_ref, dexp_ref,
               ng_ref, acs_carry):
    q = z.shape[0]
    n = SSD_STATE
    xbc = jax.nn.silu(_conv8(u3, prev3, cw_ref[...], cb_ref[...]).reshape(q, SSD_CONV_DIM))
    xs = xbc[:, :BRANCH]
    bm = xbc[:, BRANCH:BRANCH + SSD_GROUPS * n]
    cm = xbc[:, BRANCH + SSD_GROUPS * n:]
    dt = jax.nn.softplus(dt_raw + dtb_ref[...])
    a = -jnp.exp(alog_ref[...])
    da3 = _cumsum8((dt * a).reshape(q // SUBLANES, SUBLANES, LANES))
    rows, carry = [], acs_carry
    for j in range(q // SUBLANES):
        blk = da3[j] + carry
        carry = jnp.broadcast_to(blk[SUBLANES - 1:, :], (SUBLANES, LANES))
        rows.append(blk)
    acs = jnp.concatenate(rows, axis=0) if len(rows) > 1 else rows[0]
    last = acs[q - 1:q, :]
    if q % LANES == 0:
        acs_t, dt_t = acs.T, dt.T
    else:
        pad = jnp.zeros((LANES - q, LANES), F32)
        acs_t = jnp.concatenate([acs, pad], axis=0).T[:, :q]
        dt_t = jnp.concatenate([dt, pad], axis=0).T[:, :q]
    causal = (lax.broadcasted_iota(jnp.int32, (q, q), 0) >= lax.broadcasted_iota(jnp.int32, (q, q), 1))
    lane = lax.broadcasted_iota(jnp.int32, (q, LANES), 1)
    srow = lax.broadcasted_iota(jnp.int32, (LANES, LANES), 0)
    half = SSD_HEAD_DIM
    y_pairs = []
    for g in range(SSD_GROUPS):
        bm_g = bm[:, g * n:(g + 1) * n]
        cm_g = cm[:, g * n:(g + 1) * n]
        cb_g = _dot_nt(cm_g.astype(BF16), bm_g.astype(BF16))
        for jp in range(2):
            pair = 2 * g + jp
            m_parts, cce_parts, bcw_parts, cds = [], [], [], []
            for h in (2 * pair, 2 * pair + 1):
                colb = jnp.broadcast_to(acs[:, h:h + 1], (q, LANES))
                dcol = jnp.broadcast_to(dt[:, h:h + 1], (q, LANES))
                seg = colb[:, :q] - acs_t[h:h + 1, :]
                decay = jnp.exp(jnp.where(causal, seg, -jnp.inf))
                m_parts.append(cb_g * decay * dt_t[h:h + 1, :])
                cce_parts.append(cm_g * jnp.exp(colb))
                lastb = last[:, h:h + 1]
                bcw_parts.append(bm_g * (dcol * jnp.exp(lastb - colb)))
                cds.append(jnp.exp(lastb))
            xs_pair = xs[:, pair * LANES:(pair + 1) * LANES]
            top = jnp.where(lane < half, xs_pair, 0.0)
            bot = jnp.where(lane >= half, xs_pair, 0.0)
            w = jnp.concatenate([top, bot], axis=0).astype(BF16)
            if q % LANES == 0:
                yd = _dot(jnp.concatenate(m_parts, axis=1).astype(BF16), w)
            else:
                yd = _dot(m_parts[0], top) + _dot(m_parts[1], bot)
            s_pair = s_read(pair)
            s_blk = jnp.concatenate([jnp.where(srow < half, s_pair, 0.0),
                                     jnp.where(srow >= half, s_pair, 0.0)], axis=1).astype(BF16)
            yo = _dot_nt(jnp.concatenate(cce_parts, axis=1).astype(BF16), s_blk)
            ds = _dot_tn(w, jnp.concatenate(bcw_parts, axis=0).astype(BF16))
            cd = jnp.where(srow < half, jnp.broadcast_to(cds[0], (LANES, LANES)),
                           jnp.broadcast_to(cds[1], (LANES, LANES)))
            s_write(pair, s_pair * cd + ds)
            y_pairs.append(yd + yo + dexp_ref[:, pair * LANES:(pair + 1) * LANES] * xs_pair)
    gw = BRANCH // SSD_GROUPS
    y_groups = []
    for g in range(SSD_GROUPS):
        yg = jnp.concatenate(y_pairs[2 * g:2 * g + 2], axis=1) * jax.nn.silu(z[:, g * gw:(g + 1) * gw])
        y_groups.append(yg * lax.rsqrt(jnp.mean(yg * yg, axis=-1, keepdims=True) + EPS))
    return jnp.concatenate(y_groups, axis=1) * ng_ref[...], carry


def _ssd_prompt_kernel(zx_ref, dt_ref, cw_ref, cb_ref, dtb_ref, alog_ref, dexp_ref, ng_ref,
                       y_ref, sout_ref, xbuf, s_scr):
    r = zx_ref.shape[0]
    q = SSD_CHUNK
    g = q // SUBLANES
    c = SSD_CONV_DIM

    @pl.when(pl.program_id(1) == 0)
    def _():
        xbuf[0:SUBLANES, :] = jnp.zeros((SUBLANES, c), F32)
        s_scr[...] = jnp.zeros_like(s_scr)

    xbuf[SUBLANES:, :] = zx_ref[:, BRANCH:]

    def s_read(pair):
        return s_scr[pair * LANES:(pair + 1) * LANES, :]

    def s_write(pair, val):
        s_scr[pair * LANES:(pair + 1) * LANES, :] = val

    for ch in range(r // q):
        rows = slice(ch * q, (ch + 1) * q)
        u3 = xbuf[SUBLANES + ch * q:SUBLANES + (ch + 1) * q, :].reshape(g, SUBLANES, c)
        prev3 = xbuf[ch * q:(ch + 1) * q, :].reshape(g, SUBLANES, c)
        y, _ = _ssd_chunk(zx_ref[rows, :BRANCH], u3, prev3, dt_ref[rows, :], s_read, s_write, cw_ref,
                          cb_ref, dtb_ref, alog_ref, dexp_ref, ng_ref, jnp.zeros((SUBLANES, LANES), F32))
        y_ref[rows, :] = y.astype(BF16)
    xbuf[0:SUBLANES, :] = xbuf[r:, :]

    @pl.when(pl.program_id(1) == pl.num_programs(1) - 1)
    def _():
        sout_ref[...] = s_scr[...]


def _ssd_sample_kernel(zx_ref, dt_ref, prev_ref, s0_ref, cw_ref, cb_ref, dtb_ref, alog_ref, dexp_ref,
                       ng_ref, y_ref, sout_ref, *, seq):
    c = SSD_CONV_DIM
    ys = []
    for n in range(zx_ref.shape[0] // seq):
        rows = slice(n * seq, (n + 1) * seq)
        u3 = zx_ref[rows, BRANCH:].reshape(1, seq, c)
        prev3 = prev_ref[rows, :].reshape(1, seq, c)

        def s_read(pair, n=n):
            return s0_ref[n, pair * LANES:(pair + 1) * LANES, :]

        def s_write(pair, val, n=n):
            sout_ref[n, pair * LANES:(pair + 1) * LANES, :] = val

        y, _ = _ssd_chunk(zx_ref[rows, :BRANCH], u3, prev3, dt_ref[rows, :], s_read, s_write, cw_ref,
                          cb_ref, dtb_ref, alog_ref, dexp_ref, ng_ref, jnp.zeros((SUBLANES, LANES), F32))
        ys.append(y)
    y_ref[...] = jnp.concatenate(ys, axis=0).astype(BF16)


def _ssd_weight_specs(p):
    return [_const_spec(p[k].shape) for k in ("conv_w", "conv_b", "dt_bias", "a_log", "d_exp", "norm_g")]


def _ssd_weights(p):
    return tuple(p[k] for k in ("conv_w", "conv_b", "dt_bias", "a_log", "d_exp", "norm_g"))


def _ssd_core_prompt(zx, dt, p, bsz, seq, q):
    nt = seq // q
    hp = SSD_HEADS * SSD_HEAD_DIM
    return pl.pallas_call(
        _ssd_prompt_kernel,
        grid=(bsz, nt),
        in_specs=[pl.BlockSpec((q, zx.shape[1]), lambda b, i: (b * nt + i, 0)),
                  pl.BlockSpec((q, LANES), lambda b, i: (b * nt + i, 0))] + _ssd_weight_specs(p),
        out_specs=[pl.BlockSpec((q, BRANCH), lambda b, i: (b * nt + i, 0)),
                   pl.BlockSpec((None, hp, SSD_STATE), lambda b, i: (b, 0, 0))],
        out_shape=[jax.ShapeDtypeStruct((bsz * seq, BRANCH), BF16),
                   jax.ShapeDtypeStruct((bsz, hp, SSD_STATE), F32)],
        scratch_shapes=[pltpu.VMEM((SUBLANES + q, SSD_CONV_DIM), F32), pltpu.VMEM((hp, SSD_STATE), F32)],
        compiler_params=_params(2), name="ssd_core_prompt",
    )(zx, dt, *_ssd_weights(p))


def _ssd_core_sample(zx, dt, prev8, s0, idx, p, seq, nb):
    t = zx.shape[0]
    bsz = t // seq
    hp = SSD_HEADS * SSD_HEAD_DIM
    row = lambda w: pl.BlockSpec((nb * seq, w), lambda i: (i, 0))
    st = pl.BlockSpec((nb, hp, SSD_STATE), lambda i: (i, 0, 0))
    st_in = pl.BlockSpec((None, nb, hp, SSD_STATE), lambda i: (idx, i, 0, 0))
    return pl.pallas_call(
        functools.partial(_ssd_sample_kernel, seq=seq),
        grid=(bsz // nb,),
        in_specs=[row(zx.shape[1]), row(LANES), row(SSD_CONV_DIM), st_in] + _ssd_weight_specs(p),
        out_specs=[row(BRANCH), st],
        out_shape=[jax.ShapeDtypeStruct((t, BRANCH), BF16), jax.ShapeDtypeStruct((bsz, hp, SSD_STATE), F32)],
        compiler_params=_params(1), name="ssd_core_sample",
    )(zx, dt, prev8, s0, *_ssd_weights(p))


def _hg_lower_bound(rows, layer):
    mx = functools.reduce(jnp.maximum, rows)
    es = [jnp.exp(x - mx) for x in rows]
    return sum(es[1:layer + 1]) / sum(es)


def _hg_prompt_kernel(proj_ref, hlb_ref, ng_ref, y_ref, sout_ref, st_scr, *, layer):
    r = proj_ref.shape[0]
    c = BRANCH
    dk = HGRN_KEY_DIM
    blk, sub = HGRN_BLOCK, HGRN_CHUNK
    nblk, nsub = r // blk, blk // sub

    @pl.when(pl.program_id(1) == 0)
    def _():
        st_scr[...] = jnp.zeros_like(st_scr)

    causal = (lax.broadcasted_iota(jnp.int32, (blk, blk), 0)
              >= lax.broadcasted_iota(jnp.int32, (blk, blk), 1))

    def head_body(h, carry):
        lanes = pl.ds(pl.multiple_of(h * dk, dk), dk)

        def proj_part(part):
            return proj_ref[:, pl.ds(pl.multiple_of(part * c + h * dk, dk), dk)]

        lb = _hg_lower_bound([hlb_ref[j:j + 1, lanes] for j in range(DEPTH)], layer)
        f = proj_part(1)
        sig = jax.nn.sigmoid(f)
        forget = lb + (1.0 - lb) * sig
        k = ((1.0 - lb) * (1.0 - sig)).reshape(nblk, blk, dk)
        x = _cumsum8(jnp.log(forget).reshape(r // SUBLANES, SUBLANES, dk)).reshape(nblk, blk, dk)
        parts, carry_row = [], None
        for j in range(blk // SUBLANES):
            part = x[:, j * SUBLANES:(j + 1) * SUBLANES, :]
            if carry_row is not None:
                part = part + carry_row
            carry_row = jnp.broadcast_to(part[:, SUBLANES - 1:, :], part.shape)
            parts.append(part)
        gc = jnp.concatenate(parts, axis=1)
        ends = [gc[:, (i + 1) * sub - 1:(i + 1) * sub, :] for i in range(nsub)]
        starts = [jnp.zeros_like(ends[0])] + ends[:-1]
        spread = lambda rows_: jnp.concatenate(
            [jnp.broadcast_to(x_, (nblk, sub, dk)) for x_ in rows_], axis=1)
        from_start = gc - spread(starts)
        qi = jax.nn.silu(proj_part(0)).reshape(nblk, blk, dk) * jnp.exp(from_start)
        qc = qi * spread([jnp.exp(s_) for s_ in starts])
        kd = k * jnp.exp(-from_start)
        ke = kd * spread([jnp.exp(e_ - s_) for s_, e_ in zip(starts, ends)])
        kend = ke * spread([jnp.exp(ends[-1] - e_) for e_ in ends])
        v = proj_part(2).reshape(nblk, blk, dk)

        st = st_scr[h]
        outs = []
        for b in range(nblk):
            qi_b = qi[b].astype(BF16)
            att_rows = []
            for i in range(nsub):
                keys = []
                for j in range(nsub):
                    rs = slice(j * sub, (j + 1) * sub)
                    if j == i:
                        keys.append(kd[b, rs])
                    elif j < i - 1:
                        keys.append(ke[b, rs] * jnp.exp(starts[i][b] - ends[j][b]))
                    else:
                        keys.append(ke[b, rs])
                keys = jnp.concatenate(keys, axis=0).astype(BF16)
                att_rows.append(_dot_nt(qi_b[i * sub:(i + 1) * sub], keys))
            att = jnp.where(causal, jnp.concatenate(att_rows, axis=0), 0.0).astype(BF16)
            vb = v[b].astype(BF16)
            outs.append(_dot(att, vb) + _dot_nt(qc[b].astype(BF16), st.astype(BF16)))
            st = st * jnp.exp(ends[-1][b]) + _dot_tn(vb, kend[b].astype(BF16))
        st_scr[h] = st
        o = jnp.concatenate(outs, axis=0)
        o = o * lax.rsqrt(jnp.mean(o * o, axis=-1, keepdims=True) + EPS)
        gate = jax.nn.silu(proj_part(3))
        y_ref[:, lanes] = (o * ng_ref[:, lanes] * gate).astype(BF16)
        return carry

    lax.fori_loop(0, HGRN_HEADS, head_body, 0, unroll=True)

    @pl.when(pl.program_id(1) == pl.num_programs(1) - 1)
    def _():
        for h in range(HGRN_HEADS):
            sout_ref[h] = st_scr[h].T


def _hg_seq_kernel(proj_ref, hlb_ref, ng_ref, s0_ref, y_ref, sout_ref,
                   qg_scr, kg_scr, ke_scr, v_scr, dec_scr, o_scr, *, layer, chunk):
    r = proj_ref.shape[0]
    c = BRANCH
    g = r // SUBLANES
    n_chunks = r // chunk
    dk, dv = HGRN_KEY_DIM, HGRN_VAL_DIM
    assert chunk == SUBLANES

    lb = _hg_lower_bound([hlb_ref[j:j + 1, :] for j in range(DEPTH)], layer)
    f = proj_ref[:, c:2 * c]
    sig = jax.nn.sigmoid(f)
    forget = lb + (1.0 - lb) * sig
    k = (1.0 - lb) * (1.0 - sig)
    gcum3 = _cumsum8(jnp.log(forget).reshape(g, SUBLANES, c))
    gcum = gcum3.reshape(r, c)
    glast = jnp.broadcast_to(gcum3[:, SUBLANES - 1:, :], gcum3.shape).reshape(r, c)
    qg_scr[...] = jax.nn.silu(proj_ref[:, :c]) * jnp.exp(gcum)
    kg_scr[...] = k * jnp.exp(-gcum)
    ke_scr[...] = k * jnp.exp(glast - gcum)
    dec_scr[...] = jnp.exp(glast)
    v_scr[...] = proj_ref[:, 2 * c:3 * c]

    causal = (lax.broadcasted_iota(jnp.int32, (chunk, chunk), 0)
              >= lax.broadcasted_iota(jnp.int32, (chunk, chunk), 1))

    def chunk_body(ci, carry):
        rs = pl.ds(pl.multiple_of(ci * chunk, chunk), chunk)
        outs = []
        for h in range(HGRN_HEADS):
            ks = slice(h * dk, (h + 1) * dk)
            vs = slice(h * dv, (h + 1) * dv)
            qg = qg_scr[rs, ks].astype(BF16)
            vv = v_scr[rs, vs].astype(BF16)
            st = s0_ref[ci, h].T
            att = jnp.where(causal, _dot_nt(qg, kg_scr[rs, ks].astype(BF16)), 0.0)
            outs.append(_dot(att.astype(BF16), vv) + _dot_nt(qg, st.astype(BF16)))
            dec = dec_scr[rs, ks][chunk - 1:, :]
            sout_ref[ci, h] = (st * dec + _dot_tn(vv, ke_scr[rs, ks].astype(BF16))).T
        o_scr[rs, :] = jnp.concatenate(outs, axis=1)
        return carry

    lax.fori_loop(0, n_chunks, chunk_body, 0, unroll=2)

    gate = jax.nn.silu(proj_ref[:, 3 * c:])
    parts = []
    for h in range(HGRN_HEADS):
        o = o_scr[:, h * dv:(h + 1) * dv]
        parts.append(o * lax.rsqrt(jnp.mean(o * o, axis=-1, keepdims=True) + EPS))
    y_ref[...] = (jnp.concatenate(parts, axis=1) * ng_ref[...] * gate).astype(BF16)


def _hg_core_prompt(proj, hlb, ng, layer, bsz, seq, tl):
    nt = seq // tl
    c = BRANCH
    st_shape = (HGRN_HEADS, HGRN_KEY_DIM, HGRN_VAL_DIM)
    return pl.pallas_call(
        functools.partial(_hg_prompt_kernel, layer=layer),
        grid=(bsz, nt),
        in_specs=[pl.BlockSpec((tl, 4 * c), lambda b, i: (b * nt + i, 0)),
                  _const_spec(hlb.shape), _const_spec(ng.shape)],
        out_specs=[pl.BlockSpec((tl, c), lambda b, i: (b * nt + i, 0)),
                   pl.BlockSpec((None,) + st_shape, lambda b, i: (b, 0, 0, 0))],
        out_shape=[jax.ShapeDtypeStruct((bsz * seq, c), BF16),
                   jax.ShapeDtypeStruct((bsz,) + st_shape, F32)],
        scratch_shapes=[pltpu.VMEM((HGRN_HEADS, HGRN_VAL_DIM, HGRN_KEY_DIM), F32)],
        compiler_params=_params(2), name="hg_core_prompt",
    )(proj, hlb, ng)


def _hg_core_sample(proj, s0, idx, hlb, ng, layer, seq, nb):
    t = proj.shape[0]
    bsz = t // seq
    c = BRANCH
    rows = nb * seq
    st_shape = (nb, HGRN_HEADS, HGRN_KEY_DIM, HGRN_VAL_DIM)
    st_spec = pl.BlockSpec(st_shape, lambda i: (i, 0, 0, 0))
    st_in = pl.BlockSpec((None,) + st_shape, lambda i: (idx, i, 0, 0, 0))
    return pl.pallas_call(
        functools.partial(_hg_seq_kernel, layer=layer, chunk=seq),
        grid=(bsz // nb,),
        in_specs=[pl.BlockSpec((rows, 4 * c), lambda i: (i, 0)),
                  _const_spec(hlb.shape), _const_spec(ng.shape), st_in],
        out_specs=[pl.BlockSpec((rows, c), lambda i: (i, 0)), st_spec],
        out_shape=[jax.ShapeDtypeStruct((t, c), BF16), jax.ShapeDtypeStruct(s0.shape[1:], F32)],
        scratch_shapes=[pltpu.VMEM((rows, c), F32)] * 6,
        compiler_params=_params(1), name="hg_core_sample",
    )(proj, hlb, ng, s0)


ROWS_PROJ_IN = 256
ROWS_PROJ_OUT = 512
ROWS_RG_PROMPT = 512
ROWS_SSD_PROMPT = 2 * SSD_CHUNK
ROWS_HG_PROMPT = 4 * HGRN_BLOCK
ROWS_ATTN_PROMPT = 1024
SEQS_SSD_SAMPLE = 4
SEQS_HG_SAMPLE = 8
SEQS_ATTN_SAMPLE = 4
BATCH_MEM_KV = 2


def _tile(n, target):
    t = min(n, target)
    assert n % t == 0, (n, target)
    return t


def _pad_groups(state, first_row):
    n, k, c = state.shape
    return jnp.pad(state, ((0, 0), (first_row, SUBLANES - first_row - k), (0, 0))).reshape(n * SUBLANES, c)


def _trunk(x, mem_k, mem_v, states, w, bsz, seq, prompt):
    tm = _tile(x.shape[0], ROWS_PROJ_IN)
    rg_conv, rg_h, ssd_conv, ssd_s, hg_s = [], [], [], [], []
    tail = slice(seq - (CONV_W - 1), seq)
    for layer in range(DEPTH):
        kind, idx = layer % N_MIXERS, layer // N_MIXERS
        g = w["norm_g"][layer]
        if kind == 0:
            p = w["rg"][idx]
            if prompt:
                x, utail, htail = _rg_layer_prompt(x, g, p, bsz, seq, _tile(seq, ROWS_RG_PROMPT))
                rg_h.append(htail[:, SUBLANES - 1])
                rg_conv.append(utail[:, :, 0])
                y = None
            else:
                (proj,) = _norm_matmul(x, g[0:1], [p["w_in"]], tm)
                prev8 = _pad_groups(states["rg_conv"][idx], SUBLANES - (CONV_W - 1))
                h0pad = _pad_groups(states["rg_h"][idx][:, None, :], 0)
                y, h = _rg_core_sample(proj, prev8, h0pad, p, tm)
                rg_h.append(h.reshape(bsz, seq, BRANCH)[:, seq - 1])
                rg_conv.append(proj.reshape(bsz, seq, 2 * BRANCH)[:, tail, :BRANCH])
        elif kind == 1:
            p = w["ssd"][idx]
            zx, dt = _norm_matmul(x, g[0:1], [p["w_zx"], p["w_dt"]], tm)
            if prompt:
                y, s_new = _ssd_core_prompt(zx, dt, p, bsz, seq, _tile(seq, ROWS_SSD_PROMPT))
            else:
                prev8 = _pad_groups(states["ssd_conv"][idx], SUBLANES - (CONV_W - 1))
                s0 = states["ssd_s"].reshape(-1, bsz, SSD_HEADS * SSD_HEAD_DIM, SSD_STATE)
                y, s_new = _ssd_core_sample(zx, dt, prev8, s0, idx, p, seq, _tile(bsz, SEQS_SSD_SAMPLE))
            ssd_s.append(s_new.reshape(bsz, SSD_HEADS, SSD_HEAD_DIM, SSD_STATE))
            ssd_conv.append(zx.reshape(bsz, seq, BRANCH + SSD_CONV_DIM)[:, tail, BRANCH:])
        else:
            p = w["hg"][idx]
            (proj,) = _norm_matmul(x, g[0:1], [p["w_in"]], tm)
            if prompt:
                y, s_new = _hg_core_prompt(proj, w["hg_lower_bounds"], p["norm_g"], layer, bsz, seq,
                                           _tile(seq, ROWS_HG_PROMPT))
            else:
                y, s_new = _hg_core_sample(proj, states["hg_s"], idx, w["hg_lower_bounds"], p["norm_g"],
                                           layer, seq, _tile(bsz, SEQS_HG_SAMPLE))
            hg_s.append(s_new)
        if y is not None:
            x = _proj_norm_res(y, p["w_out"], g[1:2], x, _tile(x.shape[0], ROWS_PROJ_OUT))
        if prompt:
            x = _attn_prompt(x, mem_k, mem_v, layer, w["x_w_q"][layer], w["x_w_o"][layer], g,
                             bsz, seq, _tile(seq, ROWS_ATTN_PROMPT))
        else:
            x = _attn_sample(x, mem_k, mem_v, layer, w["x_w_q"][layer], w["x_w_o"][layer], g,
                             seq, _tile(bsz, SEQS_ATTN_SAMPLE))
    return x, jnp.stack(rg_conv), jnp.stack(rg_h), jnp.stack(ssd_conv), jnp.stack(ssd_s), jnp.stack(hg_s)


def kernel(x_prompt, x_sample, mem_prompt, state_rglru_conv, state_rglru_h, state_ssd_conv, state_ssd,
           state_hgrn, cache_mem_k, cache_mem_v, norm_g, mem_norm_g, rg_w_in, rg_conv_w, rg_conv_b, rg_w_a,
           rg_b_a, rg_w_x, rg_b_x, rg_lambda, rg_w_out, ssd_w_in, ssd_conv_w, ssd_conv_b, ssd_dt_bias,
           ssd_a_log, ssd_d, ssd_norm_g, ssd_w_out, hg_w_in, hg_lower_bounds, hg_norm_g, hg_w_out,
           x_w_q, x_w_k, x_w_v, x_w_o):
    bp, sp, d = x_prompt.shape
    bs, ss, _ = x_sample.shape
    n_a, n_b, n_c = rg_w_in.shape[0], ssd_w_in.shape[0], hg_w_in.shape[0]
    pad_heads = lambda v: jnp.pad(v, (0, LANES - SSD_HEADS))[None, :]
    w = {
        "norm_g": norm_g,
        "hg_lower_bounds": hg_lower_bounds,
        "x_w_q": x_w_q.astype(BF16),
        "x_w_o": x_w_o.astype(BF16),
        "rg": [{
            "w_in": rg_w_in[i].astype(BF16),
            "conv_w": rg_conv_w[i], "conv_b": rg_conv_b[i][None, :],
            "w_ax": jnp.concatenate([rg_w_a[i], rg_w_x[i]], axis=-1).astype(BF16),
            "b_a": rg_b_a[i][None, :], "b_x": rg_b_x[i][None, :], "lam": rg_lambda[i][None, :],
            "w_out": rg_w_out[i].astype(BF16),
        } for i in range(n_a)],
        "ssd": [{
            "w_zx": ssd_w_in[i][:, :BRANCH + SSD_CONV_DIM].astype(BF16),
            "w_dt": jnp.pad(ssd_w_in[i][:, BRANCH + SSD_CONV_DIM:], ((0, 0), (0, LANES - SSD_HEADS))).astype(BF16),
            "conv_w": ssd_conv_w[i], "conv_b": ssd_conv_b[i][None, :],
            "dt_bias": pad_heads(ssd_dt_bias[i]), "a_log": pad_heads(ssd_a_log[i]),
            "d_exp": jnp.repeat(ssd_d[i], SSD_HEAD_DIM)[None, :],
            "norm_g": ssd_norm_g[i][None, :],
            "w_out": ssd_w_out[i].astype(BF16),
        } for i in range(n_b)],
        "hg": [{
            "w_in": hg_w_in[i].astype(BF16),
            "norm_g": hg_norm_g[i][None, :],
            "w_out": hg_w_out[i].astype(BF16),
        } for i in range(n_c)],
    }

    mem_k_p, mem_v_p = _mem_kv(mem_prompt.reshape(bp * N_MEM, d), mem_norm_g[:, None, :],
                               x_w_k.astype(BF16), x_w_v.astype(BF16), _tile(bp, BATCH_MEM_KV))
    y_p, rgc_p, rgh_p, sc_p, ss_p, hs_p = _trunk(
        x_prompt.reshape(bp * sp, d), mem_k_p, mem_v_p, None, w, bp, sp, True)
    states = {"rg_conv": state_rglru_conv, "rg_h": state_rglru_h, "ssd_conv": state_ssd_conv,
              "ssd_s": state_ssd, "hg_s": state_hgrn}
    y_s, rgc_s, rgh_s, sc_s, ss_s, hs_s = _trunk(
        x_sample.reshape(bs * ss, d), _kv_flat(cache_mem_k), _kv_flat(cache_mem_v), states, w, bs, ss, False)
    return (y_p.reshape(bp, sp, d), y_s.reshape(bs, ss, d), rgc_p, rgh_p, sc_p, ss_p, hs_p,
            _kv_unflat(mem_k_p), _kv_unflat(mem_v_p), rgc_s, rgh_s, sc_s, ss_s, hs_s)
```

```python
import functools

import jax
import jax.numpy as jnp
from jax import lax
from jax.experimental import pallas as pl
from jax.experimental.pallas import tpu as pltpu

F32 = jnp.float32
BF16 = jnp.bfloat16

D_MODEL = 1024
DEPTH = 4
N_MIXERS = 3
BRANCH = 2 * D_MODEL
CONV_W = 4
EPS = 1e-6
LRU_BLOCKS = 8
LRU_BLOCK = BRANCH // LRU_BLOCKS
LRU_C = 8.0
SSD_HEAD_DIM = 64
SSD_HEADS = BRANCH // SSD_HEAD_DIM
SSD_STATE = 128
SSD_GROUPS = 8
SSD_CONV_DIM = BRANCH + 2 * SSD_GROUPS * SSD_STATE
SSD_CHUNK = 128
HGRN_KEY_DIM = 128
HGRN_HEADS = BRANCH // HGRN_KEY_DIM
HGRN_VAL_DIM = BRANCH // HGRN_HEADS
HGRN_CHUNK = 16
HGRN_BLOCK = 4 * HGRN_CHUNK
N_MEM = 256
X_HEADS = 4
X_HEAD_DIM = D_MODEL // X_HEADS

SUBLANES = 8
LANES = 128
VMEM_BYTES_V7X = 64 * 1024 * 1024
VMEM_LIMIT = VMEM_BYTES_V7X * 7 // 8

NT_DIMS = (((1,), (1,)), ((), ()))
TN_DIMS = (((0,), (0,)), ((), ()))


def _params(n_grid_dims):
    return pltpu.CompilerParams(
        dimension_semantics=("arbitrary",) * n_grid_dims, vmem_limit_bytes=VMEM_LIMIT)


def _const_spec(shape):
    nd = len(shape)
    return pl.BlockSpec(shape, lambda *_: (0,) * nd, pipeline_mode=pl.Buffered(1))


def _rms(x, g):
    return x * lax.rsqrt(jnp.mean(x * x, axis=-1, keepdims=True) + EPS) * g


def _dot(a, b):
    return jnp.dot(a, b, preferred_element_type=F32)


def _dot_nt(a, b):
    return lax.dot_general(a, b, NT_DIMS, preferred_element_type=F32)


def _dot_tn(a, b):
    return lax.dot_general(a, b, TN_DIMS, preferred_element_type=F32)


def _group_iota(width):
    return lax.broadcasted_iota(jnp.int32, (1, SUBLANES, width), 1)


def _conv8(u3, prev3, cw, cb):
    t = _group_iota(u3.shape[-1])
    acc = cb + cw[CONV_W - 1:CONV_W, :] * u3
    for k in range(1, CONV_W):
        shifted = pltpu.roll(jnp.where(t >= SUBLANES - k, prev3, u3), k, 1)
        acc = acc + cw[CONV_W - 1 - k:CONV_W - k, :] * shifted
    return acc


def _scan8(a3, b3):
    t = _group_iota(a3.shape[-1])
    for s in (1, 2, 4):
        m = t >= s
        a_sh = pltpu.roll(a3, s, 1)
        b_sh = pltpu.roll(b3, s, 1)
        b3 = jnp.where(m, a3 * b_sh + b3, b3)
        a3 = jnp.where(m, a3 * a_sh, a3)
    return a3, b3


def _cumsum8(x3):
    t = _group_iota(x3.shape[-1])
    for s in (1, 2, 4):
        x3 = x3 + jnp.where(t >= s, pltpu.roll(x3, s, 1), 0.0)
    return x3


def _norm_matmul_kernel(x_ref, g_ref, *refs, n_chunk):
    n_w = len(refs) // 2
    h = _rms(x_ref[...], g_ref[...]).astype(BF16)
    for w_ref, o_ref in zip(refs[:n_w], refs[n_w:]):
        n = o_ref.shape[-1]
        step = min(n_chunk, n)
        for c in range(0, n, step):
            o_ref[:, c:c + step] = _dot(h, w_ref[:, c:c + step])


def _norm_matmul(x, g, ws, tm):
    t, d = x.shape
    grid = (t // tm,)
    in_specs = [pl.BlockSpec((tm, d), lambda i: (i, 0)), _const_spec((1, d))]
    in_specs += [_const_spec(w.shape) for w in ws]
    out_specs = [pl.BlockSpec((tm, w.shape[1]), lambda i: (i, 0)) for w in ws]
    out_shape = [jax.ShapeDtypeStruct((t, w.shape[1]), F32) for w in ws]
    return pl.pallas_call(
        functools.partial(_norm_matmul_kernel, n_chunk=512),
        grid=grid, in_specs=in_specs, out_specs=out_specs, out_shape=out_shape,
        compiler_params=_params(1), name="norm_matmul",
    )(x, g, *ws)


def _proj_norm_res_kernel(a_ref, w_ref, g_ref, x_ref, o_ref):
    y = _dot(a_ref[...], w_ref[...])
    o_ref[...] = x_ref[...] + _rms(y, g_ref[...])


def _proj_norm_res(a, w, g, x, tm):
    t, k = a.shape
    d = x.shape[1]
    return pl.pallas_call(
        _proj_norm_res_kernel,
        grid=(t // tm,),
        in_specs=[pl.BlockSpec((tm, k), lambda i: (i, 0)), _const_spec(w.shape), _const_spec((1, d)),
                  pl.BlockSpec((tm, d), lambda i: (i, 0))],
        out_specs=pl.BlockSpec((tm, d), lambda i: (i, 0)),
        out_shape=jax.ShapeDtypeStruct((t, d), F32),
        compiler_params=_params(1), name="proj_norm_res",
    )(a, w, g, x)


KV_LANE_TILES = X_HEAD_DIM // LANES
KV_ROWS = KV_LANE_TILES * X_HEADS


def _kv_flat(kv):
    lead = kv.shape[:-3]
    x = kv.reshape(lead + (N_MEM, X_HEADS, KV_LANE_TILES, LANES))
    return jnp.swapaxes(x, -3, -2).reshape(lead + (N_MEM * KV_ROWS, LANES))


def _kv_unflat(flat):
    lead = flat.shape[:-2]
    x = flat.reshape(lead + (N_MEM, KV_LANE_TILES, X_HEADS, LANES))
    return jnp.swapaxes(x, -3, -2).reshape(lead + (N_MEM, X_HEADS, X_HEAD_DIM))


def _mem_kv_kernel(m_ref, g_ref, wk_ref, wv_ref, k_ref, v_ref):
    h = _rms(m_ref[...], g_ref[...]).astype(BF16)
    tm = m_ref.shape[0]
    for w_ref, o_ref in ((wk_ref, k_ref), (wv_ref, v_ref)):
        y = _dot(h, w_ref[...])
        pieces = [y[:, hd * X_HEAD_DIM + t * LANES:hd * X_HEAD_DIM + (t + 1) * LANES][None]
                  for t in range(KV_LANE_TILES) for hd in range(X_HEADS)]
        rows = jnp.swapaxes(jnp.concatenate(pieces, axis=0), 0, 1)
        o_ref[...] = rows.reshape(o_ref.shape)


def _mem_kv(mem, g, wk, wv, nb):
    t, d = mem.shape
    bsz = t // N_MEM
    tm = nb * N_MEM
    w_spec = pl.BlockSpec((None, d, d), lambda l, i: (l, 0, 0))
    o_spec = pl.BlockSpec((None, nb, N_MEM * KV_ROWS, LANES), lambda l, i: (l, i, 0, 0))
    return pl.pallas_call(
        _mem_kv_kernel,
        grid=(DEPTH, bsz // nb),
        in_specs=[pl.BlockSpec((tm, d), lambda l, i: (i, 0)),
                  pl.BlockSpec((None, 1, d), lambda l, i: (l, 0, 0)), w_spec, w_spec],
        out_specs=[o_spec, o_spec],
        out_shape=[jax.ShapeDtypeStruct((DEPTH, bsz, N_MEM * KV_ROWS, LANES), F32)] * 2,
        compiler_params=_params(2), name="mem_kv",
    )(mem, g, wk, wv)


def _attn_kernel(x_ref, k_ref, v_ref, wq_ref, wo_ref, g_ref, o_ref, q_scr, a_scr, k_scr, v_scr, *,
                 rows, hoist, seq_rows):
    i = pl.program_id(0) if hoist else None
    scale = X_HEAD_DIM ** -0.5
    n_keys = k_scr.shape[1]

    def split_heads():
        def every_kv_row(ref, first):
            rows = pl.ds(first, N_MEM, stride=KV_ROWS)
            if len(ref.shape) == 2:
                return ref[rows, :]
            return ref[:, rows, :].reshape(n_keys, LANES)

        for ref, scr in ((k_ref, k_scr), (v_ref, v_scr)):
            for h in range(X_HEADS):
                scr[h] = jnp.concatenate([every_kv_row(ref, t * X_HEADS + h) for t in range(KV_LANE_TILES)],
                                         axis=1).astype(BF16)

    if hoist:
        split_heads()
    else:
        pl.when(pl.program_id(1) == 0)(split_heads)

    def project_q():
        h = _rms(x_ref[...], g_ref[2:3, :]).astype(BF16)
        q_scr[...] = (_dot(h, wq_ref[...]) * scale).astype(BF16)

    def project_out():
        y = _dot(a_scr[...], wo_ref[...])
        o_ref[...] = x_ref[...] + _rms(y, g_ref[3:4, :])

    if hoist:
        pl.when(i == 0)(project_q)
        r0 = pl.multiple_of(i * rows, rows)
        rsl = pl.ds(r0, rows)
    else:
        project_q()
        rsl = slice(None)

    if seq_rows is not None:
        qi = lax.broadcasted_iota(jnp.int32, (rows, n_keys), 0) // seq_rows
        ki = lax.broadcasted_iota(jnp.int32, (rows, n_keys), 1) // N_MEM
        mask = qi == ki
    for h in range(X_HEADS):
        hs = slice(h * X_HEAD_DIM, (h + 1) * X_HEAD_DIM)
        qh = q_scr[rsl, hs]
        s = _dot_nt(qh, k_scr[h])
        if seq_rows is not None:
            s = jnp.where(mask, s, -jnp.inf)
        e = jnp.exp(s - jnp.max(s, axis=-1, keepdims=True))
        p = e / jnp.sum(e, axis=-1, keepdims=True)
        a_scr[rsl, hs] = _dot(p.astype(BF16), v_scr[h]).astype(BF16)

    if hoist:
        pl.when(i == pl.num_programs(0) - 1)(project_out)
    else:
        project_out()


def _attn_prompt(x, k, v, layer, wq, wo, g, bsz, seq, tl):
    d = x.shape[1]
    nt = seq // tl
    kv_spec = pl.BlockSpec((None, None, N_MEM * KV_ROWS, LANES), lambda b, i: (layer, b, 0, 0))
    kv_scr = pltpu.VMEM((X_HEADS, N_MEM, X_HEAD_DIM), BF16)
    return pl.pallas_call(
        functools.partial(_attn_kernel, rows=tl, hoist=False, seq_rows=None),
        grid=(bsz, nt),
        in_specs=[pl.BlockSpec((tl, d), lambda b, i: (b * nt + i, 0)), kv_spec, kv_spec,
                  _const_spec(wq.shape), _const_spec(wo.shape), _const_spec(g.shape)],
        out_specs=pl.BlockSpec((tl, d), lambda b, i: (b * nt + i, 0)),
        out_shape=jax.ShapeDtypeStruct(x.shape, F32),
        scratch_shapes=[pltpu.VMEM((tl, d), BF16), pltpu.VMEM((tl, d), BF16), kv_scr, kv_scr],
        compiler_params=_params(2), name="attn_prompt",
    )(x, k, v, wq, wo, g)


def _attn_sample(x, k, v, layer, wq, wo, g, seq, nb):
    t, d = x.shape
    bsz = t // seq
    kv_spec = pl.BlockSpec((None, nb, N_MEM * KV_ROWS, LANES), lambda i: (layer, i, 0, 0))
    kv_scr = pltpu.VMEM((X_HEADS, nb * N_MEM, X_HEAD_DIM), BF16)
    return pl.pallas_call(
        functools.partial(_attn_kernel, rows=nb * seq, hoist=True, seq_rows=seq),
        grid=(bsz // nb,),
        in_specs=[_const_spec(x.shape), kv_spec, kv_spec,
                  _const_spec(wq.shape), _const_spec(wo.shape), _const_spec(g.shape)],
        out_specs=pl.BlockSpec(x.shape, lambda i: (0, 0)),
        out_shape=jax.ShapeDtypeStruct(x.shape, F32),
        scratch_shapes=[pltpu.VMEM((t, d), BF16), pltpu.VMEM((t, d), BF16), kv_scr, kv_scr],
        compiler_params=_params(1), name="attn_sample",
    )(x, k, v, wq, wo, g)


def _rg_gates(conv, blk, wax_ref, ba_ref, bx_ref, lam_ref):
    sl = slice(blk * LRU_BLOCK, (blk + 1) * LRU_BLOCK)
    pre = _dot(conv.astype(BF16), wax_ref[blk])
    rg = jax.nn.sigmoid(pre[:, :LRU_BLOCK] + ba_ref[:, sl])
    ig = jax.nn.sigmoid(pre[:, LRU_BLOCK:] + bx_ref[:, sl])
    a = jnp.exp(rg * (-LRU_C * jax.nn.softplus(-lam_ref[:, sl])))
    b = jnp.exp(0.5 * jnp.log(1.0 - a * a)) * (ig * conv)
    return a, b


def _rg_block_ab(u3, prev3, blk, cw_ref, cb_ref, wax_ref, ba_ref, bx_ref, lam_ref):
    g = u3.shape[0]
    sl = slice(blk * LRU_BLOCK, (blk + 1) * LRU_BLOCK)
    conv = _conv8(u3, prev3, cw_ref[:, sl], cb_ref[:, sl]).reshape(g * SUBLANES, LRU_BLOCK)
    a, b = _rg_gates(conv, blk, wax_ref, ba_ref, bx_ref, lam_ref)
    return a.reshape(g, SUBLANES, LRU_BLOCK), b.reshape(g, SUBLANES, LRU_BLOCK)


def _rg_ab(u3, prev3, cw_ref, cb_ref, wax_ref, ba_ref, bx_ref, lam_ref):
    parts = [_rg_block_ab(u3[:, :, blk * LRU_BLOCK:(blk + 1) * LRU_BLOCK],
                          prev3[:, :, blk * LRU_BLOCK:(blk + 1) * LRU_BLOCK],
                          blk, cw_ref, cb_ref, wax_ref, ba_ref, bx_ref, lam_ref)
             for blk in range(LRU_BLOCKS)]
    return (jnp.concatenate([p[0] for p in parts], axis=2), jnp.concatenate([p[1] for p in parts], axis=2))


def _rg_layer_prompt_kernel(x_ref, g_ref, win_ref, cw_ref, cb_ref, wax_ref, ba_ref, bx_ref, lam_ref,
                            wout_ref, o_ref, utail_ref, htail_ref, tail_scr, y_scr, hc):
    r, d = x_ref.shape
    c = BRANCH
    n = r // SUBLANES
    taps = CONV_W - 1

    @pl.when(pl.program_id(1) == 0)
    def _():
        tail_scr[...] = jnp.zeros_like(tail_scr)
        hc[...] = jnp.zeros_like(hc)

    xs = jnp.swapaxes(x_ref[...].reshape(SUBLANES, n, d), 0, 1).reshape(r, d)
    xn = _rms(xs, g_ref[0:1, :]).astype(BF16)
    first_seg = lax.broadcasted_iota(jnp.int32, (SUBLANES, LRU_BLOCK), 0) == 0
    for blk in range(LRU_BLOCKS):
        sl = slice(blk * LRU_BLOCK, (blk + 1) * LRU_BLOCK)
        u3 = _dot(xn, win_ref[:, sl]).reshape(n, SUBLANES, LRU_BLOCK)
        gate = _dot(xn, win_ref[:, c + blk * LRU_BLOCK:c + (blk + 1) * LRU_BLOCK])
        hist = []
        for j in range(taps):
            prev_seg = pltpu.roll(u3[n - taps + j], 1, 0)
            hist.append(jnp.where(first_seg, tail_scr[j, :, sl], prev_seg)[None])
            tail_scr[j, :, sl] = prev_seg
            utail_ref[j, :, sl] = prev_seg
        uext = jnp.concatenate(hist + [u3], axis=0)
        conv = cb_ref[:, sl] + cw_ref[taps:CONV_W, sl] * u3
        for k in range(1, CONV_W):
            conv = conv + cw_ref[taps - k:CONV_W - k, sl] * uext[taps - k:taps - k + n]
        a, b = _rg_gates(conv.reshape(r, LRU_BLOCK), blk, wax_ref, ba_ref, bx_ref, lam_ref)
        a3 = a.reshape(n, SUBLANES, LRU_BLOCK)
        b3 = b.reshape(n, SUBLANES, LRU_BLOCK)
        hs, prods = [b3[0]], [a3[0]]
        for s in range(1, n):
            hs.append(a3[s] * hs[-1] + b3[s])
            prods.append(a3[s] * prods[-1])
        h_in = hc[0:1, sl]
        carry_rows = []
        for q in range(SUBLANES):
            carry_rows.append(h_in)
            h_in = hs[-1][q:q + 1] + prods[-1][q:q + 1] * h_in
        carry = jnp.concatenate(carry_rows, axis=0)
        h_last = jnp.broadcast_to(h_in, (SUBLANES, LRU_BLOCK))
        hc[:, sl] = h_last
        htail_ref[:, sl] = h_last
        h = jnp.concatenate([(hs[s] + prods[s] * carry)[None] for s in range(n)], axis=0)
        y_scr[:, sl] = (h.reshape(r, LRU_BLOCK) * jax.nn.silu(gate)).astype(BF16)
    out = xs + _rms(_dot(y_scr[...], wout_ref[...]), g_ref[1:2, :])
    o_ref[...] = jnp.swapaxes(out.reshape(n, SUBLANES, d), 0, 1).reshape(r, d)


def _rg_sample_kernel(proj_ref, prev_ref, h0_ref, cw_ref, cb_ref, wax_ref, ba_ref, bx_ref, lam_ref,
                      y_ref, h_ref):
    r = proj_ref.shape[0]
    c = BRANCH
    g = r // SUBLANES
    u3 = proj_ref[:, :c].reshape(g, SUBLANES, c)
    prev3 = prev_ref[...].reshape(g, SUBLANES, c)
    a3, b3 = _rg_ab(u3, prev3, cw_ref, cb_ref, wax_ref, ba_ref, bx_ref, lam_ref)
    b3 = b3 + a3 * h0_ref[...].reshape(g, SUBLANES, c)
    _, h3 = _scan8(a3, b3)
    h = h3.reshape(r, c)
    h_ref[...] = h
    y_ref[...] = (h * jax.nn.silu(proj_ref[:, c:])).astype(BF16)


def _rg_weight_specs(p):
    return [_const_spec(p["conv_w"].shape), _const_spec(p["conv_b"].shape), _const_spec(p["w_ax"].shape),
            _const_spec(p["b_a"].shape), _const_spec(p["b_x"].shape), _const_spec(p["lam"].shape)]


def _rg_weights(p):
    return (p["conv_w"], p["conv_b"], p["w_ax"], p["b_a"], p["b_x"], p["lam"])


def _rg_layer_prompt(x, g, p, bsz, seq, tl):
    nt = seq // tl
    c = BRANCH
    d = x.shape[1]
    taps = CONV_W - 1
    x_spec = pl.BlockSpec((tl, d), lambda b, i: (b * nt + i, 0))
    return pl.pallas_call(
        _rg_layer_prompt_kernel,
        grid=(bsz, nt),
        in_specs=[x_spec, _const_spec(g.shape), _const_spec(p["w_in"].shape)] + _rg_weight_specs(p)
                 + [_const_spec(p["w_out"].shape)],
        out_specs=[x_spec, pl.BlockSpec((None, taps, SUBLANES, c), lambda b, i: (b, 0, 0, 0)),
                   pl.BlockSpec((None, SUBLANES, c), lambda b, i: (b, 0, 0))],
        out_shape=[jax.ShapeDtypeStruct(x.shape, F32), jax.ShapeDtypeStruct((bsz, taps, SUBLANES, c), F32),
                   jax.ShapeDtypeStruct((bsz, SUBLANES, c), F32)],
        scratch_shapes=[pltpu.VMEM((taps, SUBLANES, c), F32), pltpu.VMEM((tl, c), BF16),
                        pltpu.VMEM((SUBLANES, c), F32)],
        compiler_params=_params(2), name="rg_layer_prompt",
    )(x, g, p["w_in"], *_rg_weights(p), p["w_out"])


def _rg_core_sample(proj, prev8, h0pad, p, tm):
    t = proj.shape[0]
    c = BRANCH
    row = lambda w: pl.BlockSpec((tm, w), lambda i: (i, 0))
    return pl.pallas_call(
        _rg_sample_kernel,
        grid=(t // tm,),
        in_specs=[row(2 * c), row(c), row(c)] + _rg_weight_specs(p),
        out_specs=[row(c), row(c)],
        out_shape=[jax.ShapeDtypeStruct((t, c), BF16), jax.ShapeDtypeStruct((t, c), F32)],
        compiler_params=_params(1), name="rg_core_sample",
    )(proj, prev8, h0pad, *_rg_weights(p))


def _ssd_chunk(z, u3, prev3, dt_raw, s_read, s_write, cw_ref, cb_ref, dtb_ref, alog_ref, dexp_ref,
               ng_ref):
    q = z.shape[0]
    n = SSD_STATE
    xbc = jax.nn.silu(_conv8(u3, prev3, cw_ref[...], cb_ref[...]).reshape(q, SSD_CONV_DIM))
    xs = xbc[:, :BRANCH]
    bm = xbc[:, BRANCH:BRANCH + SSD_GROUPS * n]
    cm = xbc[:, BRANCH + SSD_GROUPS * n:]
    dt = jax.nn.softplus(dt_raw + dtb_ref[...])
    a = -jnp.exp(alog_ref[...])
    da3 = _cumsum8((dt * a).reshape(q // SUBLANES, SUBLANES, LANES))
    rows = [da3[0]]
    for j in range(1, q // SUBLANES):
        rows.append(da3[j] + jnp.broadcast_to(rows[-1][SUBLANES - 1:, :], (SUBLANES, LANES)))
    acs = jnp.concatenate(rows, axis=0) if len(rows) > 1 else rows[0]
    last = acs[q - 1:q, :]
    if q % LANES == 0:
        acs_t, dt_t = acs.T, dt.T
    else:
        pad = jnp.zeros((LANES - q, LANES), F32)
        acs_t = jnp.concatenate([acs, pad], axis=0).T[:, :q]
        dt_t = jnp.concatenate([dt, pad], axis=0).T[:, :q]
    causal = (lax.broadcasted_iota(jnp.int32, (q, q), 0) >= lax.broadcasted_iota(jnp.int32, (q, q), 1))
    lane = lax.broadcasted_iota(jnp.int32, (q, LANES), 1)
    srow = lax.broadcasted_iota(jnp.int32, (LANES, LANES), 0)
    half = SSD_HEAD_DIM
    y_pairs = []
    for g in range(SSD_GROUPS):
        bm_g = bm[:, g * n:(g + 1) * n]
        cm_g = cm[:, g * n:(g + 1) * n]
        cb_g = _dot_nt(cm_g.astype(BF16), bm_g.astype(BF16))
        for jp in range(2):
            pair = 2 * g + jp
            m_parts, cce_parts, bcw_parts, cds = [], [], [], []
            for h in (2 * pair, 2 * pair + 1):
                colb = jnp.broadcast_to(acs[:, h:h + 1], (q, LANES))
                dcol = jnp.broadcast_to(dt[:, h:h + 1], (q, LANES))
                seg = colb[:, :q] - acs_t[h:h + 1, :]
                decay = jnp.exp(jnp.where(causal, seg, -jnp.inf))
                m_parts.append(cb_g * decay * dt_t[h:h + 1, :])
                cce_parts.append(cm_g * jnp.exp(colb))
                lastb = last[:, h:h + 1]
                bcw_parts.append(bm_g * (dcol * jnp.exp(lastb - colb)))
                cds.append(jnp.exp(lastb))
            xs_pair = xs[:, pair * LANES:(pair + 1) * LANES]
            top = jnp.where(lane < half, xs_pair, 0.0)
            bot = jnp.where(lane >= half, xs_pair, 0.0)
            w = jnp.concatenate([top, bot], axis=0).astype(BF16)
            if q % LANES == 0:
                yd = _dot(jnp.concatenate(m_parts, axis=1).astype(BF16), w)
            else:
                yd = _dot(m_parts[0], top) + _dot(m_parts[1], bot)
            s_pair = s_read(pair)
            s_blk = jnp.concatenate([jnp.where(srow < half, s_pair, 0.0),
                                     jnp.where(srow >= half, s_pair, 0.0)], axis=1).astype(BF16)
            yo = _dot_nt(jnp.concatenate(cce_parts, axis=1).astype(BF16), s_blk)
            ds = _dot_tn(w, jnp.concatenate(bcw_parts, axis=0).astype(BF16))
            cd = jnp.where(srow < half, jnp.broadcast_to(cds[0], (LANES, LANES)),
                           jnp.broadcast_to(cds[1], (LANES, LANES)))
            s_write(pair, s_pair * cd + ds)
            y_pairs.append(yd + yo + dexp_ref[:, pair * LANES:(pair + 1) * LANES] * xs_pair)
    gw = BRANCH // SSD_GROUPS
    y_groups = []
    for g in range(SSD_GROUPS):
        yg = jnp.concatenate(y_pairs[2 * g:2 * g + 2], axis=1) * jax.nn.silu(z[:, g * gw:(g + 1) * gw])
        y_groups.append(yg * lax.rsqrt(jnp.mean(yg * yg, axis=-1, keepdims=True) + EPS))
    return jnp.concatenate(y_groups, axis=1) * ng_ref[...]


def _ssd_prompt_kernel(zx_ref, dt_ref, cw_ref, cb_ref, dtb_ref, alog_ref, dexp_ref, ng_ref,
                       y_ref, sout_ref, xbuf, s_scr):
    r = zx_ref.shape[0]
    q = SSD_CHUNK
    g = q // SUBLANES
    c = SSD_CONV_DIM

    @pl.when(pl.program_id(1) == 0)
    def _():
        xbuf[0:SUBLANES, :] = jnp.zeros((SUBLANES, c), F32)
        s_scr[...] = jnp.zeros_like(s_scr)

    xbuf[SUBLANES:, :] = zx_ref[:, BRANCH:]

    def s_read(pair):
        return s_scr[pair * LANES:(pair + 1) * LANES, :]

    def s_write(pair, val):
        s_scr[pair * LANES:(pair + 1) * LANES, :] = val

    for ch in range(r // q):
        rows = slice(ch * q, (ch + 1) * q)
        u3 = xbuf[SUBLANES + ch * q:SUBLANES + (ch + 1) * q, :].reshape(g, SUBLANES, c)
        prev3 = xbuf[ch * q:(ch + 1) * q, :].reshape(g, SUBLANES, c)
        y = _ssd_chunk(zx_ref[rows, :BRANCH], u3, prev3, dt_ref[rows, :], s_read, s_write, cw_ref,
                       cb_ref, dtb_ref, alog_ref, dexp_ref, ng_ref)
        y_ref[rows, :] = y.astype(BF16)
    xbuf[0:SUBLANES, :] = xbuf[r:, :]

    @pl.when(pl.program_id(1) == pl.num_programs(1) - 1)
    def _():
        sout_ref[...] = s_scr[...]


def _ssd_sample_kernel(zx_ref, dt_ref, prev_ref, s0_ref, cw_ref, cb_ref, dtb_ref, alog_ref, dexp_ref,
                       ng_ref, y_ref, sout_ref, *, seq):
    c = SSD_CONV_DIM
    ys = []
    for n in range(zx_ref.shape[0] // seq):
        rows = slice(n * seq, (n + 1) * seq)
        u3 = zx_ref[rows, BRANCH:].reshape(1, seq, c)
        prev3 = prev_ref[rows, :].reshape(1, seq, c)

        def s_read(pair, n=n):
            return s0_ref[n, pair * LANES:(pair + 1) * LANES, :]

        def s_write(pair, val, n=n):
            sout_ref[n, pair * LANES:(pair + 1) * LANES, :] = val

        y = _ssd_chunk(zx_ref[rows, :BRANCH], u3, prev3, dt_ref[rows, :], s_read, s_write, cw_ref,
                       cb_ref, dtb_ref, alog_ref, dexp_ref, ng_ref)
        ys.append(y)
    y_ref[...] = jnp.concatenate(ys, axis=0).astype(BF16)


def _ssd_weight_specs(p):
    return [_const_spec(p[k].shape) for k in ("conv_w", "conv_b", "dt_bias", "a_log", "d_exp", "norm_g")]


def _ssd_weights(p):
    return tuple(p[k] for k in ("conv_w", "conv_b", "dt_bias", "a_log", "d_exp", "norm_g"))


def _ssd_core_prompt(zx, dt, p, bsz, seq, q):
    nt = seq // q
    hp = SSD_HEADS * SSD_HEAD_DIM
    return pl.pallas_call(
        _ssd_prompt_kernel,
        grid=(bsz, nt),
        in_specs=[pl.BlockSpec((q, zx.shape[1]), lambda b, i: (b * nt + i, 0)),
                  pl.BlockSpec((q, LANES), lambda b, i: (b * nt + i, 0))] + _ssd_weight_specs(p),
        out_specs=[pl.BlockSpec((q, BRANCH), lambda b, i: (b * nt + i, 0)),
                   pl.BlockSpec((None, hp, SSD_STATE), lambda b, i: (b, 0, 0))],
        out_shape=[jax.ShapeDtypeStruct((bsz * seq, BRANCH), BF16),
                   jax.ShapeDtypeStruct((bsz, hp, SSD_STATE), F32)],
        scratch_shapes=[pltpu.VMEM((SUBLANES + q, SSD_CONV_DIM), F32), pltpu.VMEM((hp, SSD_STATE), F32)],
        compiler_params=_params(2), name="ssd_core_prompt",
    )(zx, dt, *_ssd_weights(p))


def _ssd_core_sample(zx, dt, prev8, s0, idx, p, seq, nb):
    t = zx.shape[0]
    bsz = t // seq
    hp = SSD_HEADS * SSD_HEAD_DIM
    row = lambda w: pl.BlockSpec((nb * seq, w), lambda i: (i, 0))
    st = pl.BlockSpec((nb, hp, SSD_STATE), lambda i: (i, 0, 0))
    st_in = pl.BlockSpec((None, nb, hp, SSD_STATE), lambda i: (idx, i, 0, 0))
    return pl.pallas_call(
        functools.partial(_ssd_sample_kernel, seq=seq),
        grid=(bsz // nb,),
        in_specs=[row(zx.shape[1]), row(LANES), row(SSD_CONV_DIM), st_in] + _ssd_weight_specs(p),
        out_specs=[row(BRANCH), st],
        out_shape=[jax.ShapeDtypeStruct((t, BRANCH), BF16), jax.ShapeDtypeStruct((bsz, hp, SSD_STATE), F32)],
        compiler_params=_params(1), name="ssd_core_sample",
    )(zx, dt, prev8, s0, *_ssd_weights(p))


def _hg_lower_bound(rows, layer):
    mx = functools.reduce(jnp.maximum, rows)
    es = [jnp.exp(x - mx) for x in rows]
    return sum(es[1:layer + 1]) / sum(es)


def _hg_prompt_kernel(proj_ref, hlb_ref, ng_ref, y_ref, sout_ref, st_scr, *, layer):
    r = proj_ref.shape[0]
    c = BRANCH
    dk = HGRN_KEY_DIM
    blk, sub = HGRN_BLOCK, HGRN_CHUNK
    nblk, nsub = r // blk, blk // sub

    @pl.when(pl.program_id(1) == 0)
    def _():
        st_scr[...] = jnp.zeros_like(st_scr)

    causal = (lax.broadcasted_iota(jnp.int32, (blk, blk), 0)
              >= lax.broadcasted_iota(jnp.int32, (blk, blk), 1))

    def head_body(h, carry):
        lanes = pl.ds(pl.multiple_of(h * dk, dk), dk)

        def proj_part(part):
            return proj_ref[:, pl.ds(pl.multiple_of(part * c + h * dk, dk), dk)]

        lb = _hg_lower_bound([hlb_ref[j:j + 1, lanes] for j in range(DEPTH)], layer)
        f = proj_part(1)
        forget = lb + (1.0 - lb) * jax.nn.sigmoid(f)
        k = ((1.0 - lb) * jax.nn.sigmoid(-f)).reshape(nblk, blk, dk)
        x = _cumsum8(jnp.log(forget).reshape(r // SUBLANES, SUBLANES, dk)).reshape(nblk, blk, dk)
        parts, carry_row = [], None
        for j in range(blk // SUBLANES):
            part = x[:, j * SUBLANES:(j + 1) * SUBLANES, :]
            if carry_row is not None:
                part = part + carry_row
            carry_row = jnp.broadcast_to(part[:, SUBLANES - 1:, :], part.shape)
            parts.append(part)
        gc = jnp.concatenate(parts, axis=1)
        ends = [gc[:, (i + 1) * sub - 1:(i + 1) * sub, :] for i in range(nsub)]
        starts = [jnp.zeros_like(ends[0])] + ends[:-1]
        spread = lambda rows_: jnp.concatenate(
            [jnp.broadcast_to(x_, (nblk, sub, dk)) for x_ in rows_], axis=1)
        b_prev, b_next = spread(starts), spread(ends)
        qi = jax.nn.silu(proj_part(0)).reshape(nblk, blk, dk) * jnp.exp(gc - b_prev)
        qc = qi * jnp.exp(b_prev)
        kd = k * jnp.exp(b_prev - gc)
        ke = kd * jnp.exp(b_next - b_prev)
        kend = ke * jnp.exp(ends[-1] - b_next)
        v = proj_part(2).reshape(nblk, blk, dk)

        st = st_scr[h]
        outs = []
        for b in range(nblk):
            qi_b = qi[b].astype(BF16)
            att_rows = []
            for i in range(nsub):
                keys = []
                for j in range(nsub):
                    rs = slice(j * sub, (j + 1) * sub)
                    if j == i:
                        keys.append(kd[b, rs])
                    elif j < i - 1:
                        keys.append(ke[b, rs] * jnp.exp(starts[i][b] - ends[j][b]))
                    else:
                        keys.append(ke[b, rs])
                keys = jnp.concatenate(keys, axis=0).astype(BF16)
                att_rows.append(_dot_nt(qi_b[i * sub:(i + 1) * sub], keys))
            att = jnp.where(causal, jnp.concatenate(att_rows, axis=0), 0.0).astype(BF16)
            vb = v[b].astype(BF16)
            outs.append(_dot(att, vb) + _dot_nt(qc[b].astype(BF16), st.astype(BF16)))
            st = st * jnp.exp(ends[-1][b]) + _dot_tn(vb, kend[b].astype(BF16))
        st_scr[h] = st
        o = jnp.concatenate(outs, axis=0)
        o = o * lax.rsqrt(jnp.mean(o * o, axis=-1, keepdims=True) + EPS)
        gate = jax.nn.silu(proj_part(3))
        y_ref[:, lanes] = (o * ng_ref[:, lanes] * gate).astype(BF16)
        return carry

    lax.fori_loop(0, HGRN_HEADS, head_body, 0, unroll=True)

    @pl.when(pl.program_id(1) == pl.num_programs(1) - 1)
    def _():
        for h in range(HGRN_HEADS):
            sout_ref[h] = st_scr[h].T


def _hg_seq_kernel(proj_ref, hlb_ref, ng_ref, s0_ref, y_ref, sout_ref,
                   qg_scr, kg_scr, ke_scr, v_scr, dec_scr, o_scr, *, layer, chunk):
    r = proj_ref.shape[0]
    c = BRANCH
    g = r // SUBLANES
    n_chunks = r // chunk
    dk, dv = HGRN_KEY_DIM, HGRN_VAL_DIM
    assert chunk == SUBLANES

    lb = _hg_lower_bound([hlb_ref[j:j + 1, :] for j in range(DEPTH)], layer)
    f = proj_ref[:, c:2 * c]
    forget = lb + (1.0 - lb) * jax.nn.sigmoid(f)
    k = (1.0 - lb) * jax.nn.sigmoid(-f)
    gcum3 = _cumsum8(jnp.log(forget).reshape(g, SUBLANES, c))
    gcum = gcum3.reshape(r, c)
    glast = jnp.broadcast_to(gcum3[:, SUBLANES - 1:, :], gcum3.shape).reshape(r, c)
    qg_scr[...] = jax.nn.silu(proj_ref[:, :c]) * jnp.exp(gcum)
    kg_scr[...] = k * jnp.exp(-gcum)
    ke_scr[...] = k * jnp.exp(glast - gcum)
    dec_scr[...] = jnp.exp(glast)
    v_scr[...] = proj_ref[:, 2 * c:3 * c]

    causal = (lax.broadcasted_iota(jnp.int32, (chunk, chunk), 0)
              >= lax.broadcasted_iota(jnp.int32, (chunk, chunk), 1))

    def chunk_body(ci, carry):
        rs = pl.ds(pl.multiple_of(ci * chunk, chunk), chunk)
        outs = []
        for h in range(HGRN_HEADS):
            ks = slice(h * dk, (h + 1) * dk)
            vs = slice(h * dv, (h + 1) * dv)
            qg = qg_scr[rs, ks].astype(BF16)
            vv = v_scr[rs, vs].astype(BF16)
            st = s0_ref[ci, h].T
            att = jnp.where(causal, _dot_nt(qg, kg_scr[rs, ks].astype(BF16)), 0.0)
            outs.append(_dot(att.astype(BF16), vv) + _dot_nt(qg, st.astype(BF16)))
            dec = dec_scr[rs, ks][chunk - 1:, :]
            sout_ref[ci, h] = (st * dec + _dot_tn(vv, ke_scr[rs, ks].astype(BF16))).T
        o_scr[rs, :] = jnp.concatenate(outs, axis=1)
        return carry

    lax.fori_loop(0, n_chunks, chunk_body, 0, unroll=2)

    gate = jax.nn.silu(proj_ref[:, 3 * c:])
    parts = []
    for h in range(HGRN_HEADS):
        o = o_scr[:, h * dv:(h + 1) * dv]
        parts.append(o * lax.rsqrt(jnp.mean(o * o, axis=-1, keepdims=True) + EPS))
    y_ref[...] = (jnp.concatenate(parts, axis=1) * ng_ref[...] * gate).astype(BF16)


def _hg_core_prompt(proj, hlb, ng, layer, bsz, seq, tl):
    nt = seq // tl
    c = BRANCH
    st_shape = (HGRN_HEADS, HGRN_KEY_DIM, HGRN_VAL_DIM)
    return pl.pallas_call(
        functools.partial(_hg_prompt_kernel, layer=layer),
        grid=(bsz, nt),
        in_specs=[pl.BlockSpec((tl, 4 * c), lambda b, i: (b * nt + i, 0)),
                  _const_spec(hlb.shape), _const_spec(ng.shape)],
        out_specs=[pl.BlockSpec((tl, c), lambda b, i: (b * nt + i, 0)),
                   pl.BlockSpec((None,) + st_shape, lambda b, i: (b, 0, 0, 0))],
        out_shape=[jax.ShapeDtypeStruct((bsz * seq, c), BF16),
                   jax.ShapeDtypeStruct((bsz,) + st_shape, F32)],
        scratch_shapes=[pltpu.VMEM((HGRN_HEADS, HGRN_VAL_DIM, HGRN_KEY_DIM), F32)],
        compiler_params=_params(2), name="hg_core_prompt",
    )(proj, hlb, ng)


def _hg_core_sample(proj, s0, idx, hlb, ng, layer, seq, nb):
    t = proj.shape[0]
    bsz = t // seq
    c = BRANCH
    rows = nb * seq
    st_shape = (nb, HGRN_HEADS, HGRN_KEY_DIM, HGRN_VAL_DIM)
    st_spec = pl.BlockSpec(st_shape, lambda i: (i, 0, 0, 0))
    st_in = pl.BlockSpec((None,) + st_shape, lambda i: (idx, i, 0, 0, 0))
    return pl.pallas_call(
        functools.partial(_hg_seq_kernel, layer=layer, chunk=seq),
        grid=(bsz // nb,),
        in_specs=[pl.BlockSpec((rows, 4 * c), lambda i: (i, 0)),
                  _const_spec(hlb.shape), _const_spec(ng.shape), st_in],
        out_specs=[pl.BlockSpec((rows, c), lambda i: (i, 0)), st_spec],
        out_shape=[jax.ShapeDtypeStruct((t, c), BF16), jax.ShapeDtypeStruct(s0.shape[1:], F32)],
        scratch_shapes=[pltpu.VMEM((rows, c), F32)] * 6,
        compiler_params=_params(1), name="hg_core_sample",
    )(proj, hlb, ng, s0)


ROWS_PROJ_IN = 256
ROWS_PROJ_OUT = 512
ROWS_RG_PROMPT = 512
ROWS_SSD_PROMPT = 2 * SSD_CHUNK
ROWS_HG_PROMPT = 4 * HGRN_BLOCK
ROWS_ATTN_PROMPT = 1024
SEQS_SSD_SAMPLE = 4
SEQS_HG_SAMPLE = 8
SEQS_ATTN_SAMPLE = 4
BATCH_MEM_KV = 2


def _tile(n, target):
    t = min(n, target)
    assert n % t == 0, (n, target)
    return t


def _pad_groups(state, first_row):
    n, k, c = state.shape
    return jnp.pad(state, ((0, 0), (first_row, SUBLANES - first_row - k), (0, 0))).reshape(n * SUBLANES, c)


def _trunk(x, mem_k, mem_v, states, w, bsz, seq, prompt):
    tm = _tile(x.shape[0], ROWS_PROJ_IN)
    rg_conv, rg_h, ssd_conv, ssd_s, hg_s = [], [], [], [], []
    tail = slice(seq - (CONV_W - 1), seq)
    for layer in range(DEPTH):
        kind, idx = layer % N_MIXERS, layer // N_MIXERS
        g = w["norm_g"][layer]
        if kind == 0:
            p = w["rg"][idx]
            if prompt:
                x, utail, htail = _rg_layer_prompt(x, g, p, bsz, seq, _tile(seq, ROWS_RG_PROMPT))
                rg_h.append(htail[:, SUBLANES - 1])
                rg_conv.append(utail[:, :, 0])
                y = None
            else:
                (proj,) = _norm_matmul(x, g[0:1], [p["w_in"]], tm)
                prev8 = _pad_groups(states["rg_conv"][idx], SUBLANES - (CONV_W - 1))
                h0pad = _pad_groups(states["rg_h"][idx][:, None, :], 0)
                y, h = _rg_core_sample(proj, prev8, h0pad, p, tm)
                rg_h.append(h.reshape(bsz, seq, BRANCH)[:, seq - 1])
                rg_conv.append(proj.reshape(bsz, seq, 2 * BRANCH)[:, tail, :BRANCH])
        elif kind == 1:
            p = w["ssd"][idx]
            zx, dt = _norm_matmul(x, g[0:1], [p["w_zx"], p["w_dt"]], tm)
            if prompt:
                y, s_new = _ssd_core_prompt(zx, dt, p, bsz, seq, _tile(seq, ROWS_SSD_PROMPT))
            else:
                prev8 = _pad_groups(states["ssd_conv"][idx], SUBLANES - (CONV_W - 1))
                s0 = states["ssd_s"].reshape(-1, bsz, SSD_HEADS * SSD_HEAD_DIM, SSD_STATE)
                y, s_new = _ssd_core_sample(zx, dt, prev8, s0, idx, p, seq, _tile(bsz, SEQS_SSD_SAMPLE))
            ssd_s.append(s_new.reshape(bsz, SSD_HEADS, SSD_HEAD_DIM, SSD_STATE))
            ssd_conv.append(zx.reshape(bsz, seq, BRANCH + SSD_CONV_DIM)[:, tail, BRANCH:])
        else:
            p = w["hg"][idx]
            (proj,) = _norm_matmul(x, g[0:1], [p["w_in"]], tm)
            if prompt:
                y, s_new = _hg_core_prompt(proj, w["hg_lower_bounds"], p["norm_g"], layer, bsz, seq,
                                           _tile(seq, ROWS_HG_PROMPT))
            else:
                y, s_new = _hg_core_sample(proj, states["hg_s"], idx, w["hg_lower_bounds"], p["norm_g"],
                                           layer, seq, _tile(bsz, SEQS_HG_SAMPLE))
            hg_s.append(s_new)
        if y is not None:
            x = _proj_norm_res(y, p["w_out"], g[1:2], x, _tile(x.shape[0], ROWS_PROJ_OUT))
        if prompt:
            x = _attn_prompt(x, mem_k, mem_v, layer, w["x_w_q"][layer], w["x_w_o"][layer], g,
                             bsz, seq, _tile(seq, ROWS_ATTN_PROMPT))
        else:
            x = _attn_sample(x, mem_k, mem_v, layer, w["x_w_q"][layer], w["x_w_o"][layer], g,
                             seq, _tile(bsz, SEQS_ATTN_SAMPLE))
    return x, jnp.stack(rg_conv), jnp.stack(rg_h), jnp.stack(ssd_conv), jnp.stack(ssd_s), jnp.stack(hg_s)


def kernel(x_prompt, x_sample, mem_prompt, state_rglru_conv, state_rglru_h, state_ssd_conv, state_ssd,
           state_hgrn, cache_mem_k, cache_mem_v, norm_g, mem_norm_g, rg_w_in, rg_conv_w, rg_conv_b, rg_w_a,
           rg_b_a, rg_w_x, rg_b_x, rg_lambda, rg_w_out, ssd_w_in, ssd_conv_w, ssd_conv_b, ssd_dt_bias,
           ssd_a_log, ssd_d, ssd_norm_g, ssd_w_out, hg_w_in, hg_lower_bounds, hg_norm_g, hg_w_out,
           x_w_q, x_w_k, x_w_v, x_w_o):
    bp, sp, d = x_prompt.shape
    bs, ss, _ = x_sample.shape
    n_a, n_b, n_c = rg_w_in.shape[0], ssd_w_in.shape[0], hg_w_in.shape[0]
    pad_heads = lambda v: jnp.pad(v, (0, LANES - SSD_HEADS))[None, :]
    w = {
        "norm_g": norm_g,
        "hg_lower_bounds": hg_lower_bounds,
        "x_w_q": x_w_q.astype(BF16),
        "x_w_o": x_w_o.astype(BF16),
        "rg": [{
            "w_in": rg_w_in[i].astype(BF16),
            "conv_w": rg_conv_w[i], "conv_b": rg_conv_b[i][None, :],
            "w_ax": jnp.concatenate([rg_w_a[i], rg_w_x[i]], axis=-1).astype(BF16),
            "b_a": rg_b_a[i][None, :], "b_x": rg_b_x[i][None, :], "lam": rg_lambda[i][None, :],
            "w_out": rg_w_out[i].astype(BF16),
        } for i in range(n_a)],
        "ssd": [{
            "w_zx": ssd_w_in[i][:, :BRANCH + SSD_CONV_DIM].astype(BF16),
            "w_dt": jnp.pad(ssd_w_in[i][:, BRANCH + SSD_CONV_DIM:], ((0, 0), (0, LANES - SSD_HEADS))).astype(BF16),
            "conv_w": ssd_conv_w[i], "conv_b": ssd_conv_b[i][None, :],
            "dt_bias": pad_heads(ssd_dt_bias[i]), "a_log": pad_heads(ssd_a_log[i]),
            "d_exp": jnp.repeat(ssd_d[i], SSD_HEAD_DIM)[None, :],
            "norm_g": ssd_norm_g[i][None, :],
            "w_out": ssd_w_out[i].astype(BF16),
        } for i in range(n_b)],
        "hg": [{
            "w_in": hg_w_in[i].astype(BF16),
            "norm_g": hg_norm_g[i][None, :],
            "w_out": hg_w_out[i].astype(BF16),
        } for i in range(n_c)],
    }

    mem_k_p, mem_v_p = _mem_kv(mem_prompt.reshape(bp * N_MEM, d), mem_norm_g[:, None, :],
                               x_w_k.astype(BF16), x_w_v.astype(BF16), _tile(bp, BATCH_MEM_KV))
    y_p, rgc_p, rgh_p, sc_p, ss_p, hs_p = _trunk(
        x_prompt.reshape(bp * sp, d), mem_k_p, mem_v_p, None, w, bp, sp, True)
    states = {"rg_conv": state_rglru_conv, "rg_h": state_rglru_h, "ssd_conv": state_ssd_conv,
              "ssd_s": state_ssd, "hg_s": state_hgrn}
    y_s, rgc_s, rgh_s, sc_s, ss_s, hs_s = _trunk(
        x_sample.reshape(bs * ss, d), _kv_flat(cache_mem_k), _kv_flat(cache_mem_v), states, w, bs, ss, False)
    return (y_p.reshape(bp, sp, d), y_s.reshape(bs, ss, d), rgc_p, rgh_p, sc_p, ss_p, hs_p,
            _kv_unflat(mem_k_p), _kv_unflat(mem_v_p), rgc_s, rgh_s, sc_s, ss_s, hs_s)
```

```python
import functools

import jax
import jax.numpy as jnp
from jax import lax
from jax.experimental import pallas as pl
from jax.experimental.pallas import tpu as pltpu

F32 = jnp.float32
BF16 = jnp.bfloat16

D_MODEL = 1024
DEPTH = 4
N_MIXERS = 3
BRANCH = 2 * D_MODEL
CONV_W = 4
EPS = 1e-6
LRU_BLOCKS = 8
LRU_BLOCK = BRANCH // LRU_BLOCKS
LRU_C = 8.0
SSD_HEAD_DIM = 64
SSD_HEADS = BRANCH // SSD_HEAD_DIM
SSD_STATE = 128
SSD_GROUPS = 8
SSD_CONV_DIM = BRANCH + 2 * SSD_GROUPS * SSD_STATE
SSD_CHUNK = 128
HGRN_KEY_DIM = 128
HGRN_HEADS = BRANCH // HGRN_KEY_DIM
HGRN_VAL_DIM = BRANCH // HGRN_HEADS
HGRN_CHUNK = 16
HGRN_BLOCK = 4 * HGRN_CHUNK
N_MEM = 256
X_HEADS = 4
X_HEAD_DIM = D_MODEL // X_HEADS

SUBLANES = 8
LANES = 128
VMEM_BYTES_V7X = 64 * 1024 * 1024
VMEM_LIMIT = VMEM_BYTES_V7X * 7 // 8

NT_DIMS = (((1,), (1,)), ((), ()))
TN_DIMS = (((0,), (0,)), ((), ()))


def _params(n_grid_dims):
    return pltpu.CompilerParams(
        dimension_semantics=("arbitrary",) * n_grid_dims, vmem_limit_bytes=VMEM_LIMIT)


def _const_spec(shape):
    nd = len(shape)
    return pl.BlockSpec(shape, lambda *_: (0,) * nd, pipeline_mode=pl.Buffered(1))


def _rms(x, g):
    return x * lax.rsqrt(jnp.mean(x * x, axis=-1, keepdims=True) + EPS) * g


def _dot(a, b):
    return jnp.dot(a, b, preferred_element_type=F32)


def _dot_nt(a, b):
    return lax.dot_general(a, b, NT_DIMS, preferred_element_type=F32)


def _dot_tn(a, b):
    return lax.dot_general(a, b, TN_DIMS, preferred_element_type=F32)


def _group_iota(width):
    return lax.broadcasted_iota(jnp.int32, (1, SUBLANES, width), 1)


def _conv8(u3, prev3, cw, cb):
    t = _group_iota(u3.shape[-1])
    acc = cb + cw[CONV_W - 1:CONV_W, :] * u3
    for k in range(1, CONV_W):
        shifted = pltpu.roll(jnp.where(t >= SUBLANES - k, prev3, u3), k, 1)
        acc = acc + cw[CONV_W - 1 - k:CONV_W - k, :] * shifted
    return acc


def _scan8(a3, b3):
    t = _group_iota(a3.shape[-1])
    for s in (1, 2, 4):
        m = t >= s
        a_sh = pltpu.roll(a3, s, 1)
        b_sh = pltpu.roll(b3, s, 1)
        b3 = jnp.where(m, a3 * b_sh + b3, b3)
        a3 = jnp.where(m, a3 * a_sh, a3)
    return a3, b3


def _cumsum8(x3):
    t = _group_iota(x3.shape[-1])
    for s in (1, 2, 4):
        x3 = x3 + jnp.where(t >= s, pltpu.roll(x3, s, 1), 0.0)
    return x3


def _norm_matmul_kernel(x_ref, g_ref, *refs, n_chunk):
    n_w = len(refs) // 2
    h = _rms(x_ref[...], g_ref[...]).astype(BF16)
    for w_ref, o_ref in zip(refs[:n_w], refs[n_w:]):
        n = o_ref.shape[-1]
        step = min(n_chunk, n)
        for c in range(0, n, step):
            o_ref[:, c:c + step] = _dot(h, w_ref[:, c:c + step])


def _norm_matmul(x, g, ws, tm):
    t, d = x.shape
    grid = (t // tm,)
    in_specs = [pl.BlockSpec((tm, d), lambda i: (i, 0)), _const_spec((1, d))]
    in_specs += [_const_spec(w.shape) for w in ws]
    out_specs = [pl.BlockSpec((tm, w.shape[1]), lambda i: (i, 0)) for w in ws]
    out_shape = [jax.ShapeDtypeStruct((t, w.shape[1]), F32) for w in ws]
    return pl.pallas_call(
        functools.partial(_norm_matmul_kernel, n_chunk=512),
        grid=grid, in_specs=in_specs, out_specs=out_specs, out_shape=out_shape,
        compiler_params=_params(1), name="norm_matmul",
    )(x, g, *ws)


def _proj_norm_res_kernel(a_ref, w_ref, g_ref, x_ref, o_ref):
    y = _dot(a_ref[...], w_ref[...])
    o_ref[...] = x_ref[...] + _rms(y, g_ref[...])


def _proj_norm_res(a, w, g, x, tm):
    t, k = a.shape
    d = x.shape[1]
    return pl.pallas_call(
        _proj_norm_res_kernel,
        grid=(t // tm,),
        in_specs=[pl.BlockSpec((tm, k), lambda i: (i, 0)), _const_spec(w.shape), _const_spec((1, d)),
                  pl.BlockSpec((tm, d), lambda i: (i, 0))],
        out_specs=pl.BlockSpec((tm, d), lambda i: (i, 0)),
        out_shape=jax.ShapeDtypeStruct((t, d), F32),
        compiler_params=_params(1), name="proj_norm_res",
    )(a, w, g, x)


KV_LANE_TILES = X_HEAD_DIM // LANES
KV_ROWS = KV_LANE_TILES * X_HEADS


def _kv_flat(kv):
    lead = kv.shape[:-3]
    x = kv.reshape(lead + (N_MEM, X_HEADS, KV_LANE_TILES, LANES))
    return jnp.swapaxes(x, -3, -2).reshape(lead + (N_MEM * KV_ROWS, LANES))


def _kv_unflat(flat):
    lead = flat.shape[:-2]
    x = flat.reshape(lead + (N_MEM, KV_LANE_TILES, X_HEADS, LANES))
    return jnp.swapaxes(x, -3, -2).reshape(lead + (N_MEM, X_HEADS, X_HEAD_DIM))


def _mem_kv_kernel(m_ref, g_ref, wk_ref, wv_ref, k_ref, v_ref):
    h = _rms(m_ref[...], g_ref[...]).astype(BF16)
    tm = m_ref.shape[0]
    for w_ref, o_ref in ((wk_ref, k_ref), (wv_ref, v_ref)):
        y = _dot(h, w_ref[...])
        pieces = [y[:, hd * X_HEAD_DIM + t * LANES:hd * X_HEAD_DIM + (t + 1) * LANES][None]
                  for t in range(KV_LANE_TILES) for hd in range(X_HEADS)]
        rows = jnp.swapaxes(jnp.concatenate(pieces, axis=0), 0, 1)
        o_ref[...] = rows.reshape(o_ref.shape)


def _mem_kv(mem, g, wk, wv, nb):
    t, d = mem.shape
    bsz = t // N_MEM
    tm = nb * N_MEM
    w_spec = pl.BlockSpec((None, d, d), lambda l, i: (l, 0, 0))
    o_spec = pl.BlockSpec((None, nb, N_MEM * KV_ROWS, LANES), lambda l, i: (l, i, 0, 0))
    return pl.pallas_call(
        _mem_kv_kernel,
        grid=(DEPTH, bsz // nb),
        in_specs=[pl.BlockSpec((tm, d), lambda l, i: (i, 0)),
                  pl.BlockSpec((None, 1, d), lambda l, i: (l, 0, 0)), w_spec, w_spec],
        out_specs=[o_spec, o_spec],
        out_shape=[jax.ShapeDtypeStruct((DEPTH, bsz, N_MEM * KV_ROWS, LANES), F32)] * 2,
        compiler_params=_params(2), name="mem_kv",
    )(mem, g, wk, wv)


def _attn_kernel(x_ref, k_ref, v_ref, wq_ref, wo_ref, g_ref, o_ref, q_scr, a_scr, k_scr, v_scr, *,
                 rows, hoist, seq_rows):
    i = pl.program_id(0) if hoist else None
    scale = X_HEAD_DIM ** -0.5
    n_keys = k_scr.shape[1]

    def split_heads():
        def every_kv_row(ref, first):
            rows = pl.ds(first, N_MEM, stride=KV_ROWS)
            if len(ref.shape) == 2:
                return ref[rows, :]
            return ref[:, rows, :].reshape(n_keys, LANES)

        for ref, scr in ((k_ref, k_scr), (v_ref, v_scr)):
            for h in range(X_HEADS):
                scr[h] = jnp.concatenate([every_kv_row(ref, t * X_HEADS + h) for t in range(KV_LANE_TILES)],
                                         axis=1).astype(BF16)

    if hoist:
        split_heads()
    else:
        pl.when(pl.program_id(1) == 0)(split_heads)

    def project_q():
        h = _rms(x_ref[...], g_ref[2:3, :]).astype(BF16)
        q_scr[...] = (_dot(h, wq_ref[...]) * scale).astype(BF16)

    def project_out():
        y = _dot(a_scr[...], wo_ref[...])
        o_ref[...] = x_ref[...] + _rms(y, g_ref[3:4, :])

    if hoist:
        pl.when(i == 0)(project_q)
        r0 = pl.multiple_of(i * rows, rows)
        rsl = pl.ds(r0, rows)
    else:
        project_q()
        rsl = slice(None)

    if seq_rows is not None:
        qi = lax.broadcasted_iota(jnp.int32, (rows, n_keys), 0) // seq_rows
        ki = lax.broadcasted_iota(jnp.int32, (rows, n_keys), 1) // N_MEM
        mask = qi == ki
    for h in range(X_HEADS):
        hs = slice(h * X_HEAD_DIM, (h + 1) * X_HEAD_DIM)
        qh = q_scr[rsl, hs]
        s = _dot_nt(qh, k_scr[h])
        if seq_rows is not None:
            s = jnp.where(mask, s, -jnp.inf)
        e = jnp.exp(s - jnp.max(s, axis=-1, keepdims=True))
        p = e / jnp.sum(e, axis=-1, keepdims=True)
        a_scr[rsl, hs] = _dot(p.astype(BF16), v_scr[h]).astype(BF16)

    if hoist:
        pl.when(i == pl.num_programs(0) - 1)(project_out)
    else:
        project_out()


def _attn_prompt(x, k, v, layer, wq, wo, g, bsz, seq, tl):
    d = x.shape[1]
    nt = seq // tl
    kv_spec = pl.BlockSpec((None, None, N_MEM * KV_ROWS, LANES), lambda b, i: (layer, b, 0, 0))
    kv_scr = pltpu.VMEM((X_HEADS, N_MEM, X_HEAD_DIM), BF16)
    return pl.pallas_call(
        functools.partial(_attn_kernel, rows=tl, hoist=False, seq_rows=None),
        grid=(bsz, nt),
        in_specs=[pl.BlockSpec((tl, d), lambda b, i: (b * nt + i, 0)), kv_spec, kv_spec,
                  _const_spec(wq.shape), _const_spec(wo.shape), _const_spec(g.shape)],
        out_specs=pl.BlockSpec((tl, d), lambda b, i: (b * nt + i, 0)),
        out_shape=jax.ShapeDtypeStruct(x.shape, F32),
        scratch_shapes=[pltpu.VMEM((tl, d), BF16), pltpu.VMEM((tl, d), BF16), kv_scr, kv_scr],
        compiler_params=_params(2), name="attn_prompt",
    )(x, k, v, wq, wo, g)


def _attn_sample(x, k, v, layer, wq, wo, g, seq, nb):
    t, d = x.shape
    bsz = t // seq
    kv_spec = pl.BlockSpec((None, nb, N_MEM * KV_ROWS, LANES), lambda i: (layer, i, 0, 0))
    kv_scr = pltpu.VMEM((X_HEADS, nb * N_MEM, X_HEAD_DIM), BF16)
    return pl.pallas_call(
        functools.partial(_attn_kernel, rows=nb * seq, hoist=True, seq_rows=seq),
        grid=(bsz // nb,),
        in_specs=[_const_spec(x.shape), kv_spec, kv_spec,
                  _const_spec(wq.shape), _const_spec(wo.shape), _const_spec(g.shape)],
        out_specs=pl.BlockSpec(x.shape, lambda i: (0, 0)),
        out_shape=jax.ShapeDtypeStruct(x.shape, F32),
        scratch_shapes=[pltpu.VMEM((t, d), BF16), pltpu.VMEM((t, d), BF16), kv_scr, kv_scr],
        compiler_params=_params(1), name="attn_sample",
    )(x, k, v, wq, wo, g)


def _rg_gates(conv, blk, wax_ref, ba_ref, bx_ref, lam_ref):
    sl = slice(blk * LRU_BLOCK, (blk + 1) * LRU_BLOCK)
    pre = _dot(conv.astype(BF16), wax_ref[blk])
    rg = jax.nn.sigmoid(pre[:, :LRU_BLOCK] + ba_ref[:, sl])
    ig = jax.nn.sigmoid(pre[:, LRU_BLOCK:] + bx_ref[:, sl])
    a = jnp.exp(rg * (-LRU_C * jax.nn.softplus(-lam_ref[:, sl])))
    b = jnp.exp(0.5 * jnp.log(1.0 - a * a)) * (ig * conv)
    return a, b


def _rg_block_ab(u3, prev3, blk, cw_ref, cb_ref, wax_ref, ba_ref, bx_ref, lam_ref):
    g = u3.shape[0]
    sl = slice(blk * LRU_BLOCK, (blk + 1) * LRU_BLOCK)
    conv = _conv8(u3, prev3, cw_ref[:, sl], cb_ref[:, sl]).reshape(g * SUBLANES, LRU_BLOCK)
    a, b = _rg_gates(conv, blk, wax_ref, ba_ref, bx_ref, lam_ref)
    return a.reshape(g, SUBLANES, LRU_BLOCK), b.reshape(g, SUBLANES, LRU_BLOCK)


def _rg_ab(u3, prev3, cw_ref, cb_ref, wax_ref, ba_ref, bx_ref, lam_ref):
    parts = [_rg_block_ab(u3[:, :, blk * LRU_BLOCK:(blk + 1) * LRU_BLOCK],
                          prev3[:, :, blk * LRU_BLOCK:(blk + 1) * LRU_BLOCK],
                          blk, cw_ref, cb_ref, wax_ref, ba_ref, bx_ref, lam_ref)
             for blk in range(LRU_BLOCKS)]
    return (jnp.concatenate([p[0] for p in parts], axis=2), jnp.concatenate([p[1] for p in parts], axis=2))


def _rg_layer_prompt_kernel(x_ref, g_ref, win_ref, cw_ref, cb_ref, wax_ref, ba_ref, bx_ref, lam_ref,
                            wout_ref, o_ref, utail_ref, htail_ref, tail_scr, y_scr, hc):
    r, d = x_ref.shape
    c = BRANCH
    n = r // SUBLANES
    taps = CONV_W - 1

    @pl.when(pl.program_id(1) == 0)
    def _():
        tail_scr[...] = jnp.zeros_like(tail_scr)
        hc[...] = jnp.zeros_like(hc)

    xs = jnp.swapaxes(x_ref[...].reshape(SUBLANES, n, d), 0, 1).reshape(r, d)
    xn = _rms(xs, g_ref[0:1, :]).astype(BF16)
    first_seg = lax.broadcasted_iota(jnp.int32, (SUBLANES, LRU_BLOCK), 0) == 0
    for blk in range(LRU_BLOCKS):
        sl = slice(blk * LRU_BLOCK, (blk + 1) * LRU_BLOCK)
        u3 = _dot(xn, win_ref[:, sl]).reshape(n, SUBLANES, LRU_BLOCK)
        gate = _dot(xn, win_ref[:, c + blk * LRU_BLOCK:c + (blk + 1) * LRU_BLOCK])
        hist = []
        for j in range(taps):
            prev_seg = pltpu.roll(u3[n - taps + j], 1, 0)
            hist.append(jnp.where(first_seg, tail_scr[j, :, sl], prev_seg)[None])
            tail_scr[j, :, sl] = prev_seg
            utail_ref[j, :, sl] = prev_seg
        uext = jnp.concatenate(hist + [u3], axis=0)
        conv = cb_ref[:, sl] + cw_ref[taps:CONV_W, sl] * u3
        for k in range(1, CONV_W):
            conv = conv + cw_ref[taps - k:CONV_W - k, sl] * uext[taps - k:taps - k + n]
        a, b = _rg_gates(conv.reshape(r, LRU_BLOCK), blk, wax_ref, ba_ref, bx_ref, lam_ref)
        a3 = a.reshape(n, SUBLANES, LRU_BLOCK)
        b3 = b.reshape(n, SUBLANES, LRU_BLOCK)
        hs, prods = [b3[0]], [a3[0]]
        for s in range(1, n):
            hs.append(a3[s] * hs[-1] + b3[s])
            prods.append(a3[s] * prods[-1])
        h_in = hc[0:1, sl]
        carry_rows = []
        for q in range(SUBLANES):
            carry_rows.append(h_in)
            h_in = hs[-1][q:q + 1] + prods[-1][q:q + 1] * h_in
        carry = jnp.concatenate(carry_rows, axis=0)
        h_last = jnp.broadcast_to(h_in, (SUBLANES, LRU_BLOCK))
        hc[:, sl] = h_last
        htail_ref[:, sl] = h_last
        h = jnp.concatenate([(hs[s] + prods[s] * carry)[None] for s in range(n)], axis=0)
        y_scr[:, sl] = (h.reshape(r, LRU_BLOCK) * jax.nn.silu(gate)).astype(BF16)
    out = xs + _rms(_dot(y_scr[...], wout_ref[...]), g_ref[1:2, :])
    o_ref[...] = jnp.swapaxes(out.reshape(n, SUBLANES, d), 0, 1).reshape(r, d)


def _rg_sample_kernel(proj_ref, prev_ref, h0_ref, cw_ref, cb_ref, wax_ref, ba_ref, bx_ref, lam_ref,
                      y_ref, h_ref):
    r = proj_ref.shape[0]
    c = BRANCH
    g = r // SUBLANES
    u3 = proj_ref[:, :c].reshape(g, SUBLANES, c)
    prev3 = prev_ref[...].reshape(g, SUBLANES, c)
    a3, b3 = _rg_ab(u3, prev3, cw_ref, cb_ref, wax_ref, ba_ref, bx_ref, lam_ref)
    b3 = b3 + a3 * h0_ref[...].reshape(g, SUBLANES, c)
    _, h3 = _scan8(a3, b3)
    h = h3.reshape(r, c)
    h_ref[...] = h
    y_ref[...] = (h * jax.nn.silu(proj_ref[:, c:])).astype(BF16)


def _rg_weight_specs(p):
    return [_const_spec(p["conv_w"].shape), _const_spec(p["conv_b"].shape), _const_spec(p["w_ax"].shape),
            _const_spec(p["b_a"].shape), _const_spec(p["b_x"].shape), _const_spec(p["lam"].shape)]


def _rg_weights(p):
    return (p["conv_w"], p["conv_b"], p["w_ax"], p["b_a"], p["b_x"], p["lam"])


def _rg_layer_prompt(x, g, p, bsz, seq, tl):
    nt = seq // tl
    c = BRANCH
    d = x.shape[1]
    taps = CONV_W - 1
    x_spec = pl.BlockSpec((tl, d), lambda b, i: (b * nt + i, 0))
    return pl.pallas_call(
        _rg_layer_prompt_kernel,
        grid=(bsz, nt),
        in_specs=[x_spec, _const_spec(g.shape), _const_spec(p["w_in"].shape)] + _rg_weight_specs(p)
                 + [_const_spec(p["w_out"].shape)],
        out_specs=[x_spec, pl.BlockSpec((None, taps, SUBLANES, c), lambda b, i: (b, 0, 0, 0)),
                   pl.BlockSpec((None, SUBLANES, c), lambda b, i: (b, 0, 0))],
        out_shape=[jax.ShapeDtypeStruct(x.shape, F32), jax.ShapeDtypeStruct((bsz, taps, SUBLANES, c), F32),
                   jax.ShapeDtypeStruct((bsz, SUBLANES, c), F32)],
        scratch_shapes=[pltpu.VMEM((taps, SUBLANES, c), F32), pltpu.VMEM((tl, c), BF16),
                        pltpu.VMEM((SUBLANES, c), F32)],
        compiler_params=_params(2), name="rg_layer_prompt",
    )(x, g, p["w_in"], *_rg_weights(p), p["w_out"])


def _rg_core_sample(proj, prev8, h0pad, p, tm):
    t = proj.shape[0]
    c = BRANCH
    row = lambda w: pl.BlockSpec((tm, w), lambda i: (i, 0))
    return pl.pallas_call(
        _rg_sample_kernel,
        grid=(t // tm,),
        in_specs=[row(2 * c), row(c), row(c)] + _rg_weight_specs(p),
        out_specs=[row(c), row(c)],
        out_shape=[jax.ShapeDtypeStruct((t, c), BF16), jax.ShapeDtypeStruct((t, c), F32)],
        compiler_params=_params(1), name="rg_core_sample",
    )(proj, prev8, h0pad, *_rg_weights(p))


def _ssd_chunk(z, u3, prev3, dt_raw, s_read, s_write, cw_ref, cb_ref, dtb_ref, alog_ref, dexp_ref,
               ng_ref):
    q = z.shape[0]
    n = SSD_STATE
    xbc = jax.nn.silu(_conv8(u3, prev3, cw_ref[...], cb_ref[...]).reshape(q, SSD_CONV_DIM))
    xs = xbc[:, :BRANCH]
    bm = xbc[:, BRANCH:BRANCH + SSD_GROUPS * n]
    cm = xbc[:, BRANCH + SSD_GROUPS * n:]
    dt = jax.nn.softplus(dt_raw + dtb_ref[...])
    a = -jnp.exp(alog_ref[...])
    da3 = _cumsum8((dt * a).reshape(q // SUBLANES, SUBLANES, LANES))
    rows = [da3[0]]
    for j in range(1, q // SUBLANES):
        rows.append(da3[j] + jnp.broadcast_to(rows[-1][SUBLANES - 1:, :], (SUBLANES, LANES)))
    acs = jnp.concatenate(rows, axis=0) if len(rows) > 1 else rows[0]
    last = acs[q - 1:q, :]
    if q % LANES == 0:
        acs_t, dt_t = acs.T, dt.T
    else:
        pad = jnp.zeros((LANES - q, LANES), F32)
        acs_t = jnp.concatenate([acs, pad], axis=0).T[:, :q]
        dt_t = jnp.concatenate([dt, pad], axis=0).T[:, :q]
    causal = (lax.broadcasted_iota(jnp.int32, (q, q), 0) >= lax.broadcasted_iota(jnp.int32, (q, q), 1))
    lane = lax.broadcasted_iota(jnp.int32, (q, LANES), 1)
    srow = lax.broadcasted_iota(jnp.int32, (LANES, LANES), 0)
    half = SSD_HEAD_DIM
    y_pairs = []
    for g in range(SSD_GROUPS):
        bm_g = bm[:, g * n:(g + 1) * n]
        cm_g = cm[:, g * n:(g + 1) * n]
        cb_g = _dot_nt(cm_g.astype(BF16), bm_g.astype(BF16))
        for jp in range(2):
            pair = 2 * g + jp
            m_parts, cce_parts, bcw_parts, cds = [], [], [], []
            for h in (2 * pair, 2 * pair + 1):
                colb = jnp.broadcast_to(acs[:, h:h + 1], (q, LANES))
                dcol = jnp.broadcast_to(dt[:, h:h + 1], (q, LANES))
                seg = colb[:, :q] - acs_t[h:h + 1, :]
                decay = jnp.exp(jnp.where(causal, seg, -jnp.inf))
                m_parts.append(cb_g * decay * dt_t[h:h + 1, :])
                cce_parts.append(cm_g * jnp.exp(colb))
                lastb = last[:, h:h + 1]
                bcw_parts.append(bm_g * (dcol * jnp.exp(lastb - colb)))
                cds.append(jnp.exp(lastb))
            xs_pair = xs[:, pair * LANES:(pair + 1) * LANES]
            top = jnp.where(lane < half, xs_pair, 0.0)
            bot = jnp.where(lane >= half, xs_pair, 0.0)
            w = jnp.concatenate([top, bot], axis=0).astype(BF16)
            if q % LANES == 0:
                yd = _dot(jnp.concatenate(m_parts, axis=1).astype(BF16), w)
            else:
                yd = _dot(m_parts[0], top) + _dot(m_parts[1], bot)
            s_pair = s_read(pair)
            s_blk = jnp.concatenate([jnp.where(srow < half, s_pair, 0.0),
                                     jnp.where(srow >= half, s_pair, 0.0)], axis=1).astype(BF16)
            yo = _dot_nt(jnp.concatenate(cce_parts, axis=1).astype(BF16), s_blk)
            ds = _dot_tn(w, jnp.concatenate(bcw_parts, axis=0).astype(BF16))
            cd = jnp.where(srow < half, jnp.broadcast_to(cds[0], (LANES, LANES)),
                           jnp.broadcast_to(cds[1], (LANES, LANES)))
            s_write(pair, s_pair * cd + ds)
            y_pairs.append(yd + yo + dexp_ref[:, pair * LANES:(pair + 1) * LANES] * xs_pair)
    gw = BRANCH // SSD_GROUPS
    y_groups = []
    for g in range(SSD_GROUPS):
        yg = jnp.concatenate(y_pairs[2 * g:2 * g + 2], axis=1) * jax.nn.silu(z[:, g * gw:(g + 1) * gw])
        y_groups.append(yg * lax.rsqrt(jnp.mean(yg * yg, axis=-1, keepdims=True) + EPS))
    return jnp.concatenate(y_groups, axis=1) * ng_ref[...]


def _ssd_prompt_kernel(zx_ref, dt_ref, cw_ref, cb_ref, dtb_ref, alog_ref, dexp_ref, ng_ref,
                       y_ref, sout_ref, xbuf, s_scr):
    r = zx_ref.shape[0]
    q = SSD_CHUNK
    g = q // SUBLANES
    c = SSD_CONV_DIM

    @pl.when(pl.program_id(1) == 0)
    def _():
        xbuf[0:SUBLANES, :] = jnp.zeros((SUBLANES, c), F32)
        s_scr[...] = jnp.zeros_like(s_scr)

    xbuf[SUBLANES:, :] = zx_ref[:, BRANCH:]

    def s_read(pair):
        return s_scr[pair * LANES:(pair + 1) * LANES, :]

    def s_write(pair, val):
        s_scr[pair * LANES:(pair + 1) * LANES, :] = val

    for ch in range(r // q):
        rows = slice(ch * q, (ch + 1) * q)
        u3 = xbuf[SUBLANES + ch * q:SUBLANES + (ch + 1) * q, :].reshape(g, SUBLANES, c)
        prev3 = xbuf[ch * q:(ch + 1) * q, :].reshape(g, SUBLANES, c)
        y = _ssd_chunk(zx_ref[rows, :BRANCH], u3, prev3, dt_ref[rows, :], s_read, s_write, cw_ref,
                       cb_ref, dtb_ref, alog_ref, dexp_ref, ng_ref)
        y_ref[rows, :] = y.astype(BF16)
    xbuf[0:SUBLANES, :] = xbuf[r:, :]

    @pl.when(pl.program_id(1) == pl.num_programs(1) - 1)
    def _():
        sout_ref[...] = s_scr[...]


def _ssd_sample_kernel(zx_ref, dt_ref, prev_ref, s0_ref, cw_ref, cb_ref, dtb_ref, alog_ref, dexp_ref,
                       ng_ref, y_ref, sout_ref, *, seq):
    c = SSD_CONV_DIM
    ys = []
    for n in range(zx_ref.shape[0] // seq):
        rows = slice(n * seq, (n + 1) * seq)
        u3 = zx_ref[rows, BRANCH:].reshape(1, seq, c)
        prev3 = prev_ref[rows, :].reshape(1, seq, c)

        def s_read(pair, n=n):
            return s0_ref[n, pair * LANES:(pair + 1) * LANES, :]

        def s_write(pair, val, n=n):
            sout_ref[n, pair * LANES:(pair + 1) * LANES, :] = val

        y = _ssd_chunk(zx_ref[rows, :BRANCH], u3, prev3, dt_ref[rows, :], s_read, s_write, cw_ref,
                       cb_ref, dtb_ref, alog_ref, dexp_ref, ng_ref)
        ys.append(y)
    y_ref[...] = jnp.concatenate(ys, axis=0).astype(BF16)


def _ssd_weight_specs(p):
    return [_const_spec(p[k].shape) for k in ("conv_w", "conv_b", "dt_bias", "a_log", "d_exp", "norm_g")]


def _ssd_weights(p):
    return tuple(p[k] for k in ("conv_w", "conv_b", "dt_bias", "a_log", "d_exp", "norm_g"))


def _ssd_core_prompt(zx, dt, p, bsz, seq, q):
    nt = seq // q
    hp = SSD_HEADS * SSD_HEAD_DIM
    return pl.pallas_call(
        _ssd_prompt_kernel,
        grid=(bsz, nt),
        in_specs=[pl.BlockSpec((q, zx.shape[1]), lambda b, i: (b * nt + i, 0)),
                  pl.BlockSpec((q, LANES), lambda b, i: (b * nt + i, 0))] + _ssd_weight_specs(p),
        out_specs=[pl.BlockSpec((q, BRANCH), lambda b, i: (b * nt + i, 0)),
                   pl.BlockSpec((None, hp, SSD_STATE), lambda b, i: (b, 0, 0))],
        out_shape=[jax.ShapeDtypeStruct((bsz * seq, BRANCH), BF16),
                   jax.ShapeDtypeStruct((bsz, hp, SSD_STATE), F32)],
        scratch_shapes=[pltpu.VMEM((SUBLANES + q, SSD_CONV_DIM), F32), pltpu.VMEM((hp, SSD_STATE), F32)],
        compiler_params=_params(2), name="ssd_core_prompt",
    )(zx, dt, *_ssd_weights(p))


def _ssd_core_sample(zx, dt, prev8, s0, idx, p, seq, nb):
    t = zx.shape[0]
    bsz = t // seq
    hp = SSD_HEADS * SSD_HEAD_DIM
    row = lambda w: pl.BlockSpec((nb * seq, w), lambda i: (i, 0))
    st = pl.BlockSpec((nb, hp, SSD_STATE), lambda i: (i, 0, 0))
    st_in = pl.BlockSpec((None, nb, hp, SSD_STATE), lambda i: (idx, i, 0, 0))
    return pl.pallas_call(
        functools.partial(_ssd_sample_kernel, seq=seq),
        grid=(bsz // nb,),
        in_specs=[row(zx.shape[1]), row(LANES), row(SSD_CONV_DIM), st_in] + _ssd_weight_specs(p),
        out_specs=[row(BRANCH), st],
        out_shape=[jax.ShapeDtypeStruct((t, BRANCH), BF16), jax.ShapeDtypeStruct((bsz, hp, SSD_STATE), F32)],
        compiler_params=_params(1), name="ssd_core_sample",
    )(zx, dt, prev8, s0, *_ssd_weights(p))


def _hg_lower_bound(rows, layer):
    mx = functools.reduce(jnp.maximum, rows)
    es = [jnp.exp(x - mx) for x in rows]
    return sum(es[1:layer + 1]) / sum(es)


def _hg_prompt_kernel(proj_ref, hlb_ref, ng_ref, y_ref, sout_ref, st_scr, *, layer):
    r = proj_ref.shape[0]
    c = BRANCH
    dk = HGRN_KEY_DIM
    blk, sub = HGRN_BLOCK, HGRN_CHUNK
    nblk, nsub = r // blk, blk // sub

    @pl.when(pl.program_id(1) == 0)
    def _():
        st_scr[...] = jnp.zeros_like(st_scr)

    causal = (lax.broadcasted_iota(jnp.int32, (blk, blk), 0)
              >= lax.broadcasted_iota(jnp.int32, (blk, blk), 1))

    def head_body(h, carry):
        lanes = pl.ds(pl.multiple_of(h * dk, dk), dk)

        def proj_part(part):
            return proj_ref[:, pl.ds(pl.multiple_of(part * c + h * dk, dk), dk)]

        lb = _hg_lower_bound([hlb_ref[j:j + 1, lanes] for j in range(DEPTH)], layer)
        f = proj_part(1)
        forget = lb + (1.0 - lb) * jax.nn.sigmoid(f)
        k = ((1.0 - lb) * jax.nn.sigmoid(-f)).reshape(nblk, blk, dk)
        x = _cumsum8(jnp.log(forget).reshape(r // SUBLANES, SUBLANES, dk)).reshape(nblk, blk, dk)
        parts, carry_row = [], None
        for j in range(blk // SUBLANES):
            part = x[:, j * SUBLANES:(j + 1) * SUBLANES, :]
            if carry_row is not None:
                part = part + carry_row
            carry_row = jnp.broadcast_to(part[:, SUBLANES - 1:, :], part.shape)
            parts.append(part)
        gc = jnp.concatenate(parts, axis=1)
        ends = [gc[:, (i + 1) * sub - 1:(i + 1) * sub, :] for i in range(nsub)]
        starts = [jnp.zeros_like(ends[0])] + ends[:-1]
        spread = lambda rows_: jnp.concatenate(
            [jnp.broadcast_to(x_, (nblk, sub, dk)) for x_ in rows_], axis=1)
        b_prev, b_next = spread(starts), spread(ends)
        qi = jax.nn.silu(proj_part(0)).reshape(nblk, blk, dk) * jnp.exp(gc - b_prev)
        qc = qi * jnp.exp(b_prev)
        kd = k * jnp.exp(b_prev - gc)
        ke = kd * jnp.exp(b_next - b_prev)
        kend = ke * jnp.exp(ends[-1] - b_next)
        v = proj_part(2).reshape(nblk, blk, dk)

        st = st_scr[h]
        outs = []
        for b in range(nblk):
            qi_b = qi[b].astype(BF16)
            att_rows = []
            for i in range(nsub):
                keys = []
                for j in range(nsub):
                    rs = slice(j * sub, (j + 1) * sub)
                    if j == i:
                        keys.append(kd[b, rs])
                    elif j < i - 1:
                        keys.append(ke[b, rs] * jnp.exp(starts[i][b] - ends[j][b]))
                    else:
                        keys.append(ke[b, rs])
                keys = jnp.concatenate(keys, axis=0).astype(BF16)
                att_rows.append(_dot_nt(qi_b[i * sub:(i + 1) * sub], keys))
            att = jnp.where(causal, jnp.concatenate(att_rows, axis=0), 0.0).astype(BF16)
            vb = v[b].astype(BF16)
            outs.append(_dot(att, vb) + _dot_nt(qc[b].astype(BF16), st.astype(BF16)))
            st = st * jnp.exp(ends[-1][b]) + _dot_tn(vb, kend[b].astype(BF16))
        st_scr[h] = st
        o = jnp.concatenate(outs, axis=0)
        o = o * lax.rsqrt(jnp.mean(o * o, axis=-1, keepdims=True) + EPS)
        gate = jax.nn.silu(proj_part(3))
        y_ref[:, lanes] = (o * ng_ref[:, lanes] * gate).astype(BF16)
        return carry

    lax.fori_loop(0, HGRN_HEADS, head_body, 0, unroll=True)

    @pl.when(pl.program_id(1) == pl.num_programs(1) - 1)
    def _():
        for h in range(HGRN_HEADS):
            sout_ref[h] = st_scr[h].T


def _hg_seq_kernel(proj_ref, hlb_ref, ng_ref, s0_ref, y_ref, sout_ref,
                   qg_scr, kg_scr, ke_scr, v_scr, dec_scr, o_scr, *, layer, chunk):
    r = proj_ref.shape[0]
    c = BRANCH
    g = r // SUBLANES
    n_chunks = r // chunk
    dk, dv = HGRN_KEY_DIM, HGRN_VAL_DIM
    assert chunk == SUBLANES

    lb = _hg_lower_bound([hlb_ref[j:j + 1, :] for j in range(DEPTH)], layer)
    f = proj_ref[:, c:2 * c]
    forget = lb + (1.0 - lb) * jax.nn.sigmoid(f)
    k = (1.0 - lb) * jax.nn.sigmoid(-f)
    gcum3 = _cumsum8(jnp.log(forget).reshape(g, SUBLANES, c))
    gcum = gcum3.reshape(r, c)
    glast = jnp.broadcast_to(gcum3[:, SUBLANES - 1:, :], gcum3.shape).reshape(r, c)
    qg_scr[...] = jax.nn.silu(proj_ref[:, :c]) * jnp.exp(gcum)
    kg_scr[...] = k * jnp.exp(-gcum)
    ke_scr[...] = k * jnp.exp(glast - gcum)
    dec_scr[...] = jnp.exp(glast)
    v_scr[...] = proj_ref[:, 2 * c:3 * c]

    causal = (lax.broadcasted_iota(jnp.int32, (chunk, chunk), 0)
              >= lax.broadcasted_iota(jnp.int32, (chunk, chunk), 1))

    def chunk_body(ci, carry):
        rs = pl.ds(pl.multiple_of(ci * chunk, chunk), chunk)
        outs = []
        for h in range(HGRN_HEADS):
            ks = slice(h * dk, (h + 1) * dk)
            vs = slice(h * dv, (h + 1) * dv)
            qg = qg_scr[rs, ks].astype(BF16)
            vv = v_scr[rs, vs].astype(BF16)
            st = s0_ref[ci, h].T
            att = jnp.where(causal, _dot_nt(qg, kg_scr[rs, ks].astype(BF16)), 0.0)
            outs.append(_dot(att.astype(BF16), vv) + _dot_nt(qg, st.astype(BF16)))
            dec = dec_scr[rs, ks][chunk - 1:, :]
            sout_ref[ci, h] = (st * dec + _dot_tn(vv, ke_scr[rs, ks].astype(BF16))).T
        o_scr[rs, :] = jnp.concatenate(outs, axis=1)
        return carry

    lax.fori_loop(0, n_chunks, chunk_body, 0, unroll=2)

    gate = jax.nn.silu(proj_ref[:, 3 * c:])
    parts = []
    for h in range(HGRN_HEADS):
        o = o_scr[:, h * dv:(h + 1) * dv]
        parts.append(o * lax.rsqrt(jnp.mean(o * o, axis=-1, keepdims=True) + EPS))
    y_ref[...] = (jnp.concatenate(parts, axis=1) * ng_ref[...] * gate).astype(BF16)


def _hg_core_prompt(proj, hlb, ng, layer, bsz, seq, tl):
    nt = seq // tl
    c = BRANCH
    st_shape = (HGRN_HEADS, HGRN_KEY_DIM, HGRN_VAL_DIM)
    return pl.pallas_call(
        functools.partial(_hg_prompt_kernel, layer=layer),
        grid=(bsz, nt),
        in_specs=[pl.BlockSpec((tl, 4 * c), lambda b, i: (b * nt + i, 0)),
                  _const_spec(hlb.shape), _const_spec(ng.shape)],
        out_specs=[pl.BlockSpec((tl, c), lambda b, i: (b * nt + i, 0)),
                   pl.BlockSpec((None,) + st_shape, lambda b, i: (b, 0, 0, 0))],
        out_shape=[jax.ShapeDtypeStruct((bsz * seq, c), BF16),
                   jax.ShapeDtypeStruct((bsz,) + st_shape, F32)],
        scratch_shapes=[pltpu.VMEM((HGRN_HEADS, HGRN_VAL_DIM, HGRN_KEY_DIM), F32)],
        compiler_params=_params(2), name="hg_core_prompt",
    )(proj, hlb, ng)


def _hg_core_sample(proj, s0, idx, hlb, ng, layer, seq, nb):
    t = proj.shape[0]
    bsz = t // seq
    c = BRANCH
    rows = nb * seq
    st_shape = (nb, HGRN_HEADS, HGRN_KEY_DIM, HGRN_VAL_DIM)
    st_spec = pl.BlockSpec(st_shape, lambda i: (i, 0, 0, 0))
    st_in = pl.BlockSpec((None,) + st_shape, lambda i: (idx, i, 0, 0, 0))
    return pl.pallas_call(
        functools.partial(_hg_seq_kernel, layer=layer, chunk=seq),
        grid=(bsz // nb,),
        in_specs=[pl.BlockSpec((rows, 4 * c), lambda i: (i, 0)),
                  _const_spec(hlb.shape), _const_spec(ng.shape), st_in],
        out_specs=[pl.BlockSpec((rows, c), lambda i: (i, 0)), st_spec],
        out_shape=[jax.ShapeDtypeStruct((t, c), BF16), jax.ShapeDtypeStruct(s0.shape[1:], F32)],
        scratch_shapes=[pltpu.VMEM((rows, c), F32)] * 6,
        compiler_params=_params(1), name="hg_core_sample",
    )(proj, hlb, ng, s0)


ROWS_PROJ_IN = 256
ROWS_PROJ_OUT = 512
ROWS_RG_PROMPT = 1024
ROWS_SSD_PROMPT = 2 * SSD_CHUNK
ROWS_HG_PROMPT = 4 * HGRN_BLOCK
ROWS_ATTN_PROMPT = 1024
SEQS_SSD_SAMPLE = 4
SEQS_HG_SAMPLE = 8
SEQS_ATTN_SAMPLE = 4
BATCH_MEM_KV = 2


def _tile(n, target):
    t = min(n, target)
    assert n % t == 0, (n, target)
    return t


def _pad_groups(state, first_row):
    n, k, c = state.shape
    return jnp.pad(state, ((0, 0), (first_row, SUBLANES - first_row - k), (0, 0))).reshape(n * SUBLANES, c)


def _trunk(x, mem_k, mem_v, states, w, bsz, seq, prompt):
    tm = _tile(x.shape[0], ROWS_PROJ_IN)
    rg_conv, rg_h, ssd_conv, ssd_s, hg_s = [], [], [], [], []
    tail = slice(seq - (CONV_W - 1), seq)
    for layer in range(DEPTH):
        kind, idx = layer % N_MIXERS, layer // N_MIXERS
        g = w["norm_g"][layer]
        if kind == 0:
            p = w["rg"][idx]
            if prompt:
                x, utail, htail = _rg_layer_prompt(x, g, p, bsz, seq, _tile(seq, ROWS_RG_PROMPT))
                rg_h.append(htail[:, SUBLANES - 1])
                rg_conv.append(utail[:, :, 0])
                y = None
            else:
                (proj,) = _norm_matmul(x, g[0:1], [p["w_in"]], tm)
                prev8 = _pad_groups(states["rg_conv"][idx], SUBLANES - (CONV_W - 1))
                h0pad = _pad_groups(states["rg_h"][idx][:, None, :], 0)
                y, h = _rg_core_sample(proj, prev8, h0pad, p, tm)
                rg_h.append(h.reshape(bsz, seq, BRANCH)[:, seq - 1])
                rg_conv.append(proj.reshape(bsz, seq, 2 * BRANCH)[:, tail, :BRANCH])
        elif kind == 1:
            p = w["ssd"][idx]
            zx, dt = _norm_matmul(x, g[0:1], [p["w_zx"], p["w_dt"]], tm)
            if prompt:
                y, s_new = _ssd_core_prompt(zx, dt, p, bsz, seq, _tile(seq, ROWS_SSD_PROMPT))
            else:
                prev8 = _pad_groups(states["ssd_conv"][idx], SUBLANES - (CONV_W - 1))
                s0 = states["ssd_s"].reshape(-1, bsz, SSD_HEADS * SSD_HEAD_DIM, SSD_STATE)
                y, s_new = _ssd_core_sample(zx, dt, prev8, s0, idx, p, seq, _tile(bsz, SEQS_SSD_SAMPLE))
            ssd_s.append(s_new.reshape(bsz, SSD_HEADS, SSD_HEAD_DIM, SSD_STATE))
            ssd_conv.append(zx.reshape(bsz, seq, BRANCH + SSD_CONV_DIM)[:, tail, BRANCH:])
        else:
            p = w["hg"][idx]
            (proj,) = _norm_matmul(x, g[0:1], [p["w_in"]], tm)
            if prompt:
                y, s_new = _hg_core_prompt(proj, w["hg_lower_bounds"], p["norm_g"], layer, bsz, seq,
                                           _tile(seq, ROWS_HG_PROMPT))
            else:
                y, s_new = _hg_core_sample(proj, states["hg_s"], idx, w["hg_lower_bounds"], p["norm_g"],
                                           layer, seq, _tile(bsz, SEQS_HG_SAMPLE))
            hg_s.append(s_new)
        if y is not None:
            x = _proj_norm_res(y, p["w_out"], g[1:2], x, _tile(x.shape[0], ROWS_PROJ_OUT))
        if prompt:
            x = _attn_prompt(x, mem_k, mem_v, layer, w["x_w_q"][layer], w["x_w_o"][layer], g,
                             bsz, seq, _tile(seq, ROWS_ATTN_PROMPT))
        else:
            x = _attn_sample(x, mem_k, mem_v, layer, w["x_w_q"][layer], w["x_w_o"][layer], g,
                             seq, _tile(bsz, SEQS_ATTN_SAMPLE))
    return x, jnp.stack(rg_conv), jnp.stack(rg_h), jnp.stack(ssd_conv), jnp.stack(ssd_s), jnp.stack(hg_s)


def kernel(x_prompt, x_sample, mem_prompt, state_rglru_conv, state_rglru_h, state_ssd_conv, state_ssd,
           state_hgrn, cache_mem_k, cache_mem_v, norm_g, mem_norm_g, rg_w_in, rg_conv_w, rg_conv_b, rg_w_a,
           rg_b_a, rg_w_x, rg_b_x, rg_lambda, rg_w_out, ssd_w_in, ssd_conv_w, ssd_conv_b, ssd_dt_bias,
           ssd_a_log, ssd_d, ssd_norm_g, ssd_w_out, hg_w_in, hg_lower_bounds, hg_norm_g, hg_w_out,
           x_w_q, x_w_k, x_w_v, x_w_o):
    bp, sp, d = x_prompt.shape
    bs, ss, _ = x_sample.shape
    n_a, n_b, n_c = rg_w_in.shape[0], ssd_w_in.shape[0], hg_w_in.shape[0]
    pad_heads = lambda v: jnp.pad(v, (0, LANES - SSD_HEADS))[None, :]
    w = {
        "norm_g": norm_g,
        "hg_lower_bounds": hg_lower_bounds,
        "x_w_q": x_w_q.astype(BF16),
        "x_w_o": x_w_o.astype(BF16),
        "rg": [{
            "w_in": rg_w_in[i].astype(BF16),
            "conv_w": rg_conv_w[i], "conv_b": rg_conv_b[i][None, :],
            "w_ax": jnp.concatenate([rg_w_a[i], rg_w_x[i]], axis=-1).astype(BF16),
            "b_a": rg_b_a[i][None, :], "b_x": rg_b_x[i][None, :], "lam": rg_lambda[i][None, :],
            "w_out": rg_w_out[i].astype(BF16),
        } for i in range(n_a)],
        "ssd": [{
            "w_zx": ssd_w_in[i][:, :BRANCH + SSD_CONV_DIM].astype(BF16),
            "w_dt": jnp.pad(ssd_w_in[i][:, BRANCH + SSD_CONV_DIM:], ((0, 0), (0, LANES - SSD_HEADS))).astype(BF16),
            "conv_w": ssd_conv_w[i], "conv_b": ssd_conv_b[i][None, :],
            "dt_bias": pad_heads(ssd_dt_bias[i]), "a_log": pad_heads(ssd_a_log[i]),
            "d_exp": jnp.repeat(ssd_d[i], SSD_HEAD_DIM)[None, :],
            "norm_g": ssd_norm_g[i][None, :],
            "w_out": ssd_w_out[i].astype(BF16),
        } for i in range(n_b)],
        "hg": [{
            "w_in": hg_w_in[i].astype(BF16),
            "norm_g": hg_norm_g[i][None, :],
            "w_out": hg_w_out[i].astype(BF16),
        } for i in range(n_c)],
    }

    mem_k_p, mem_v_p = _mem_kv(mem_prompt.reshape(bp * N_MEM, d), mem_norm_g[:, None, :],
                               x_w_k.astype(BF16), x_w_v.astype(BF16), _tile(bp, BATCH_MEM_KV))
    y_p, rgc_p, rgh_p, sc_p, ss_p, hs_p = _trunk(
        x_prompt.reshape(bp * sp, d), mem_k_p, mem_v_p, None, w, bp, sp, True)
    states = {"rg_conv": state_rglru_conv, "rg_h": state_rglru_h, "ssd_conv": state_ssd_conv,
              "ssd_s": state_ssd, "hg_s": state_hgrn}
    y_s, rgc_s, rgh_s, sc_s, ss_s, hs_s = _trunk(
        x_sample.reshape(bs * ss, d), _kv_flat(cache_mem_k), _kv_flat(cache_mem_v), states, w, bs, ss, False)
    return (y_p.reshape(bp, sp, d), y_s.reshape(bs, ss, d), rgc_p, rgh_p, sc_p, ss_p, hs_p,
            _kv_unflat(mem_k_p), _kv_unflat(mem_v_p), rgc_s, rgh_s, sc_s, ss_s, hs_s)
```

```python
import functools

import jax
import jax.numpy as jnp
from jax import lax
from jax.experimental import pallas as pl
from jax.experimental.pallas import tpu as pltpu

F32 = jnp.float32
BF16 = jnp.bfloat16

D_MODEL = 1024
DEPTH = 4
N_MIXERS = 3
BRANCH = 2 * D_MODEL
CONV_W = 4
EPS = 1e-6
LRU_BLOCKS = 8
LRU_BLOCK = BRANCH // LRU_BLOCKS
LRU_C = 8.0
SSD_HEAD_DIM = 64
SSD_HEADS = BRANCH // SSD_HEAD_DIM
SSD_STATE = 128
SSD_GROUPS = 8
SSD_CONV_DIM = BRANCH + 2 * SSD_GROUPS * SSD_STATE
SSD_CHUNK = 128
HGRN_KEY_DIM = 128
HGRN_HEADS = BRANCH // HGRN_KEY_DIM
HGRN_VAL_DIM = BRANCH // HGRN_HEADS
HGRN_CHUNK = 16
HGRN_BLOCK = 4 * HGRN_CHUNK
N_MEM = 256
X_HEADS = 4
X_HEAD_DIM = D_MODEL // X_HEADS

SUBLANES = 8
LANES = 128
VMEM_BYTES_V7X = 64 * 1024 * 1024
VMEM_LIMIT = VMEM_BYTES_V7X * 7 // 8

NT_DIMS = (((1,), (1,)), ((), ()))
TN_DIMS = (((0,), (0,)), ((), ()))


def _params(n_grid_dims):
    return pltpu.CompilerParams(
        dimension_semantics=("arbitrary",) * n_grid_dims, vmem_limit_bytes=VMEM_LIMIT)


def _const_spec(shape):
    nd = len(shape)
    return pl.BlockSpec(shape, lambda *_: (0,) * nd, pipeline_mode=pl.Buffered(1))


def _rms(x, g):
    return x * lax.rsqrt(jnp.mean(x * x, axis=-1, keepdims=True) + EPS) * g


def _dot(a, b):
    return jnp.dot(a, b, preferred_element_type=F32)


def _dot_nt(a, b):
    return lax.dot_general(a, b, NT_DIMS, preferred_element_type=F32)


def _dot_tn(a, b):
    return lax.dot_general(a, b, TN_DIMS, preferred_element_type=F32)


def _group_iota(width):
    return lax.broadcasted_iota(jnp.int32, (1, SUBLANES, width), 1)


def _conv8(u3, prev3, cw, cb):
    t = _group_iota(u3.shape[-1])
    acc = cb + cw[CONV_W - 1:CONV_W, :] * u3
    for k in range(1, CONV_W):
        shifted = pltpu.roll(jnp.where(t >= SUBLANES - k, prev3, u3), k, 1)
        acc = acc + cw[CONV_W - 1 - k:CONV_W - k, :] * shifted
    return acc


def _scan8(a3, b3):
    t = _group_iota(a3.shape[-1])
    for s in (1, 2, 4):
        m = t >= s
        a_sh = pltpu.roll(a3, s, 1)
        b_sh = pltpu.roll(b3, s, 1)
        b3 = jnp.where(m, a3 * b_sh + b3, b3)
        a3 = jnp.where(m, a3 * a_sh, a3)
    return a3, b3


def _cumsum8(x3):
    t = _group_iota(x3.shape[-1])
    for s in (1, 2, 4):
        x3 = x3 + jnp.where(t >= s, pltpu.roll(x3, s, 1), 0.0)
    return x3


def _norm_matmul_kernel(x_ref, g_ref, *refs, n_chunk):
    n_w = len(refs) // 2
    h = _rms(x_ref[...], g_ref[...]).astype(BF16)
    for w_ref, o_ref in zip(refs[:n_w], refs[n_w:]):
        n = o_ref.shape[-1]
        step = min(n_chunk, n)
        for c in range(0, n, step):
            o_ref[:, c:c + step] = _dot(h, w_ref[:, c:c + step])


def _norm_matmul(x, g, ws, tm):
    t, d = x.shape
    grid = (t // tm,)
    in_specs = [pl.BlockSpec((tm, d), lambda i: (i, 0)), _const_spec((1, d))]
    in_specs += [_const_spec(w.shape) for w in ws]
    out_specs = [pl.BlockSpec((tm, w.shape[1]), lambda i: (i, 0)) for w in ws]
    out_shape = [jax.ShapeDtypeStruct((t, w.shape[1]), F32) for w in ws]
    return pl.pallas_call(
        functools.partial(_norm_matmul_kernel, n_chunk=512),
        grid=grid, in_specs=in_specs, out_specs=out_specs, out_shape=out_shape,
        compiler_params=_params(1), name="norm_matmul",
    )(x, g, *ws)


def _proj_norm_res_kernel(a_ref, w_ref, g_ref, x_ref, o_ref):
    y = _dot(a_ref[...], w_ref[...])
    o_ref[...] = x_ref[...] + _rms(y, g_ref[...])


def _proj_norm_res(a, w, g, x, tm):
    t, k = a.shape
    d = x.shape[1]
    return pl.pallas_call(
        _proj_norm_res_kernel,
        grid=(t // tm,),
        in_specs=[pl.BlockSpec((tm, k), lambda i: (i, 0)), _const_spec(w.shape), _const_spec((1, d)),
                  pl.BlockSpec((tm, d), lambda i: (i, 0))],
        out_specs=pl.BlockSpec((tm, d), lambda i: (i, 0)),
        out_shape=jax.ShapeDtypeStruct((t, d), F32),
        compiler_params=_params(1), name="proj_norm_res",
    )(a, w, g, x)


KV_LANE_TILES = X_HEAD_DIM // LANES
KV_ROWS = KV_LANE_TILES * X_HEADS


def _kv_flat(kv):
    lead = kv.shape[:-3]
    x = kv.reshape(lead + (N_MEM, X_HEADS, KV_LANE_TILES, LANES))
    return jnp.swapaxes(x, -3, -2).reshape(lead + (N_MEM * KV_ROWS, LANES))


def _kv_unflat(flat):
    lead = flat.shape[:-2]
    x = flat.reshape(lead + (N_MEM, KV_LANE_TILES, X_HEADS, LANES))
    return jnp.swapaxes(x, -3, -2).reshape(lead + (N_MEM, X_HEADS, X_HEAD_DIM))


def _mem_kv_kernel(m_ref, g_ref, wk_ref, wv_ref, k_ref, v_ref):
    h = _rms(m_ref[...], g_ref[...]).astype(BF16)
    tm = m_ref.shape[0]
    for w_ref, o_ref in ((wk_ref, k_ref), (wv_ref, v_ref)):
        y = _dot(h, w_ref[...])
        pieces = [y[:, hd * X_HEAD_DIM + t * LANES:hd * X_HEAD_DIM + (t + 1) * LANES][None]
                  for t in range(KV_LANE_TILES) for hd in range(X_HEADS)]
        rows = jnp.swapaxes(jnp.concatenate(pieces, axis=0), 0, 1)
        o_ref[...] = rows.reshape(o_ref.shape)


def _mem_kv(mem, g, wk, wv, nb):
    t, d = mem.shape
    bsz = t // N_MEM
    tm = nb * N_MEM
    w_spec = pl.BlockSpec((None, d, d), lambda l, i: (l, 0, 0))
    o_spec = pl.BlockSpec((None, nb, N_MEM * KV_ROWS, LANES), lambda l, i: (l, i, 0, 0))
    return pl.pallas_call(
        _mem_kv_kernel,
        grid=(DEPTH, bsz // nb),
        in_specs=[pl.BlockSpec((tm, d), lambda l, i: (i, 0)),
                  pl.BlockSpec((None, 1, d), lambda l, i: (l, 0, 0)), w_spec, w_spec],
        out_specs=[o_spec, o_spec],
        out_shape=[jax.ShapeDtypeStruct((DEPTH, bsz, N_MEM * KV_ROWS, LANES), F32)] * 2,
        compiler_params=_params(2), name="mem_kv",
    )(mem, g, wk, wv)


def _attn_kernel(x_ref, k_ref, v_ref, wq_ref, wo_ref, g_ref, o_ref, q_scr, a_scr, k_scr, v_scr, *,
                 rows, hoist, seq_rows):
    i = pl.program_id(0) if hoist else None
    scale = X_HEAD_DIM ** -0.5
    n_keys = k_scr.shape[1]

    def split_heads():
        def every_kv_row(ref, first):
            rows = pl.ds(first, N_MEM, stride=KV_ROWS)
            if len(ref.shape) == 2:
                return ref[rows, :]
            return ref[:, rows, :].reshape(n_keys, LANES)

        for ref, scr in ((k_ref, k_scr), (v_ref, v_scr)):
            for h in range(X_HEADS):
                scr[h] = jnp.concatenate([every_kv_row(ref, t * X_HEADS + h) for t in range(KV_LANE_TILES)],
                                         axis=1).astype(BF16)

    if hoist:
        split_heads()
    else:
        pl.when(pl.program_id(1) == 0)(split_heads)

    def project_q():
        h = _rms(x_ref[...], g_ref[2:3, :]).astype(BF16)
        q_scr[...] = (_dot(h, wq_ref[...]) * scale).astype(BF16)

    def project_out():
        y = _dot(a_scr[...], wo_ref[...])
        o_ref[...] = x_ref[...] + _rms(y, g_ref[3:4, :])

    if hoist:
        pl.when(i == 0)(project_q)
        r0 = pl.multiple_of(i * rows, rows)
        rsl = pl.ds(r0, rows)
    else:
        project_q()
        rsl = slice(None)

    if seq_rows is not None:
        qi = lax.broadcasted_iota(jnp.int32, (rows, n_keys), 0) // seq_rows
        ki = lax.broadcasted_iota(jnp.int32, (rows, n_keys), 1) // N_MEM
        mask = qi == ki
    for h in range(X_HEADS):
        hs = slice(h * X_HEAD_DIM, (h + 1) * X_HEAD_DIM)
        qh = q_scr[rsl, hs]
        s = _dot_nt(qh, k_scr[h])
        if seq_rows is not None:
            s = jnp.where(mask, s, -jnp.inf)
        e = jnp.exp(s - jnp.max(s, axis=-1, keepdims=True))
        p = e / jnp.sum(e, axis=-1, keepdims=True)
        a_scr[rsl, hs] = _dot(p.astype(BF16), v_scr[h]).astype(BF16)

    if hoist:
        pl.when(i == pl.num_programs(0) - 1)(project_out)
    else:
        project_out()


def _attn_prompt(x, k, v, layer, wq, wo, g, bsz, seq, tl):
    d = x.shape[1]
    nt = seq // tl
    kv_spec = pl.BlockSpec((None, None, N_MEM * KV_ROWS, LANES), lambda b, i: (layer, b, 0, 0))
    kv_scr = pltpu.VMEM((X_HEADS, N_MEM, X_HEAD_DIM), BF16)
    return pl.pallas_call(
        functools.partial(_attn_kernel, rows=tl, hoist=False, seq_rows=None),
        grid=(bsz, nt),
        in_specs=[pl.BlockSpec((tl, d), lambda b, i: (b * nt + i, 0)), kv_spec, kv_spec,
                  _const_spec(wq.shape), _const_spec(wo.shape), _const_spec(g.shape)],
        out_specs=pl.BlockSpec((tl, d), lambda b, i: (b * nt + i, 0)),
        out_shape=jax.ShapeDtypeStruct(x.shape, F32),
        scratch_shapes=[pltpu.VMEM((tl, d), BF16), pltpu.VMEM((tl, d), BF16), kv_scr, kv_scr],
        compiler_params=_params(2), name="attn_prompt",
    )(x, k, v, wq, wo, g)


def _attn_sample(x, k, v, layer, wq, wo, g, seq, nb):
    t, d = x.shape
    bsz = t // seq
    kv_spec = pl.BlockSpec((None, nb, N_MEM * KV_ROWS, LANES), lambda i: (layer, i, 0, 0))
    kv_scr = pltpu.VMEM((X_HEADS, nb * N_MEM, X_HEAD_DIM), BF16)
    return pl.pallas_call(
        functools.partial(_attn_kernel, rows=nb * seq, hoist=True, seq_rows=seq),
        grid=(bsz // nb,),
        in_specs=[_const_spec(x.shape), kv_spec, kv_spec,
                  _const_spec(wq.shape), _const_spec(wo.shape), _const_spec(g.shape)],
        out_specs=pl.BlockSpec(x.shape, lambda i: (0, 0)),
        out_shape=jax.ShapeDtypeStruct(x.shape, F32),
        scratch_shapes=[pltpu.VMEM((t, d), BF16), pltpu.VMEM((t, d), BF16), kv_scr, kv_scr],
        compiler_params=_params(1), name="attn_sample",
    )(x, k, v, wq, wo, g)


def _rg_gates(conv, blk, wax_ref, ba_ref, bx_ref, lam_ref):
    sl = slice(blk * LRU_BLOCK, (blk + 1) * LRU_BLOCK)
    pre = _dot(conv.astype(BF16), wax_ref[blk])
    rg = jax.nn.sigmoid(pre[:, :LRU_BLOCK] + ba_ref[:, sl])
    ig = jax.nn.sigmoid(pre[:, LRU_BLOCK:] + bx_ref[:, sl])
    a = jnp.exp(rg * (-LRU_C * jax.nn.softplus(-lam_ref[:, sl])))
    b = jnp.exp(0.5 * jnp.log(1.0 - a * a)) * (ig * conv)
    return a, b


def _rg_block_ab(u3, prev3, blk, cw_ref, cb_ref, wax_ref, ba_ref, bx_ref, lam_ref):
    g = u3.shape[0]
    sl = slice(blk * LRU_BLOCK, (blk + 1) * LRU_BLOCK)
    conv = _conv8(u3, prev3, cw_ref[:, sl], cb_ref[:, sl]).reshape(g * SUBLANES, LRU_BLOCK)
    a, b = _rg_gates(conv, blk, wax_ref, ba_ref, bx_ref, lam_ref)
    return a.reshape(g, SUBLANES, LRU_BLOCK), b.reshape(g, SUBLANES, LRU_BLOCK)


def _rg_ab(u3, prev3, cw_ref, cb_ref, wax_ref, ba_ref, bx_ref, lam_ref):
    parts = [_rg_block_ab(u3[:, :, blk * LRU_BLOCK:(blk + 1) * LRU_BLOCK],
                          prev3[:, :, blk * LRU_BLOCK:(blk + 1) * LRU_BLOCK],
                          blk, cw_ref, cb_ref, wax_ref, ba_ref, bx_ref, lam_ref)
             for blk in range(LRU_BLOCKS)]
    return (jnp.concatenate([p[0] for p in parts], axis=2), jnp.concatenate([p[1] for p in parts], axis=2))


def _rg_layer_prompt_kernel(x_ref, g_ref, win_ref, cw_ref, cb_ref, wax_ref, ba_ref, bx_ref, lam_ref,
                            wout_ref, o_ref, utail_ref, htail_ref, tail_scr, y_scr, hc):
    r, d = x_ref.shape
    c = BRANCH
    n = r // SUBLANES
    taps = CONV_W - 1

    @pl.when(pl.program_id(1) == 0)
    def _():
        tail_scr[...] = jnp.zeros_like(tail_scr)
        hc[...] = jnp.zeros_like(hc)

    xs = jnp.swapaxes(x_ref[...].reshape(SUBLANES, n, d), 0, 1).reshape(r, d)
    xn = _rms(xs, g_ref[0:1, :]).astype(BF16)
    first_seg = lax.broadcasted_iota(jnp.int32, (SUBLANES, LRU_BLOCK), 0) == 0
    for blk in range(LRU_BLOCKS):
        sl = slice(blk * LRU_BLOCK, (blk + 1) * LRU_BLOCK)
        u3 = _dot(xn, win_ref[:, sl]).reshape(n, SUBLANES, LRU_BLOCK)
        gate = _dot(xn, win_ref[:, c + blk * LRU_BLOCK:c + (blk + 1) * LRU_BLOCK])
        hist = []
        for j in range(taps):
            prev_seg = pltpu.roll(u3[n - taps + j], 1, 0)
            hist.append(jnp.where(first_seg, tail_scr[j, :, sl], prev_seg)[None])
            tail_scr[j, :, sl] = prev_seg
            utail_ref[j, :, sl] = prev_seg
        uext = jnp.concatenate(hist + [u3], axis=0)
        conv = cb_ref[:, sl] + cw_ref[taps:CONV_W, sl] * u3
        for k in range(1, CONV_W):
            conv = conv + cw_ref[taps - k:CONV_W - k, sl] * uext[taps - k:taps - k + n]
        a, b = _rg_gates(conv.reshape(r, LRU_BLOCK), blk, wax_ref, ba_ref, bx_ref, lam_ref)
        a3 = a.reshape(n, SUBLANES, LRU_BLOCK)
        b3 = b.reshape(n, SUBLANES, LRU_BLOCK)
        hs, prods = [b3[0]], [a3[0]]
        for s in range(1, n):
            hs.append(a3[s] * hs[-1] + b3[s])
            prods.append(a3[s] * prods[-1])
        h_in = hc[0:1, sl]
        carry_rows = []
        for q in range(SUBLANES):
            carry_rows.append(h_in)
            h_in = hs[-1][q:q + 1] + prods[-1][q:q + 1] * h_in
        carry = jnp.concatenate(carry_rows, axis=0)
        h_last = jnp.broadcast_to(h_in, (SUBLANES, LRU_BLOCK))
        hc[:, sl] = h_last
        htail_ref[:, sl] = h_last
        h = jnp.concatenate([(hs[s] + prods[s] * carry)[None] for s in range(n)], axis=0)
        y_scr[:, sl] = (h.reshape(r, LRU_BLOCK) * jax.nn.silu(gate)).astype(BF16)
    out = xs + _rms(_dot(y_scr[...], wout_ref[...]), g_ref[1:2, :])
    o_ref[...] = jnp.swapaxes(out.reshape(n, SUBLANES, d), 0, 1).reshape(r, d)


def _rg_sample_kernel(proj_ref, prev_ref, h0_ref, cw_ref, cb_ref, wax_ref, ba_ref, bx_ref, lam_ref,
                      y_ref, h_ref):
    r = proj_ref.shape[0]
    c = BRANCH
    g = r // SUBLANES
    u3 = proj_ref[:, :c].reshape(g, SUBLANES, c)
    prev3 = prev_ref[...].reshape(g, SUBLANES, c)
    a3, b3 = _rg_ab(u3, prev3, cw_ref, cb_ref, wax_ref, ba_ref, bx_ref, lam_ref)
    b3 = b3 + a3 * h0_ref[...].reshape(g, SUBLANES, c)
    _, h3 = _scan8(a3, b3)
    h = h3.reshape(r, c)
    h_ref[...] = h
    y_ref[...] = (h * jax.nn.silu(proj_ref[:, c:])).astype(BF16)


def _rg_weight_specs(p):
    return [_const_spec(p["conv_w"].shape), _const_spec(p["conv_b"].shape), _const_spec(p["w_ax"].shape),
            _const_spec(p["b_a"].shape), _const_spec(p["b_x"].shape), _const_spec(p["lam"].shape)]


def _rg_weights(p):
    return (p["conv_w"], p["conv_b"], p["w_ax"], p["b_a"], p["b_x"], p["lam"])


def _rg_layer_prompt(x, g, p, bsz, seq, tl):
    nt = seq // tl
    c = BRANCH
    d = x.shape[1]
    taps = CONV_W - 1
    x_spec = pl.BlockSpec((tl, d), lambda b, i: (b * nt + i, 0))
    return pl.pallas_call(
        _rg_layer_prompt_kernel,
        grid=(bsz, nt),
        in_specs=[x_spec, _const_spec(g.shape), _const_spec(p["w_in"].shape)] + _rg_weight_specs(p)
                 + [_const_spec(p["w_out"].shape)],
        out_specs=[x_spec, pl.BlockSpec((None, taps, SUBLANES, c), lambda b, i: (b, 0, 0, 0)),
                   pl.BlockSpec((None, SUBLANES, c), lambda b, i: (b, 0, 0))],
        out_shape=[jax.ShapeDtypeStruct(x.shape, F32), jax.ShapeDtypeStruct((bsz, taps, SUBLANES, c), F32),
                   jax.ShapeDtypeStruct((bsz, SUBLANES, c), F32)],
        scratch_shapes=[pltpu.VMEM((taps, SUBLANES, c), F32), pltpu.VMEM((tl, c), BF16),
                        pltpu.VMEM((SUBLANES, c), F32)],
        compiler_params=_params(2), name="rg_layer_prompt",
    )(x, g, p["w_in"], *_rg_weights(p), p["w_out"])


def _rg_core_sample(proj, prev8, h0pad, p, tm):
    t = proj.shape[0]
    c = BRANCH
    row = lambda w: pl.BlockSpec((tm, w), lambda i: (i, 0))
    return pl.pallas_call(
        _rg_sample_kernel,
        grid=(t // tm,),
        in_specs=[row(2 * c), row(c), row(c)] + _rg_weight_specs(p),
        out_specs=[row(c), row(c)],
        out_shape=[jax.ShapeDtypeStruct((t, c), BF16), jax.ShapeDtypeStruct((t, c), F32)],
        compiler_params=_params(1), name="rg_core_sample",
    )(proj, prev8, h0pad, *_rg_weights(p))


def _ssd_chunk(z, u3, prev3, dt_raw, s_read, s_write, cw_ref, cb_ref, dtb_ref, alog_ref, dexp_ref,
               ng_ref):
    q = z.shape[0]
    n = SSD_STATE
    xbc = jax.nn.silu(_conv8(u3, prev3, cw_ref[...], cb_ref[...]).reshape(q, SSD_CONV_DIM))
    xs = xbc[:, :BRANCH]
    bm = xbc[:, BRANCH:BRANCH + SSD_GROUPS * n]
    cm = xbc[:, BRANCH + SSD_GROUPS * n:]
    dt = jax.nn.softplus(dt_raw + dtb_ref[...])
    a = -jnp.exp(alog_ref[...])
    da3 = _cumsum8((dt * a).reshape(q // SUBLANES, SUBLANES, LANES))
    rows = [da3[0]]
    for j in range(1, q // SUBLANES):
        rows.append(da3[j] + jnp.broadcast_to(rows[-1][SUBLANES - 1:, :], (SUBLANES, LANES)))
    acs = jnp.concatenate(rows, axis=0) if len(rows) > 1 else rows[0]
    last = acs[q - 1:q, :]
    if q % LANES == 0:
        acs_t, dt_t = acs.T, dt.T
    else:
        pad = jnp.zeros((LANES - q, LANES), F32)
        acs_t = jnp.concatenate([acs, pad], axis=0).T[:, :q]
        dt_t = jnp.concatenate([dt, pad], axis=0).T[:, :q]
    causal = (lax.broadcasted_iota(jnp.int32, (q, q), 0) >= lax.broadcasted_iota(jnp.int32, (q, q), 1))
    lane = lax.broadcasted_iota(jnp.int32, (q, LANES), 1)
    srow = lax.broadcasted_iota(jnp.int32, (LANES, LANES), 0)
    half = SSD_HEAD_DIM
    y_pairs = []
    for g in range(SSD_GROUPS):
        bm_g = bm[:, g * n:(g + 1) * n]
        cm_g = cm[:, g * n:(g + 1) * n]
        cb_g = _dot_nt(cm_g.astype(BF16), bm_g.astype(BF16))
        for jp in range(2):
            pair = 2 * g + jp
            m_parts, cce_parts, bcw_parts, cds = [], [], [], []
            for h in (2 * pair, 2 * pair + 1):
                colb = jnp.broadcast_to(acs[:, h:h + 1], (q, LANES))
                dcol = jnp.broadcast_to(dt[:, h:h + 1], (q, LANES))
                seg = colb[:, :q] - acs_t[h:h + 1, :]
                decay = jnp.exp(jnp.where(causal, seg, -jnp.inf))
                m_parts.append(cb_g * decay * dt_t[h:h + 1, :])
                cce_parts.append(cm_g * jnp.exp(colb))
                lastb = last[:, h:h + 1]
                bcw_parts.append(bm_g * (dcol * jnp.exp(lastb - colb)))
                cds.append(jnp.exp(lastb))
            xs_pair = xs[:, pair * LANES:(pair + 1) * LANES]
            top = jnp.where(lane < half, xs_pair, 0.0)
            bot = jnp.where(lane >= half, xs_pair, 0.0)
            w = jnp.concatenate([top, bot], axis=0).astype(BF16)
            if q % LANES == 0:
                yd = _dot(jnp.concatenate(m_parts, axis=1).astype(BF16), w)
            else:
                yd = _dot(m_parts[0], top) + _dot(m_parts[1], bot)
            s_pair = s_read(pair)
            s_blk = jnp.concatenate([jnp.where(srow < half, s_pair, 0.0),
                                     jnp.where(srow >= half, s_pair, 0.0)], axis=1).astype(BF16)
            yo = _dot_nt(jnp.concatenate(cce_parts, axis=1).astype(BF16), s_blk)
            ds = _dot_tn(w, jnp.concatenate(bcw_parts, axis=0).astype(BF16))
            cd = jnp.where(srow < half, jnp.broadcast_to(cds[0], (LANES, LANES)),
                           jnp.broadcast_to(cds[1], (LANES, LANES)))
            s_write(pair, s_pair * cd + ds)
            y_pairs.append(yd + yo + dexp_ref[:, pair * LANES:(pair + 1) * LANES] * xs_pair)
    gw = BRANCH // SSD_GROUPS
    y_groups = []
    for g in range(SSD_GROUPS):
        yg = jnp.concatenate(y_pairs[2 * g:2 * g + 2], axis=1) * jax.nn.silu(z[:, g * gw:(g + 1) * gw])
        y_groups.append(yg * lax.rsqrt(jnp.mean(yg * yg, axis=-1, keepdims=True) + EPS))
    return jnp.concatenate(y_groups, axis=1) * ng_ref[...]


def _ssd_prompt_kernel(zx_ref, dt_ref, cw_ref, cb_ref, dtb_ref, alog_ref, dexp_ref, ng_ref,
                       y_ref, sout_ref, xbuf, s_scr):
    r = zx_ref.shape[0]
    q = SSD_CHUNK
    g = q // SUBLANES
    c = SSD_CONV_DIM

    @pl.when(pl.program_id(1) == 0)
    def _():
        xbuf[0:SUBLANES, :] = jnp.zeros((SUBLANES, c), F32)
        s_scr[...] = jnp.zeros_like(s_scr)

    xbuf[SUBLANES:, :] = zx_ref[:, BRANCH:]

    def s_read(pair):
        return s_scr[pair * LANES:(pair + 1) * LANES, :]

    def s_write(pair, val):
        s_scr[pair * LANES:(pair + 1) * LANES, :] = val

    for ch in range(r // q):
        rows = slice(ch * q, (ch + 1) * q)
        u3 = xbuf[SUBLANES + ch * q:SUBLANES + (ch + 1) * q, :].reshape(g, SUBLANES, c)
        prev3 = xbuf[ch * q:(ch + 1) * q, :].reshape(g, SUBLANES, c)
        y = _ssd_chunk(zx_ref[rows, :BRANCH], u3, prev3, dt_ref[rows, :], s_read, s_write, cw_ref,
                       cb_ref, dtb_ref, alog_ref, dexp_ref, ng_ref)
        y_ref[rows, :] = y.astype(BF16)
    xbuf[0:SUBLANES, :] = xbuf[r:, :]

    @pl.when(pl.program_id(1) == pl.num_programs(1) - 1)
    def _():
        sout_ref[...] = s_scr[...]


def _ssd_sample_kernel(zx_ref, dt_ref, prev_ref, s0_ref, cw_ref, cb_ref, dtb_ref, alog_ref, dexp_ref,
                       ng_ref, y_ref, sout_ref, *, seq):
    c = SSD_CONV_DIM
    ys = []
    for n in range(zx_ref.shape[0] // seq):
        rows = slice(n * seq, (n + 1) * seq)
        u3 = zx_ref[rows, BRANCH:].reshape(1, seq, c)
        prev3 = prev_ref[rows, :].reshape(1, seq, c)

        def s_read(pair, n=n):
            return s0_ref[n, pair * LANES:(pair + 1) * LANES, :]

        def s_write(pair, val, n=n):
            sout_ref[n, pair * LANES:(pair + 1) * LANES, :] = val

        y = _ssd_chunk(zx_ref[rows, :BRANCH], u3, prev3, dt_ref[rows, :], s_read, s_write, cw_ref,
                       cb_ref, dtb_ref, alog_ref, dexp_ref, ng_ref)
        ys.append(y)
    y_ref[...] = jnp.concatenate(ys, axis=0).astype(BF16)


def _ssd_weight_specs(p):
    return [_const_spec(p[k].shape) for k in ("conv_w", "conv_b", "dt_bias", "a_log", "d_exp", "norm_g")]


def _ssd_weights(p):
    return tuple(p[k] for k in ("conv_w", "conv_b", "dt_bias", "a_log", "d_exp", "norm_g"))


def _ssd_core_prompt(zx, dt, p, bsz, seq, q):
    nt = seq // q
    hp = SSD_HEADS * SSD_HEAD_DIM
    return pl.pallas_call(
        _ssd_prompt_kernel,
        grid=(bsz, nt),
        in_specs=[pl.BlockSpec((q, zx.shape[1]), lambda b, i: (b * nt + i, 0)),
                  pl.BlockSpec((q, LANES), lambda b, i: (b * nt + i, 0))] + _ssd_weight_specs(p),
        out_specs=[pl.BlockSpec((q, BRANCH), lambda b, i: (b * nt + i, 0)),
                   pl.BlockSpec((None, hp, SSD_STATE), lambda b, i: (b, 0, 0))],
        out_shape=[jax.ShapeDtypeStruct((bsz * seq, BRANCH), BF16),
                   jax.ShapeDtypeStruct((bsz, hp, SSD_STATE), F32)],
        scratch_shapes=[pltpu.VMEM((SUBLANES + q, SSD_CONV_DIM), F32), pltpu.VMEM((hp, SSD_STATE), F32)],
        compiler_params=_params(2), name="ssd_core_prompt",
    )(zx, dt, *_ssd_weights(p))


def _ssd_core_sample(zx, dt, prev8, s0, idx, p, seq, nb):
    t = zx.shape[0]
    bsz = t // seq
    hp = SSD_HEADS * SSD_HEAD_DIM
    row = lambda w: pl.BlockSpec((nb * seq, w), lambda i: (i, 0))
    st = pl.BlockSpec((nb, hp, SSD_STATE), lambda i: (i, 0, 0))
    st_in = pl.BlockSpec((None, nb, hp, SSD_STATE), lambda i: (idx, i, 0, 0))
    return pl.pallas_call(
        functools.partial(_ssd_sample_kernel, seq=seq),
        grid=(bsz // nb,),
        in_specs=[row(zx.shape[1]), row(LANES), row(SSD_CONV_DIM), st_in] + _ssd_weight_specs(p),
        out_specs=[row(BRANCH), st],
        out_shape=[jax.ShapeDtypeStruct((t, BRANCH), BF16), jax.ShapeDtypeStruct((bsz, hp, SSD_STATE), F32)],
        compiler_params=_params(1), name="ssd_core_sample",
    )(zx, dt, prev8, s0, *_ssd_weights(p))


def _hg_lower_bound(rows, layer):
    mx = functools.reduce(jnp.maximum, rows)
    es = [jnp.exp(x - mx) for x in rows]
    return sum(es[1:layer + 1]) / sum(es)


def _hg_prompt_kernel(proj_ref, hlb_ref, ng_ref, y_ref, sout_ref, st_scr, *, layer):
    r = proj_ref.shape[0]
    c = BRANCH
    dk = HGRN_KEY_DIM
    blk, sub = HGRN_BLOCK, HGRN_CHUNK
    nblk, nsub = r // blk, blk // sub

    @pl.when(pl.program_id(1) == 0)
    def _():
        st_scr[...] = jnp.zeros_like(st_scr)

    causal = (lax.broadcasted_iota(jnp.int32, (blk, blk), 0)
              >= lax.broadcasted_iota(jnp.int32, (blk, blk), 1))

    def head_body(h, carry):
        lanes = pl.ds(pl.multiple_of(h * dk, dk), dk)

        def proj_part(part):
            return proj_ref[:, pl.ds(pl.multiple_of(part * c + h * dk, dk), dk)]

        lb = _hg_lower_bound([hlb_ref[j:j + 1, lanes] for j in range(DEPTH)], layer)
        f = proj_part(1)
        forget = lb + (1.0 - lb) * jax.nn.sigmoid(f)
        k = ((1.0 - lb) * jax.nn.sigmoid(-f)).reshape(nblk, blk, dk)
        x = _cumsum8(jnp.log(forget).reshape(r // SUBLANES, SUBLANES, dk)).reshape(nblk, blk, dk)
        parts, carry_row = [], None
        for j in range(blk // SUBLANES):
            part = x[:, j * SUBLANES:(j + 1) * SUBLANES, :]
            if carry_row is not None:
                part = part + carry_row
            carry_row = jnp.broadcast_to(part[:, SUBLANES - 1:, :], part.shape)
            parts.append(part)
        gc = jnp.concatenate(parts, axis=1)
        ends = [gc[:, (i + 1) * sub - 1:(i + 1) * sub, :] for i in range(nsub)]
        starts = [jnp.zeros_like(ends[0])] + ends[:-1]
        spread = lambda rows_: jnp.concatenate(
            [jnp.broadcast_to(x_, (nblk, sub, dk)) for x_ in rows_], axis=1)
        b_prev, b_next = spread(starts), spread(ends)
        qi = jax.nn.silu(proj_part(0)).reshape(nblk, blk, dk) * jnp.exp(gc - b_prev)
        qc = qi * jnp.exp(b_prev)
        kd = k * jnp.exp(b_prev - gc)
        ke = kd * jnp.exp(b_next - b_prev)
        kend = ke * jnp.exp(ends[-1] - b_next)
        v = proj_part(2).reshape(nblk, blk, dk)

        st = st_scr[h]
        outs = []
        for b in range(nblk):
            qi_b = qi[b].astype(BF16)
            att_rows = []
            for i in range(nsub):
                keys = []
                for j in range(nsub):
                    rs = slice(j * sub, (j + 1) * sub)
                    if j == i:
                        keys.append(kd[b, rs])
                    elif j < i - 1:
                        keys.append(ke[b, rs] * jnp.exp(starts[i][b] - ends[j][b]))
                    else:
                        keys.append(ke[b, rs])
                keys = jnp.concatenate(keys, axis=0).astype(BF16)
                att_rows.append(_dot_nt(qi_b[i * sub:(i + 1) * sub], keys))
            att = jnp.where(causal, jnp.concatenate(att_rows, axis=0), 0.0).astype(BF16)
            vb = v[b].astype(BF16)
            outs.append(_dot(att, vb) + _dot_nt(qc[b].astype(BF16), st.astype(BF16)))
            st = st * jnp.exp(ends[-1][b]) + _dot_tn(vb, kend[b].astype(BF16))
        st_scr[h] = st
        o = jnp.concatenate(outs, axis=0)
        o = o * lax.rsqrt(jnp.mean(o * o, axis=-1, keepdims=True) + EPS)
        gate = jax.nn.silu(proj_part(3))
        y_ref[:, lanes] = (o * ng_ref[:, lanes] * gate).astype(BF16)
        return carry

    lax.fori_loop(0, HGRN_HEADS, head_body, 0, unroll=True)

    @pl.when(pl.program_id(1) == pl.num_programs(1) - 1)
    def _():
        for h in range(HGRN_HEADS):
            sout_ref[h] = st_scr[h].T


def _hg_seq_kernel(proj_ref, hlb_ref, ng_ref, s0_ref, y_ref, sout_ref,
                   qg_scr, kg_scr, ke_scr, v_scr, dec_scr, o_scr, *, layer, chunk):
    r = proj_ref.shape[0]
    c = BRANCH
    g = r // SUBLANES
    n_chunks = r // chunk
    dk, dv = HGRN_KEY_DIM, HGRN_VAL_DIM
    assert chunk == SUBLANES

    lb = _hg_lower_bound([hlb_ref[j:j + 1, :] for j in range(DEPTH)], layer)
    f = proj_ref[:, c:2 * c]
    forget = lb + (1.0 - lb) * jax.nn.sigmoid(f)
    k = (1.0 - lb) * jax.nn.sigmoid(-f)
    gcum3 = _cumsum8(jnp.log(forget).reshape(g, SUBLANES, c))
    gcum = gcum3.reshape(r, c)
    glast = jnp.broadcast_to(gcum3[:, SUBLANES - 1:, :], gcum3.shape).reshape(r, c)
    qg_scr[...] = jax.nn.silu(proj_ref[:, :c]) * jnp.exp(gcum)
    kg_scr[...] = k * jnp.exp(-gcum)
    ke_scr[...] = k * jnp.exp(glast - gcum)
    dec_scr[...] = jnp.exp(glast)
    v_scr[...] = proj_ref[:, 2 * c:3 * c]

    causal = (lax.broadcasted_iota(jnp.int32, (chunk, chunk), 0)
              >= lax.broadcasted_iota(jnp.int32, (chunk, chunk), 1))

    def chunk_body(ci, carry):
        rs = pl.ds(pl.multiple_of(ci * chunk, chunk), chunk)
        outs = []
        for h in range(HGRN_HEADS):
            ks = slice(h * dk, (h + 1) * dk)
            vs = slice(h * dv, (h + 1) * dv)
            qg = qg_scr[rs, ks].astype(BF16)
            vv = v_scr[rs, vs].astype(BF16)
            st = s0_ref[ci, h]
            att = jnp.where(causal, _dot_nt(qg, kg_scr[rs, ks].astype(BF16)), 0.0)
            outs.append(_dot(att.astype(BF16), vv) + _dot(qg, st.astype(BF16)))
            dec_col = dec_scr[rs, ks].T[:, chunk - 1:]
            sout_ref[ci, h] = st * dec_col + _dot_tn(ke_scr[rs, ks].astype(BF16), vv)
        o_scr[rs, :] = jnp.concatenate(outs, axis=1)
        return carry

    lax.fori_loop(0, n_chunks, chunk_body, 0, unroll=2)

    gate = jax.nn.silu(proj_ref[:, 3 * c:])
    parts = []
    for h in range(HGRN_HEADS):
        o = o_scr[:, h * dv:(h + 1) * dv]
        parts.append(o * lax.rsqrt(jnp.mean(o * o, axis=-1, keepdims=True) + EPS))
    y_ref[...] = (jnp.concatenate(parts, axis=1) * ng_ref[...] * gate).astype(BF16)


def _hg_core_prompt(proj, hlb, ng, layer, bsz, seq, tl):
    nt = seq // tl
    c = BRANCH
    st_shape = (HGRN_HEADS, HGRN_KEY_DIM, HGRN_VAL_DIM)
    return pl.pallas_call(
        functools.partial(_hg_prompt_kernel, layer=layer),
        grid=(bsz, nt),
        in_specs=[pl.BlockSpec((tl, 4 * c), lambda b, i: (b * nt + i, 0)),
                  _const_spec(hlb.shape), _const_spec(ng.shape)],
        out_specs=[pl.BlockSpec((tl, c), lambda b, i: (b * nt + i, 0)),
                   pl.BlockSpec((None,) + st_shape, lambda b, i: (b, 0, 0, 0))],
        out_shape=[jax.ShapeDtypeStruct((bsz * seq, c), BF16),
                   jax.ShapeDtypeStruct((bsz,) + st_shape, F32)],
        scratch_shapes=[pltpu.VMEM((HGRN_HEADS, HGRN_VAL_DIM, HGRN_KEY_DIM), F32)],
        compiler_params=_params(2), name="hg_core_prompt",
    )(proj, hlb, ng)


def _hg_core_sample(proj, s0, idx, hlb, ng, layer, seq, nb):
    t = proj.shape[0]
    bsz = t // seq
    c = BRANCH
    rows = nb * seq
    st_shape = (nb, HGRN_HEADS, HGRN_KEY_DIM, HGRN_VAL_DIM)
    st_spec = pl.BlockSpec(st_shape, lambda i: (i, 0, 0, 0))
    st_in = pl.BlockSpec((None,) + st_shape, lambda i: (idx, i, 0, 0, 0))
    return pl.pallas_call(
        functools.partial(_hg_seq_kernel, layer=layer, chunk=seq),
        grid=(bsz // nb,),
        in_specs=[pl.BlockSpec((rows, 4 * c), lambda i: (i, 0)),
                  _const_spec(hlb.shape), _const_spec(ng.shape), st_in],
        out_specs=[pl.BlockSpec((rows, c), lambda i: (i, 0)), st_spec],
        out_shape=[jax.ShapeDtypeStruct((t, c), BF16), jax.ShapeDtypeStruct(s0.shape[1:], F32)],
        scratch_shapes=[pltpu.VMEM((rows, c), F32)] * 6,
        compiler_params=_params(1), name="hg_core_sample",
    )(proj, hlb, ng, s0)


ROWS_PROJ_IN = 256
ROWS_PROJ_OUT = 512
ROWS_RG_PROMPT = 1024
ROWS_SSD_PROMPT = 2 * SSD_CHUNK
ROWS_HG_PROMPT = 4 * HGRN_BLOCK
ROWS_ATTN_PROMPT = 1024
SEQS_SSD_SAMPLE = 4
SEQS_HG_SAMPLE = 8
SEQS_ATTN_SAMPLE = 4
BATCH_MEM_KV = 2


def _tile(n, target):
    t = min(n, target)
    assert n % t == 0, (n, target)
    return t


def _pad_groups(state, first_row):
    n, k, c = state.shape
    return jnp.pad(state, ((0, 0), (first_row, SUBLANES - first_row - k), (0, 0))).reshape(n * SUBLANES, c)


def _trunk(x, mem_k, mem_v, states, w, bsz, seq, prompt):
    tm = _tile(x.shape[0], ROWS_PROJ_IN)
    rg_conv, rg_h, ssd_conv, ssd_s, hg_s = [], [], [], [], []
    tail = slice(seq - (CONV_W - 1), seq)
    for layer in range(DEPTH):
        kind, idx = layer % N_MIXERS, layer // N_MIXERS
        g = w["norm_g"][layer]
        if kind == 0:
            p = w["rg"][idx]
            if prompt:
                x, utail, htail = _rg_layer_prompt(x, g, p, bsz, seq, _tile(seq, ROWS_RG_PROMPT))
                rg_h.append(htail[:, SUBLANES - 1])
                rg_conv.append(utail[:, :, 0])
                y = None
            else:
                (proj,) = _norm_matmul(x, g[0:1], [p["w_in"]], tm)
                prev8 = _pad_groups(states["rg_conv"][idx], SUBLANES - (CONV_W - 1))
                h0pad = _pad_groups(states["rg_h"][idx][:, None, :], 0)
                y, h = _rg_core_sample(proj, prev8, h0pad, p, tm)
                rg_h.append(h.reshape(bsz, seq, BRANCH)[:, seq - 1])
                rg_conv.append(proj.reshape(bsz, seq, 2 * BRANCH)[:, tail, :BRANCH])
        elif kind == 1:
            p = w["ssd"][idx]
            zx, dt = _norm_matmul(x, g[0:1], [p["w_zx"], p["w_dt"]], tm)
            if prompt:
                y, s_new = _ssd_core_prompt(zx, dt, p, bsz, seq, _tile(seq, ROWS_SSD_PROMPT))
            else:
                prev8 = _pad_groups(states["ssd_conv"][idx], SUBLANES - (CONV_W - 1))
                s0 = states["ssd_s"].reshape(-1, bsz, SSD_HEADS * SSD_HEAD_DIM, SSD_STATE)
                y, s_new = _ssd_core_sample(zx, dt, prev8, s0, idx, p, seq, _tile(bsz, SEQS_SSD_SAMPLE))
            ssd_s.append(s_new.reshape(bsz, SSD_HEADS, SSD_HEAD_DIM, SSD_STATE))
            ssd_conv.append(zx.reshape(bsz, seq, BRANCH + SSD_CONV_DIM)[:, tail, BRANCH:])
        else:
            p = w["hg"][idx]
            (proj,) = _norm_matmul(x, g[0:1], [p["w_in"]], tm)
            if prompt:
                y, s_new = _hg_core_prompt(proj, w["hg_lower_bounds"], p["norm_g"], layer, bsz, seq,
                                           _tile(seq, ROWS_HG_PROMPT))
            else:
                y, s_new = _hg_core_sample(proj, states["hg_s"], idx, w["hg_lower_bounds"], p["norm_g"],
                                           layer, seq, _tile(bsz, SEQS_HG_SAMPLE))
            hg_s.append(s_new)
        if y is not None:
            x = _proj_norm_res(y, p["w_out"], g[1:2], x, _tile(x.shape[0], ROWS_PROJ_OUT))
        if prompt:
            x = _attn_prompt(x, mem_k, mem_v, layer, w["x_w_q"][layer], w["x_w_o"][layer], g,
                             bsz, seq, _tile(seq, ROWS_ATTN_PROMPT))
        else:
            x = _attn_sample(x, mem_k, mem_v, layer, w["x_w_q"][layer], w["x_w_o"][layer], g,
                             seq, _tile(bsz, SEQS_ATTN_SAMPLE))
    return x, jnp.stack(rg_conv), jnp.stack(rg_h), jnp.stack(ssd_conv), jnp.stack(ssd_s), jnp.stack(hg_s)


def kernel(x_prompt, x_sample, mem_prompt, state_rglru_conv, state_rglru_h, state_ssd_conv, state_ssd,
           state_hgrn, cache_mem_k, cache_mem_v, norm_g, mem_norm_g, rg_w_in, rg_conv_w, rg_conv_b, rg_w_a,
           rg_b_a, rg_w_x, rg_b_x, rg_lambda, rg_w_out, ssd_w_in, ssd_conv_w, ssd_conv_b, ssd_dt_bias,
           ssd_a_log, ssd_d, ssd_norm_g, ssd_w_out, hg_w_in, hg_lower_bounds, hg_norm_g, hg_w_out,
           x_w_q, x_w_k, x_w_v, x_w_o):
    bp, sp, d = x_prompt.shape
    bs, ss, _ = x_sample.shape
    n_a, n_b, n_c = rg_w_in.shape[0], ssd_w_in.shape[0], hg_w_in.shape[0]
    pad_heads = lambda v: jnp.pad(v, (0, LANES - SSD_HEADS))[None, :]
    w = {
        "norm_g": norm_g,
        "hg_lower_bounds": hg_lower_bounds,
        "x_w_q": x_w_q.astype(BF16),
        "x_w_o": x_w_o.astype(BF16),
        "rg": [{
            "w_in": rg_w_in[i].astype(BF16),
            "conv_w": rg_conv_w[i], "conv_b": rg_conv_b[i][None, :],
            "w_ax": jnp.concatenate([rg_w_a[i], rg_w_x[i]], axis=-1).astype(BF16),
            "b_a": rg_b_a[i][None, :], "b_x": rg_b_x[i][None, :], "lam": rg_lambda[i][None, :],
            "w_out": rg_w_out[i].astype(BF16),
        } for i in range(n_a)],
        "ssd": [{
            "w_zx": ssd_w_in[i][:, :BRANCH + SSD_CONV_DIM].astype(BF16),
            "w_dt": jnp.pad(ssd_w_in[i][:, BRANCH + SSD_CONV_DIM:], ((0, 0), (0, LANES - SSD_HEADS))).astype(BF16),
            "conv_w": ssd_conv_w[i], "conv_b": ssd_conv_b[i][None, :],
            "dt_bias": pad_heads(ssd_dt_bias[i]), "a_log": pad_heads(ssd_a_log[i]),
            "d_exp": jnp.repeat(ssd_d[i], SSD_HEAD_DIM)[None, :],
            "norm_g": ssd_norm_g[i][None, :],
            "w_out": ssd_w_out[i].astype(BF16),
        } for i in range(n_b)],
        "hg": [{
            "w_in": hg_w_in[i].astype(BF16),
            "norm_g": hg_norm_g[i][None, :],
            "w_out": hg_w_out[i].astype(BF16),
        } for i in range(n_c)],
    }

    mem_k_p, mem_v_p = _mem_kv(mem_prompt.reshape(bp * N_MEM, d), mem_norm_g[:, None, :],
                               x_w_k.astype(BF16), x_w_v.astype(BF16), _tile(bp, BATCH_MEM_KV))
    y_p, rgc_p, rgh_p, sc_p, ss_p, hs_p = _trunk(
        x_prompt.reshape(bp * sp, d), mem_k_p, mem_v_p, None, w, bp, sp, True)
    states = {"rg_conv": state_rglru_conv, "rg_h": state_rglru_h, "ssd_conv": state_ssd_conv,
              "ssd_s": state_ssd, "hg_s": state_hgrn}
    y_s, rgc_s, rgh_s, sc_s, ss_s, hs_s = _trunk(
        x_sample.reshape(bs * ss, d), _kv_flat(cache_mem_k), _kv_flat(cache_mem_v), states, w, bs, ss, False)
    return (y_p.reshape(bp, sp, d), y_s.reshape(bs, ss, d), rgc_p, rgh_p, sc_p, ss_p, hs_p,
            _kv_unflat(mem_k_p), _kv_unflat(mem_v_p), rgc_s, rgh_s, sc_s, ss_s, hs_s)
```
